```python
import math
import functools
import jax
import jax.numpy as jnp
from jax import lax
import numpy as np

D_MODEL = 1024
BATCH = 1
SEQ = 16384
DEPTH = 2
DEC_BATCH = 32
DEC_SEQ = 8
PAST_LEN = 16384
PAGE_SIZE = 128

H_FOX = 6
FOX_DH = 64
D_FOX = H_FOX * FOX_DH
FOX_BLOCK = 128
H_GLA = 6
GLA_DK = 32
GLA_DV = 64
D_GLA = H_GLA * GLA_DV
GLA_RANK = 16
GLA_TAU = 16.0
GLA_CHUNK = 64
SSM_G = 16
SSM_GC = 16
SSM_P = 64
D_SSM = SSM_G * SSM_GC
D_MIX = D_FOX + D_GLA + D_SSM
PEER_KEYS = 128
PEER_N = PEER_KEYS * PEER_KEYS
PEER_HEADS = 8
PEER_TOPK = 16
PEER_DK = 256
PEER_BLOCK = 128
EPS = 1e-6
IN_SIZES = (D_FOX, D_FOX, D_FOX, H_FOX, H_GLA * GLA_DK, H_GLA * GLA_DK, D_GLA, GLA_RANK, D_GLA, D_SSM)
D_IN = sum(IN_SIZES)

kernel_name = 'hymba_fox_gla_s5_peer_step'


def _rmsnorm(x, g):
    xf = x.astype(jnp.float32)
    xf = xf * lax.rsqrt(jnp.mean(xf * xf, axis=-1, keepdims=True) + EPS)
    return xf.astype(x.dtype) * g


def _split_points():
    pts, acc = [], 0
    for s in IN_SIZES[:-1]:
        acc += s
        pts.append(acc)
    return pts


def _fox_prompt(q, k, v, logf):
    B, L, H, dh = q.shape
    nb = L // FOX_BLOCK
    c = jnp.cumsum(logf.astype(jnp.float32), axis=1)
    c_k = c.transpose(0, 2, 1)[:, :, None, :]
    q_blocks = q.reshape(B, nb, FOX_BLOCK, H, dh).swapaxes(0, 1)
    c_blocks = c.reshape(B, nb, FOX_BLOCK, H).swapaxes(0, 1)
    key_pos = jnp.arange(L)
    scale = dh ** -0.5

    def one_block(args):
        i, qb, cb = args
        s = jnp.einsum('bqhd,bkhd->bhqk', qb, k).astype(jnp.float32) * scale
        s = s + cb.transpose(0, 2, 1)[..., None] - c_k
        q_pos = i * FOX_BLOCK + jnp.arange(FOX_BLOCK)
        s = jnp.where(key_pos[None, :] <= q_pos[:, None], s, -jnp.inf)
        p = jax.nn.softmax(s, axis=-1).astype(v.dtype)
        return jnp.einsum('bhqk,bkhd->bqhd', p, v)

    o = lax.map(one_block, (jnp.arange(nb), q_blocks, c_blocks))
    return o.swapaxes(0, 1).reshape(B, L, H * dh)


def _fox_sample(q, k, v, logf, cache_k, cache_v, cache_lf, page_table, layer):
    Bd, T, H, dh = q.shape
    scale = dh ** -0.5
    causal = jnp.arange(T)[None, :] <= jnp.arange(T)[:, None]

    def one_seq(args):
        qb, kb, vb, lfb, pt = args
        kp = cache_k[layer, pt].reshape(-1, H, dh)
        vp = cache_v[layer, pt].reshape(-1, H, dh)
        cp = jnp.cumsum(cache_lf[layer, pt].reshape(-1, H).astype(jnp.float32), axis=0)
        cn = cp[-1] + jnp.cumsum(lfb.astype(jnp.float32), axis=0)
        s_past = jnp.einsum('qhd,khd->hqk', qb, kp).astype(jnp.float32) * scale + cn.T[:, :, None] - cp.T[:, None, :]
        s_new = jnp.einsum('qhd,khd->hqk', qb, kb).astype(jnp.float32) * scale + cn.T[:, :, None] - cn.T[:, None, :]
        s_new = jnp.where(causal, s_new, -jnp.inf)
        p = jax.nn.softmax(jnp.concatenate([s_past, s_new], axis=-1), axis=-1).astype(vb.dtype)
        n_past = kp.shape[0]
        return jnp.einsum('hqk,khd->qhd', p[..., :n_past], vp) + jnp.einsum('hqk,khd->qhd', p[..., n_past:], vb)

    o = lax.map(one_seq, (q, k, v, logf, page_table))
    return o.reshape(Bd, T, H * dh)


def _gla(q, k, v, loga, s0):
    B, L, H, dk = q.shape
    dv = v.shape[-1]
    C = min(GLA_CHUNK, L)
    n = -(-L // C)
    pad = n * C - L

    def chunks(t):
        t = jnp.pad(t.astype(jnp.float32), [(0, 0), (0, pad)] + [(0, 0)] * (t.ndim - 2))
        return t.reshape(B, n, C, *t.shape[2:]).swapaxes(0, 1)

    tri = (jnp.arange(C)[None, :] <= jnp.arange(C)[:, None])[None, :, :, None, None]

    def step(S, args):
        qc, kc, vc, lc = args
        b = jnp.cumsum(lc, axis=1)
        o_inter = jnp.einsum('bthk,bhkv->bthv', qc * jnp.exp(b), S)
        decay = jnp.exp(jnp.where(tri, b[:, :, None] - b[:, None, :], -jnp.inf))
        att = jnp.einsum('bthk,bshk,btshk->bhts', qc, kc, decay)
        o_intra = jnp.einsum('bhts,bshv->bthv', att, vc)
        b_last = b[:, -1]
        S = jnp.exp(b_last)[..., None] * S + jnp.einsum('bshk,bshv->bhkv', kc * jnp.exp(b_last[:, None] - b), vc)
        return S, o_inter + o_intra

    S, o = lax.scan(step, s0.astype(jnp.float32), (chunks(q), chunks(k), chunks(v), chunks(loga)))
    o = o.swapaxes(0, 1).reshape(B, n * C, H, dv)[:, :L]
    return o, S


def _cplx_combine(e1, e2):
    ar1, ai1, br1, bi1 = e1
    ar2, ai2, br2, bi2 = e2
    return (ar1 * ar2 - ai1 * ai2, ar1 * ai2 + ai1 * ar2,
            ar2 * br1 - ai2 * bi1 + br2, ar2 * bi1 + ai2 * br1 + bi2)


def _ssm(u, p, h0_re, h0_im):
    B, L, _ = u.shape
    uf = u.astype(jnp.float32).reshape(B, L, SSM_G, SSM_GC)
    a_re = p['ssm_a_re'].astype(jnp.float32)
    a_im = p['ssm_a_im'].astype(jnp.float32)
    dt = jnp.exp(p['ssm_log_dt'].astype(jnp.float32))[:, None]
    mag = jnp.exp(a_re * dt)
    abar_re, abar_im = mag * jnp.cos(a_im * dt), mag * jnp.sin(a_im * dt)
    den = a_re * a_re + a_im * a_im
    nr, ni = abar_re - 1.0, abar_im
    fr = (nr * a_re + ni * a_im) / den
    fi = (ni * a_re - nr * a_im) / den
    b_re = p['ssm_b_re'].astype(jnp.float32)
    b_im = p['ssm_b_im'].astype(jnp.float32)
    bb_re = fr[..., None] * b_re - fi[..., None] * b_im
    bb_im = fr[..., None] * b_im + fi[..., None] * b_re
    dr = jnp.einsum('blgc,gpc->blgp', uf, bb_re)
    di = jnp.einsum('blgc,gpc->blgp', uf, bb_im)
    h0r = h0_re.astype(jnp.float32)
    h0i = h0_im.astype(jnp.float32)
    dr = dr.at[:, 0].add(abar_re * h0r - abar_im * h0i)
    di = di.at[:, 0].add(abar_re * h0i + abar_im * h0r)
    ar = jnp.broadcast_to(abar_re, dr.shape)
    ai = jnp.broadcast_to(abar_im, dr.shape)
    _, _, hr, hi = lax.associative_scan(_cplx_combine, (ar, ai, dr, di), axis=1)
    y = (jnp.einsum('blgp,gcp->blgc', hr, p['ssm_c_re'].astype(jnp.float32))
         - jnp.einsum('blgp,gcp->blgc', hi, p['ssm_c_im'].astype(jnp.float32))
         + p['ssm_d'].astype(jnp.float32) * uf).reshape(B, L, D_SSM)
    zg = jax.nn.gelu(y)
    out = zg * jax.nn.sigmoid(zg @ p['ssm_w_glu'].astype(jnp.float32) + p['ssm_b_glu'].astype(jnp.float32))
    return out.astype(u.dtype), hr[:, -1], hi[:, -1]


def _peer(xt, w_q, keys, u_tab, v_tab):
    T, D = xt.shape
    nb = -(-T // PEER_BLOCK)
    xpad = jnp.pad(xt, ((0, nb * PEER_BLOCK - T), (0, 0))).reshape(nb, PEER_BLOCK, D)
    half = PEER_DK // 2

    def block(xb):
        q = jnp.einsum('td,dhk->thk', xb, w_q)
        s1 = jnp.einsum('thk,hnk->thn', q[..., :half], keys[:, 0]).astype(jnp.float32)
        s2 = jnp.einsum('thk,hnk->thn', q[..., half:], keys[:, 1]).astype(jnp.float32)
        v1, i1 = lax.top_k(s1, PEER_TOPK)
        v2, i2 = lax.top_k(s2, PEER_TOPK)
        cand = (v1[..., :, None] + v2[..., None, :]).reshape(PEER_BLOCK, PEER_HEADS, PEER_TOPK * PEER_TOPK)
        cidx = (i1[..., :, None] * PEER_KEYS + i2[..., None, :]).reshape(PEER_BLOCK, PEER_HEADS, PEER_TOPK * PEER_TOPK)
        sc, pos = lax.top_k(cand, PEER_TOPK)
        eid = jnp.take_along_axis(cidx, pos, axis=-1)
        g = jax.nn.softmax(sc, axis=-1)
        act = jax.nn.gelu(jnp.einsum('td,thkd->thk', xb, u_tab[eid]).astype(jnp.float32))
        return jnp.einsum('thk,thkd->td', (g * act).astype(xb.dtype), v_tab[eid])

    return lax.map(block, xpad).reshape(nb * PEER_BLOCK, D)[:T]


def _trunk_layer(x, p, fox_attend, gla_s0, ssm_h0_re, ssm_h0_im):
    B, L, _ = x.shape
    xn = _rmsnorm(x, p['g_mix'])
    z = jnp.einsum('bld,de->ble', xn, p['w_in'])
    fq, fk, fv, ff, gq, gk, gv, ga, gr, su = jnp.split(z, _split_points(), axis=-1)
    fq = fq.reshape(B, L, H_FOX, FOX_DH)
    fk = fk.reshape(B, L, H_FOX, FOX_DH)
    fv = fv.reshape(B, L, H_FOX, FOX_DH)
    logf = jax.nn.log_sigmoid(ff.astype(jnp.float32) + p['fox_bf'].astype(jnp.float32))
    fox_o = _rmsnorm(fox_attend(fq, fk, fv, logf).reshape(B, L, D_FOX), p['fox_norm'])
    gq = gq.reshape(B, L, H_GLA, GLA_DK) * (GLA_DK ** -0.5)
    gk = gk.reshape(B, L, H_GLA, GLA_DK)
    gv = gv.reshape(B, L, H_GLA, GLA_DV)
    za = jnp.einsum('blr,re->ble', ga, p['gla_wa2']) + p['gla_ba']
    loga = (jax.nn.log_sigmoid(za.astype(jnp.float32)) / GLA_TAU).reshape(B, L, H_GLA, GLA_DK)
    gla_o, gla_s = _gla(gq, gk, gv, loga, gla_s0)
    gla_o = _rmsnorm(gla_o.astype(x.dtype), p['gla_norm']) * jax.nn.silu(gr.reshape(B, L, H_GLA, GLA_DV))
    ssm_o, h_re, h_im = _ssm(su, p, ssm_h0_re, ssm_h0_im)
    mix = jnp.concatenate([fox_o, gla_o.reshape(B, L, D_GLA).astype(fox_o.dtype), ssm_o.astype(fox_o.dtype)], axis=-1)
    x = x + jnp.einsum('ble,ed->bld', mix, p['w_out'])
    xf = _rmsnorm(x, p['g_ffn']).reshape(B * L, D_MODEL)
    x = x + _peer(xf, p['peer_wq'], p['peer_keys'], p['peer_u'], p['peer_v']).reshape(B, L, D_MODEL)
    return x, (fk, fv, logf, gla_s, h_re, h_im)


def setup_inputs(seed: int = 0) -> dict:
    key = jax.random.key(seed)
    ks = iter(jax.random.split(key, 40))
    f32 = jnp.float32

    def nrm(shape, s):
        return jax.random.normal(next(ks), shape, f32) * s

    n_pages = PAST_LEN // PAGE_SIZE
    n_used = DEC_BATCH * n_pages
    n_pool = n_used + max(1, n_used // 4)
    x_prompt = nrm((BATCH, SEQ, D_MODEL), 1.0)
    x_sample = nrm((DEC_BATCH, DEC_SEQ, D_MODEL), 1.0)
    page_table = jax.random.permutation(next(ks), n_pool)[:n_used].reshape(DEC_BATCH, n_pages).astype(jnp.int32)
    cache_fox_k = nrm((DEPTH, n_pool, PAGE_SIZE, H_FOX, FOX_DH), 1.0)
    cache_fox_v = nrm((DEPTH, n_pool, PAGE_SIZE, H_FOX, FOX_DH), 1.0)
    cache_fox_logf = jax.nn.log_sigmoid(5.0 + nrm((DEPTH, n_pool, PAGE_SIZE, H_FOX), 1.0))
    state_gla = nrm((DEPTH, DEC_BATCH, H_GLA, GLA_DK, GLA_DV), 0.5)
    state_ssm_re = nrm((DEPTH, DEC_BATCH, SSM_G, SSM_P), 0.1)
    state_ssm_im = nrm((DEPTH, DEC_BATCH, SSM_G, SSM_P), 0.1)
    g_mix = 1.0 + nrm((DEPTH, D_MODEL), 0.02)
    w_in = nrm((DEPTH, D_MODEL, D_IN), D_MODEL ** -0.5)
    fox_bf = 5.0 + nrm((DEPTH, H_FOX), 0.5)
    fox_norm = 1.0 + nrm((DEPTH, D_FOX), 0.02)
    gla_wa2 = nrm((DEPTH, GLA_RANK, H_GLA * GLA_DK), GLA_RANK ** -0.5)
    gla_ba = nrm((DEPTH, H_GLA * GLA_DK), 0.1)
    gla_norm = 1.0 + nrm((DEPTH, GLA_DV), 0.02)
    ssm_a_re = -0.5 + nrm((DEPTH, SSM_G, SSM_P), 0.01)
    ssm_a_im = jnp.pi * jnp.arange(SSM_P, dtype=f32) + nrm((DEPTH, SSM_G, SSM_P), 0.01)
    ssm_log_dt = jax.random.uniform(next(ks), (DEPTH, SSM_G), f32, math.log(1e-3), math.log(1e-1))
    ssm_b_re = nrm((DEPTH, SSM_G, SSM_P, SSM_GC), (2 * SSM_GC) ** -0.5)
    ssm_b_im = nrm((DEPTH, SSM_G, SSM_P, SSM_GC), (2 * SSM_GC) ** -0.5)
    ssm_c_re = nrm((DEPTH, SSM_G, SSM_GC, SSM_P), SSM_P ** -0.5)
    ssm_c_im = nrm((DEPTH, SSM_G, SSM_GC, SSM_P), SSM_P ** -0.5)
    ssm_d = nrm((DEPTH, SSM_G, SSM_GC), 1.0)
    ssm_w_glu = nrm((DEPTH, D_SSM, D_SSM), D_SSM ** -0.5)
    ssm_b_glu = nrm((DEPTH, D_SSM), 0.02)
    w_out = nrm((DEPTH, D_MIX, D_MODEL), D_MIX ** -0.5)
    g_ffn = 1.0 + nrm((DEPTH, D_MODEL), 0.02)
    peer_wq = nrm((DEPTH, D_MODEL, PEER_HEADS, PEER_DK), D_MODEL ** -0.5)
    peer_keys = nrm((DEPTH, PEER_HEADS, 2, PEER_KEYS, PEER_DK // 2), (PEER_DK // 2) ** -0.5)
    peer_u = nrm((DEPTH, PEER_N, D_MODEL), D_MODEL ** -0.5)
    peer_v = nrm((DEPTH, PEER_N, D_MODEL), 0.1)
    g_final = 1.0 + nrm((D_MODEL,), 0.02)
    return {'x_prompt': x_prompt, 'x_sample': x_sample,
            'cache_fox_k': cache_fox_k, 'cache_fox_v': cache_fox_v, 'cache_fox_logf': cache_fox_logf,
            'state_gla': state_gla, 'state_ssm_re': state_ssm_re, 'state_ssm_im': state_ssm_im,
            'page_table': page_table,
            'g_mix': g_mix, 'w_in': w_in, 'fox_bf': fox_bf, 'fox_norm': fox_norm,
            'gla_wa2': gla_wa2, 'gla_ba': gla_ba, 'gla_norm': gla_norm,
            'ssm_a_re': ssm_a_re, 'ssm_a_im': ssm_a_im, 'ssm_log_dt': ssm_log_dt,
            'ssm_b_re': ssm_b_re, 'ssm_b_im': ssm_b_im, 'ssm_c_re': ssm_c_re, 'ssm_c_im': ssm_c_im,
            'ssm_d': ssm_d, 'ssm_w_glu': ssm_w_glu, 'ssm_b_glu': ssm_b_glu,
            'w_out': w_out, 'g_ffn': g_ffn,
            'peer_wq': peer_wq, 'peer_keys': peer_keys, 'peer_u': peer_u, 'peer_v': peer_v,
            'g_final': g_final}


def reference(x_prompt, x_sample, cache_fox_k, cache_fox_v, cache_fox_logf, state_gla, state_ssm_re,
              state_ssm_im, page_table, g_mix, w_in, fox_bf, fox_norm, gla_wa2, gla_ba, gla_norm,
              ssm_a_re, ssm_a_im, ssm_log_dt, ssm_b_re, ssm_b_im, ssm_c_re, ssm_c_im, ssm_d,
              ssm_w_glu, ssm_b_glu, w_out, g_ffn, peer_wq, peer_keys, peer_u, peer_v, g_final):
    def layer_params(l):
        return {'g_mix': g_mix[l], 'w_in': w_in[l], 'fox_bf': fox_bf[l], 'fox_norm': fox_norm[l],
                'gla_wa2': gla_wa2[l], 'gla_ba': gla_ba[l], 'gla_norm': gla_norm[l],
                'ssm_a_re': ssm_a_re[l], 'ssm_a_im': ssm_a_im[l], 'ssm_log_dt': ssm_log_dt[l],
                'ssm_b_re': ssm_b_re[l], 'ssm_b_im': ssm_b_im[l], 'ssm_c_re': ssm_c_re[l],
                'ssm_c_im': ssm_c_im[l], 'ssm_d': ssm_d[l], 'ssm_w_glu': ssm_w_glu[l],
                'ssm_b_glu': ssm_b_glu[l], 'w_out': w_out[l], 'g_ffn': g_ffn[l],
                'peer_wq': peer_wq[l], 'peer_keys': peer_keys[l], 'peer_u': peer_u[l], 'peer_v': peer_v[l]}

    bp = x_prompt.shape[0]
    xp, xs = x_prompt, x_sample
    p_rows, s_rows = [], []
    for l in range(DEPTH):
        p = layer_params(l)
        xp, st_p = _trunk_layer(xp, p, _fox_prompt,
                                jnp.zeros((bp, H_GLA, GLA_DK, GLA_DV), jnp.float32),
                                jnp.zeros((bp, SSM_G, SSM_P), jnp.float32),
                                jnp.zeros((bp, SSM_G, SSM_P), jnp.float32))
        fox_s = functools.partial(_fox_sample, cache_k=cache_fox_k, cache_v=cache_fox_v,
                                  cache_lf=cache_fox_logf, page_table=page_table, layer=l)
        xs, st_s = _trunk_layer(xs, p, fox_s, state_gla[l], state_ssm_re[l], state_ssm_im[l])
        p_rows.append(st_p)
        s_rows.append(st_s)
    fox_k_p, fox_v_p, fox_lf_p, gla_p, ssm_re_p, ssm_im_p = [jnp.stack(t) for t in zip(*p_rows)]
    fox_k_s, fox_v_s, fox_lf_s, gla_s, ssm_re_s, ssm_im_s = [jnp.stack(t) for t in zip(*s_rows)]
    y_prompt = _rmsnorm(xp, g_final)
    y_sample = _rmsnorm(xs, g_final)
    return (y_prompt, y_sample, fox_k_p, fox_v_p, fox_lf_p, gla_p, ssm_re_p, ssm_im_p,
            fox_k_s, fox_v_s, fox_lf_s, gla_s, ssm_re_s, ssm_im_s)
```

```python
import functools

import numpy as np
import jax
import jax.numpy as jnp
from jax import lax
from jax.experimental import pallas as pl
from jax.experimental.pallas import tpu as pltpu

f32 = jnp.float32
bf16 = jnp.bfloat16
i32 = jnp.int32

EPS = 1e-6
LANES = 128
VMEM_LIMIT = 56 * 1024 * 1024

H_FOX, FOX_DH = 6, 64
D_FOX = H_FOX * FOX_DH
H_GLA, GLA_DK, GLA_DV = 6, 32, 64
D_GK = H_GLA * GLA_DK
D_GLA = H_GLA * GLA_DV
GLA_RANK = 16
GLA_TAU = 16.0
SSM_G, SSM_GC, SSM_P = 16, 16, 64
D_SSM = SSM_G * SSM_GC
D_STATE = SSM_G * SSM_P
PEER_KEYS, PEER_HEADS, PEER_TOPK, PEER_DK = 128, 8, 16, 256
PEER_SLOTS = PEER_HEADS * PEER_TOPK

C_FQ, C_FK, C_FV = 0, 384, 768
C_GQ, C_GK = 1152, 1408
C_GV, C_GR, C_SU, C_TAIL = 1664, 2048, 2432, 2688
D_INP = 2816
TAIL_FF, TAIL_GA = 0, 8

TOK_TILE = 256
FOX_TILE = 512
FOX_PAGES = 4
GLA_STEP = 512
GLA_CHUNK = 16
PEER_ETILE = 1024


def _dot(a, b):
    return jnp.dot(a, b, preferred_element_type=f32)


def _dot_nt(a, b):
    return lax.dot_general(a, b, (((1,), (1,)), ((), ())), preferred_element_type=f32)


def _dot_tn(a, b):
    return lax.dot_general(a, b, (((0,), (0,)), ((), ())), preferred_element_type=f32)


def _split3(x):
    h = x.astype(bf16)
    r = x - h.astype(f32)
    m = r.astype(bf16)
    l = (r - m.astype(f32)).astype(bf16)
    return h, m, l


def _dot3_lhs01(sel, x):
    h, m, l = _split3(x)
    return _dot(sel, h) + _dot(sel, m) + _dot(sel, l)


def _dot3_rhs01(x, sel):
    h, m, l = _split3(x)
    return _dot(h, sel) + _dot(m, sel) + _dot(l, sel)


def _log_sigmoid(x):
    return -(jnp.maximum(-x, 0.0) + jnp.log1p(jnp.exp(-jnp.abs(x))))


def _iota(shape, dim):
    return lax.broadcasted_iota(i32, shape, dim)


def _div_pow2(x, n):
    assert n & (n - 1) == 0
    return jnp.right_shift(x, n.bit_length() - 1)


def _mod_pow2(x, n):
    assert n & (n - 1) == 0
    return jnp.bitwise_and(x, n - 1)


def _params(sem, vmem=None):
    return pltpu.CompilerParams(dimension_semantics=sem, vmem_limit_bytes=vmem)


def _inproj_body(x_ref, g_ref, w_ref, bfp_ref, wa2_ref, ba_ref,
                 fq_ref, fkb_ref, fvb_ref, fk_ref, fv_ref, lf_ref, c_ref,
                 gq_ref, gk_ref, gv_ref, la_ref, gr_ref, su_ref, carry_ref):
    @pl.when(pl.program_id(0) == 0)
    def _init():
        carry_ref[...] = jnp.zeros_like(carry_ref)

    x = x_ref[...]
    tt = x.shape[0]
    xn = x * lax.rsqrt(jnp.mean(x * x, axis=-1, keepdims=True) + EPS) * g_ref[...]
    z = _dot(xn.astype(bf16), w_ref[...])
    fq_ref[...] = (z[:, C_FQ:C_FQ + D_FOX] * (FOX_DH ** -0.5)).astype(bf16)
    fk = z[:, C_FK:C_FK + D_FOX]
    fv = z[:, C_FV:C_FV + D_FOX]
    fk_ref[...] = fk
    fv_ref[...] = fv
    fkb_ref[...] = fk.astype(bf16)
    fvb_ref[...] = fv.astype(bf16)
    gq_ref[...] = z[:, C_GQ:C_GQ + D_GK] * (GLA_DK ** -0.5)
    gk_ref[...] = z[:, C_GK:C_GK + D_GK]
    gv_ref[...] = z[:, C_GV:C_GV + D_GLA]
    gr_ref[...] = z[:, C_GR:C_GR + D_GLA]
    su_ref[...] = z[:, C_SU:C_SU + D_SSM]
    tail = z[:, C_TAIL:C_TAIL + LANES]
    lane = _iota((tt, LANES), 1)
    lf = jnp.where(lane < H_FOX, _log_sigmoid(tail + bfp_ref[...]), 0.0)
    lf_ref[...] = lf
    tri = jnp.where(_iota((tt, tt), 0) >= _iota((tt, tt), 1), 1.0, 0.0).astype(bf16)
    cs = _dot3_lhs01(tri, lf) + carry_ref[...]
    c_ref[...] = cs
    carry_ref[...] = cs[tt - 1:tt, :]
    za = _dot(tail.astype(bf16), wa2_ref[...]) + ba_ref[...]
    la_ref[...] = _log_sigmoid(za) * (1.0 / GLA_TAU)


def _inproj(x, g_mix, w_p, bfp, wa2p, ba):
    t = x.shape[0]
    tt = TOK_TILE
    row = lambda n: pl.BlockSpec((tt, n), lambda i: (i, 0))
    const = lambda a: pl.BlockSpec(a.shape, lambda i: (0,) * a.ndim)
    outs = [
        (D_FOX, bf16), (D_FOX, bf16), (D_FOX, bf16), (D_FOX, f32), (D_FOX, f32),
        (LANES, f32), (LANES, f32),
        (D_GK, f32), (D_GK, f32), (D_GLA, f32), (D_GK, f32), (D_GLA, f32), (D_SSM, f32),
    ]
    return pl.pallas_call(
        _inproj_body,
        grid=(t // tt,),
        in_specs=[row(x.shape[1]), const(g_mix), const(w_p), const(bfp), const(wa2p), const(ba)],
        out_specs=[row(n) for n, _ in outs],
        out_shape=[jax.ShapeDtypeStruct((t, n), d) for n, d in outs],
        scratch_shapes=[pltpu.VMEM((1, LANES), f32)],
        compiler_params=_params(("arbitrary",), VMEM_LIMIT),
        name="inproj",
    )(x, g_mix, w_p, bfp, wa2p, ba)


def _fox_prompt_body(qi_ref, ki_ref, q_ref, k_ref, v_ref, cc_ref, cr_ref, o_ref,
                     m_ref, l_ref, acc_ref):
    p = pl.program_id(1)
    qi = qi_ref[p]
    ki = ki_ref[p]

    @pl.when(ki == 0)
    def _init():
        m_ref[...] = jnp.full_like(m_ref, -jnp.inf)
        l_ref[...] = jnp.zeros_like(l_ref)
        acc_ref[...] = jnp.zeros_like(acc_ref)

    def step(diagonal):
        q = q_ref[...]
        k = k_ref[...]
        v = v_ref[...]
        tq, tk = q.shape[0], k.shape[0]
        lane = _iota((1, LANES), 1)
        acc = acc_ref[...]
        new_acc = acc
        for hh in range(2):
            sel = (lane >= FOX_DH) if hh else (lane < FOX_DH)
            s = _dot_nt(jnp.where(sel, q, jnp.zeros_like(q)), k)
            s = s + cc_ref[:, hh:hh + 1] - cr_ref[hh:hh + 1, :]
            if diagonal:
                s = jnp.where(_iota((tq, tk), 0) >= _iota((tq, tk), 1), s, -jnp.inf)
            m_old = m_ref[hh]
            m_new = jnp.maximum(m_old, jnp.max(s, axis=-1, keepdims=True))
            pr = jnp.exp(s - m_new)
            alpha = jnp.exp(m_old - m_new)
            l_ref[hh] = alpha * l_ref[hh] + jnp.sum(pr, axis=-1, keepdims=True)
            m_ref[hh] = m_new
            pv = _dot(pr.astype(bf16), v)
            new_acc = jnp.where(sel, alpha * acc + pv, new_acc)
        acc_ref[...] = new_acc

    @pl.when(ki < qi)
    def _off():
        step(False)

    @pl.when(ki == qi)
    def _diag():
        step(True)
        lane = _iota((1, LANES), 1)
        inv = jnp.where(lane < FOX_DH, 1.0 / l_ref[0], 1.0 / l_ref[1])
        o_ref[...] = acc_ref[...] * inv


def _fox_prompt(fq, fkb, fvb, ccol, crow, seq):
    tq = FOX_TILE
    nq = seq // tq
    qi_tbl = np.concatenate([np.full(i + 1, i) for i in range(nq)]).astype(np.int32)
    ki_tbl = np.concatenate([np.arange(i + 1) for i in range(nq)]).astype(np.int32)
    grid_spec = pltpu.PrefetchScalarGridSpec(
        num_scalar_prefetch=2,
        grid=(H_FOX // 2, len(qi_tbl)),
        in_specs=[
            pl.BlockSpec((tq, LANES), lambda h, p, qi, ki: (qi[p], h)),
            pl.BlockSpec((tq, LANES), lambda h, p, qi, ki: (ki[p], h)),
            pl.BlockSpec((tq, LANES), lambda h, p, qi, ki: (ki[p], h)),
            pl.BlockSpec((None, tq, 2), lambda h, p, qi, ki: (h, qi[p], 0)),
            pl.BlockSpec((None, 2, tq), lambda h, p, qi, ki: (h, 0, ki[p])),
        ],
        out_specs=pl.BlockSpec((tq, LANES), lambda h, p, qi, ki: (qi[p], h)),
        scratch_shapes=[pltpu.VMEM((2, tq, 1), f32), pltpu.VMEM((2, tq, 1), f32),
                        pltpu.VMEM((tq, LANES), f32)],
    )
    return pl.pallas_call(
        _fox_prompt_body,
        grid_spec=grid_spec,
        out_shape=jax.ShapeDtypeStruct((seq, D_FOX), f32),
        compiler_params=_params(("arbitrary", "arbitrary"), VMEM_LIMIT),
        name="fox_prompt",
    )(jnp.asarray(qi_tbl), jnp.asarray(ki_tbl), fq, fkb, fvb, ccol, crow)


def _expand_heads(x, rows):
    return jnp.concatenate(
        [jnp.broadcast_to(x[h:h + 1, :], (rows, x.shape[1])) for h in range(H_FOX)], axis=0)


def _fox_sample_body(pt_ref, q_ref, kn_ref, vn_ref, lfn_ref, *rest):
    del pt_ref
    npg = FOX_PAGES
    kp_refs = rest[0:npg]
    vp_refs = rest[npg:2 * npg]
    lfp_refs = rest[2 * npg:3 * npg]
    o_ref = rest[3 * npg]
    m_ref, l_ref, acc_ref, car_ref, qbd_ref, cq_ref = rest[3 * npg + 1:]
    j = pl.program_id(1)
    t = q_ref.shape[0]
    hq = H_FOX * t
    page = kp_refs[0].shape[0]

    def update(s, v):
        m_old = m_ref[...]
        m_new = jnp.maximum(m_old, jnp.max(s, axis=-1, keepdims=True))
        pr = jnp.exp(s - m_new)
        alpha = jnp.exp(m_old - m_new)
        l_ref[...] = alpha * l_ref[...] + jnp.sum(pr, axis=-1, keepdims=True)
        m_ref[...] = m_new
        acc_ref[...] = alpha * acc_ref[...] + _dot(pr.astype(bf16), v)

    @pl.when(j == 0)
    def _new_tokens():
        m_ref[...] = jnp.full_like(m_ref, -jnp.inf)
        l_ref[...] = jnp.zeros_like(l_ref)
        acc_ref[...] = jnp.zeros_like(acc_ref)
        car_ref[...] = jnp.zeros_like(car_ref)
        q = q_ref[...]
        q6 = jnp.concatenate([q] * H_FOX, axis=0)
        keep = _div_pow2(_iota((hq, D_FOX), 0), t) == _div_pow2(_iota((hq, D_FOX), 1), FOX_DH)
        qbd_ref[...] = jnp.where(keep, q6, 0.0).astype(bf16)
        cum = lfn_ref[...]
        lane8 = _iota(cum.shape, 1)
        sh = 1
        while sh < t:
            cum = cum + jnp.where(lane8 >= sh, pltpu.roll(cum, sh, 1), 0.0)
            sh *= 2
        cum_rows = _expand_heads(cum, t)
        qpos = _mod_pow2(_iota((hq, LANES), 0), t)
        kpos = _iota((hq, LANES), 1)
        cq = jnp.sum(jnp.where(kpos == qpos, cum_rows, 0.0), axis=-1, keepdims=True)
        cq_ref[...] = cq
        zpad = jnp.zeros((LANES - t, D_FOX), f32)
        kn = jnp.concatenate([kn_ref[...], zpad], axis=0).astype(bf16)
        vn = jnp.concatenate([vn_ref[...], zpad], axis=0).astype(bf16)
        s = _dot_nt(qbd_ref[...], kn) + cq - cum_rows
        s = jnp.where(kpos <= qpos, s, -jnp.inf)
        update(s, vn)

    upper = jnp.where(_iota((page, page), 0) > _iota((page, page), 1), 1.0, 0.0).astype(bf16)
    for g in range(npg):
        lfp = lfp_refs[g][...]
        suffix = _dot3_rhs01(lfp, upper) + car_ref[...]
        car_ref[...] = car_ref[...] + jnp.sum(lfp, axis=-1, keepdims=True)
        kp = kp_refs[g][...].astype(bf16)
        vp = vp_refs[g][...].astype(bf16)
        s = _dot_nt(qbd_ref[...], kp) + cq_ref[...] + _expand_heads(suffix, t)
        update(s, vp)

    @pl.when(j == pl.num_programs(1) - 1)
    def _finish():
        o48 = acc_ref[...] / l_ref[...]
        head = _div_pow2(_iota((t, D_FOX), 1), FOX_DH)
        out = jnp.zeros((t, D_FOX), f32)
        for h in range(H_FOX):
            out = out + jnp.where(head == h, o48[h * t:(h + 1) * t, :], 0.0)
        o_ref[...] = out


def _fox_sample(q_s, k_s, v_s, lfn_t, cache_k, cache_v, cache_lft, page_table, layer, t):
    nb, n_pages = page_table.shape
    page = cache_k.shape[2]
    npg = FOX_PAGES
    hq = H_FOX * t
    seq_spec = pl.BlockSpec((t, D_FOX), lambda b, j, pt: (b, 0))

    def page_spec(shape, g):
        return pl.BlockSpec((None, None) + shape,
                            lambda b, j, pt: (layer, pt[b, n_pages - 1 - (j * npg + g)], 0, 0))

    grid_spec = pltpu.PrefetchScalarGridSpec(
        num_scalar_prefetch=1,
        grid=(nb, n_pages // npg),
        in_specs=[seq_spec, seq_spec, seq_spec,
                  pl.BlockSpec((None, 8, LANES), lambda b, j, pt: (b, 0, 0))]
        + [page_spec((page, D_FOX), g) for g in range(npg)]
        + [page_spec((page, D_FOX), g) for g in range(npg)]
        + [page_spec((8, page), g) for g in range(npg)],
        out_specs=seq_spec,
        scratch_shapes=[pltpu.VMEM((hq, 1), f32), pltpu.VMEM((hq, 1), f32),
                        pltpu.VMEM((hq, D_FOX), f32), pltpu.VMEM((8, 1), f32),
                        pltpu.VMEM((hq, D_FOX), bf16), pltpu.VMEM((hq, 1), f32)],
    )
    return pl.pallas_call(
        _fox_sample_body,
        grid_spec=grid_spec,
        out_shape=jax.ShapeDtypeStruct((nb * t, D_FOX), f32),
        compiler_params=_params(("arbitrary", "arbitrary"), VMEM_LIMIT),
        name="fox_sample",
    )(page_table, q_s, k_s, v_s, lfn_t, *([cache_k] * npg), *([cache_v] * npg), *([cache_lft] * npg))


def _gla_body(q_ref, k_ref, la_ref, v_ref, s0_ref, e_ref, mask_ref, o_ref, sout_ref, st_ref, *, chunk):
    i = pl.program_id(1)

    @pl.when(i == 0)
    def _init():
        st_ref[...] = s0_ref[...]

    c = chunk
    n_chunks = q_ref.shape[0] // c
    row = _iota((c, D_GK), 0)

    def one_chunk(ci, carry):
        r0 = pl.multiple_of(ci * c, c)
        q = q_ref[pl.ds(r0, c), :]
        k = k_ref[pl.ds(r0, c), :]
        v = v_ref[pl.ds(r0, c), :]
        b = la_ref[pl.ds(r0, c), :]
        sh = 1
        while sh < c:
            b = b + jnp.where(row >= sh, pltpu.roll(b, sh, 0), 0.0)
            sh *= 2
        st = st_ref[...]
        o = _dot_nt((q * jnp.exp(b)).astype(bf16), st.astype(bf16))
        pieces = []
        for s in range(c):
            d = jnp.where(row >= s, b - b[s:s + 1, :], -jnp.inf)
            pieces.append(q * k[s:s + 1, :] * jnp.exp(d))
        pst = jnp.concatenate(pieces, axis=0)
        hi = pst.astype(bf16)
        lo = (pst - hi.astype(f32)).astype(bf16)
        att = _dot(hi, e_ref[...]) + _dot(lo, e_ref[...])
        for s in range(c):
            o = o + att[s * c:(s + 1) * c, :] * v[s:s + 1, :]
        o_ref[pl.ds(r0, c), :] = o
        b_last = b[c - 1:c, :]
        kd = (k * jnp.exp(b_last - b)).astype(bf16)
        st_ref[...] = st * jnp.exp(b_last) + _dot_tn(v.astype(bf16), kd) * mask_ref[...]
        return carry

    lax.fori_loop(0, n_chunks, one_chunk, 0)

    @pl.when(i == pl.num_programs(1) - 1)
    def _finish():
        sout_ref[...] = st_ref[...]


def _gla(gq, gk, la, gv, s0t, e_mat, mask_t, *, row0, nseq, steps, rows_per_step, chunk):
    blk0 = row0 // rows_per_step
    rows = lambda n: pl.BlockSpec((rows_per_step, n), lambda s, i: (blk0 + s * steps + i, 0))
    orow = pl.BlockSpec((rows_per_step, D_GLA), lambda s, i: (s * steps + i, 0))
    st_spec = pl.BlockSpec((None, D_GLA, D_GK), lambda s, i: (s, 0, 0))
    const = lambda a: pl.BlockSpec(a.shape, lambda s, i: (0,) * a.ndim)
    return pl.pallas_call(
        functools.partial(_gla_body, chunk=chunk),
        grid=(nseq, steps),
        in_specs=[rows(D_GK), rows(D_GK), rows(D_GK), rows(D_GLA), st_spec, const(e_mat), const(mask_t)],
        out_specs=[orow, st_spec],
        out_shape=[jax.ShapeDtypeStruct((nseq * steps * rows_per_step, D_GLA), f32),
                   jax.ShapeDtypeStruct((nseq, D_GLA, D_GK), f32)],
        scratch_shapes=[pltpu.VMEM((D_GLA, D_GK), f32)],
        compiler_params=_params(("arbitrary", "arbitrary"), VMEM_LIMIT),
        name="gla",
    )(gq, gk, la, gv, s0t, e_mat, mask_t)


def _ssm_prep_body(are_ref, aim_ref, ldt_ref, bre_ref, bim_ref,
                   pre_ref, pim_ref, bbre_ref, bbim_ref):
    a_re = are_ref[...]
    a_im = aim_ref[...]
    dt = jnp.exp(ldt_ref[...])
    mag = jnp.exp(a_re * dt)
    abar_re = mag * jnp.cos(a_im * dt)
    abar_im = mag * jnp.sin(a_im * dt)
    den = a_re * a_re + a_im * a_im
    nr, ni = abar_re - 1.0, abar_im
    fr = (nr * a_re + ni * a_im) / den
    fi = (ni * a_re - nr * a_im) / den
    bbre_ref[...] = fr * bre_ref[...] - fi * bim_ref[...]
    bbim_ref[...] = fr * bim_ref[...] + fi * bre_ref[...]
    n = pre_ref.shape[0]
    pre_ref[0:1, :] = abar_re
    pim_ref[0:1, :] = abar_im
    m = 1
    while m < n:
        tr = pre_ref[m - 1:m, :]
        ti = pim_ref[m - 1:m, :]
        xr = pre_ref[0:m, :]
        xi = pim_ref[0:m, :]
        pre_ref[m:2 * m, :] = xr * tr - xi * ti
        pim_ref[m:2 * m, :] = xr * ti + xi * tr
        m *= 2


def _ssm_prep(a_re, a_im, log_dt, bt_re, bt_im, n_pow):
    outs = [(n_pow, D_STATE), (n_pow, D_STATE), (SSM_GC, D_STATE), (SSM_GC, D_STATE)]
    return pl.pallas_call(
        _ssm_prep_body,
        out_shape=[jax.ShapeDtypeStruct(s, f32) for s in outs],
        name="ssm_prep",
    )(a_re, a_im, log_dt, bt_re, bt_im)


def _ssm_body(u_ref, wb_ref, pre_ref, pim_ref, hin_re_ref, hin_im_ref, wc_ref, d_ref, wg_ref, bg_ref,
              o_ref, hre_ref, him_ref, car_re_ref, car_im_ref, *, seg, tile_carry):
    i = pl.program_id(0)
    u = u_ref[...]
    tt = u.shape[0]
    d = _dot(u.astype(bf16), wb_ref[...])
    xr = d[:, :D_STATE]
    xi = d[:, D_STATE:]
    pos = _mod_pow2(_iota((tt, D_STATE), 0), seg)
    sh = 1
    while sh < seg:
        ar = pre_ref[sh - 1:sh, :]
        ai = pim_ref[sh - 1:sh, :]
        sr = jnp.where(pos >= sh, pltpu.roll(xr, sh, 0), 0.0)
        si = jnp.where(pos >= sh, pltpu.roll(xi, sh, 0), 0.0)
        xr, xi = xr + ar * sr - ai * si, xi + ar * si + ai * sr
        sh *= 2
    if tile_carry:
        @pl.when(i == 0)
        def _init():
            car_re_ref[...] = hin_re_ref[...]
            car_im_ref[...] = hin_im_ref[...]
        h0r = car_re_ref[...]
        h0i = car_im_ref[...]
    else:
        h0r = hin_re_ref[...]
        h0i = hin_im_ref[...]
    pr = pre_ref[...]
    pi = pim_ref[...]
    hr = xr + pr * h0r - pi * h0i
    hi = xi + pr * h0i + pi * h0r
    if tile_carry:
        car_re_ref[...] = hr[tt - 1:tt, :]
        car_im_ref[...] = hi[tt - 1:tt, :]
        hre_ref[...] = hr[tt - 1:tt, :]
        him_ref[...] = hi[tt - 1:tt, :]
    else:
        hre_ref[...] = hr
        him_ref[...] = hi
    hcat = jnp.concatenate([hr, hi], axis=1).astype(bf16)
    y = _dot(hcat, wc_ref[...]) + d_ref[...] * u
    zg = jax.nn.gelu(y)
    o_ref[...] = (zg * jax.nn.sigmoid(_dot(zg.astype(bf16), wg_ref[...]) + bg_ref[...])).astype(bf16)


def _ssm(su, wb, p_re, p_im, hin_re, hin_im, wc, dvec, wglu, bglu, *, row0, n_rows, seg, tile_carry):
    tt = p_re.shape[0]
    blk0 = row0 // tt
    const = lambda a: pl.BlockSpec(a.shape, lambda i: (0,) * a.ndim)
    if tile_carry:
        h_spec = const(hin_re)
        hout_spec = pl.BlockSpec((1, D_STATE), lambda i: (0, 0))
        hout_shape = jax.ShapeDtypeStruct((1, D_STATE), f32)
    else:
        h_spec = pl.BlockSpec((tt, D_STATE), lambda i: (i, 0))
        hout_spec = h_spec
        hout_shape = jax.ShapeDtypeStruct((n_rows, D_STATE), f32)
    return pl.pallas_call(
        functools.partial(_ssm_body, seg=seg, tile_carry=tile_carry),
        grid=(n_rows // tt,),
        in_specs=[pl.BlockSpec((tt, D_SSM), lambda i: (blk0 + i, 0)), const(wb), const(p_re), const(p_im),
                  h_spec, h_spec, const(wc), const(dvec), const(wglu), const(bglu)],
        out_specs=[pl.BlockSpec((tt, D_SSM), lambda i: (i, 0)), hout_spec, hout_spec],
        out_shape=[jax.ShapeDtypeStruct((n_rows, D_SSM), bf16), hout_shape, hout_shape],
        scratch_shapes=[pltpu.VMEM((1, D_STATE), f32), pltpu.VMEM((1, D_STATE), f32)],
        compiler_params=_params(("arbitrary",), VMEM_LIMIT),
        name="ssm",
    )(su, wb, p_re, p_im, hin_re, hin_im, wc, dvec, wglu, bglu)


def _outproj_body(fox_ref, gla_ref, gr_ref, ssm_ref, x_ref, fn_ref, gn_ref, avg_ref, w_ref, gf_ref,
                  x1_ref, xf_ref):
    fo = fox_ref[...]
    fo = fo * lax.rsqrt(jnp.mean(fo * fo, axis=-1, keepdims=True) + EPS) * fn_ref[...]
    go = gla_ref[...]
    ms = _dot3_rhs01(go * go, avg_ref[...]) * (1.0 / GLA_DV)
    go = go * lax.rsqrt(ms + EPS) * gn_ref[...] * jax.nn.silu(gr_ref[...])
    w = w_ref
    y = (_dot(fo.astype(bf16), w[0:D_FOX, :]) + _dot(go.astype(bf16), w[D_FOX:D_FOX + D_GLA, :])
         + _dot(ssm_ref[...], w[D_FOX + D_GLA:, :]))
    x1 = x_ref[...] + y
    x1_ref[...] = x1
    xf = x1 * lax.rsqrt(jnp.mean(x1 * x1, axis=-1, keepdims=True) + EPS) * gf_ref[...]
    xf_ref[...] = xf.astype(bf16)


def _outproj(fox_o, gla_o, gr, ssm_o, x, fox_norm, gla_norm6, avg, w_out, g_ffn):
    t, dm = x.shape
    tt = TOK_TILE
    row = lambda n: pl.BlockSpec((tt, n), lambda i: (i, 0))
    const = lambda a: pl.BlockSpec(a.shape, lambda i: (0,) * a.ndim)
    return pl.pallas_call(
        _outproj_body,
        grid=(t // tt,),
        in_specs=[row(D_FOX), row(D_GLA), row(D_GLA), row(D_SSM), row(dm),
                  const(fox_norm), const(gla_norm6), const(avg), const(w_out), const(g_ffn)],
        out_specs=[row(dm), row(dm)],
        out_shape=[jax.ShapeDtypeStruct((t, dm), f32), jax.ShapeDtypeStruct((t, dm), bf16)],
        compiler_params=_params(("arbitrary",), VMEM_LIMIT),
        name="outproj",
    )(fox_o, gla_o, gr, ssm_o, x, fox_norm, gla_norm6, avg, w_out, g_ffn)


def _topk_rows(s, k):
    n = s.shape[0]
    it = _iota(s.shape, 0)
    vals, idxs = [], []
    for _ in range(k):
        m = jnp.max(s, axis=0, keepdims=True)
        idx = jnp.min(jnp.where(s == m, it, n), axis=0, keepdims=True)
        vals.append(m)
        idxs.append(idx)
        s = jnp.where(it == idx, -jnp.inf, s)
    return jnp.concatenate(vals, axis=0), jnp.concatenate(idxs, axis=0)


def _take_rows(table, sel):
    return jnp.sum(jnp.where(_iota(table.shape, 0) == sel, table, 0), axis=0, keepdims=True)


def _peer_topk_body(xf_ref, wqt_ref, keys_ref, i1_ref, i2_ref, g_ref):
    kk = PEER_TOPK
    half = PEER_DK // 2
    qt = _dot_nt(wqt_ref[...], xf_ref[...])
    i1_all, i2_all, g_all = [], [], []
    for h in range(PEER_HEADS):
        q1 = qt[h * PEER_DK:h * PEER_DK + half, :].astype(bf16)
        q2 = qt[h * PEER_DK + half:(h + 1) * PEER_DK, :].astype(bf16)
        v1, i1 = _topk_rows(_dot(keys_ref[h, 0], q1), kk)
        v2, i2 = _topk_rows(_dot(keys_ref[h, 1], q2), kk)
        cand = jnp.concatenate([v1[a:a + 1, :] + v2 for a in range(kk)], axis=0)
        sc, pos = _topk_rows(cand, kk)
        sel1 = jnp.concatenate([_take_rows(i1, _div_pow2(pos[j:j + 1, :], kk)) for j in range(kk)], axis=0)
        sel2 = jnp.concatenate([_take_rows(i2, _mod_pow2(pos[j:j + 1, :], kk)) for j in range(kk)], axis=0)
        ex = jnp.exp(sc - sc[0:1, :])
        i1_all.append(sel1)
        i2_all.append(sel2)
        g_all.append(ex / jnp.sum(ex, axis=0, keepdims=True))
    i1_ref[...] = jnp.concatenate(i1_all, axis=0).T
    i2_ref[...] = jnp.concatenate(i2_all, axis=0).T
    g_ref[...] = jnp.concatenate(g_all, axis=0).T


def _peer_topk(xf, wqt, keys):
    t, dm = xf.shape
    tt = TOK_TILE
    const = lambda a: pl.BlockSpec(a.shape, lambda i: (0,) * a.ndim)
    slot = pl.BlockSpec((tt, PEER_SLOTS), lambda i: (i, 0))
    return pl.pallas_call(
        _peer_topk_body,
        grid=(t // tt,),
        in_specs=[pl.BlockSpec((tt, dm), lambda i: (i, 0)), const(wqt), const(keys)],
        out_specs=[slot, slot, slot],
        out_shape=[jax.ShapeDtypeStruct((t, PEER_SLOTS), i32), jax.ShapeDtypeStruct((t, PEER_SLOTS), i32),
                   jax.ShapeDtypeStruct((t, PEER_SLOTS), f32)],
        compiler_params=_params(("arbitrary",), VMEM_LIMIT),
        name="peer_topk",
    )(xf, wqt, keys)


def _peer_dense_body(xf_ref, i1_ref, i2_ref, g_ref, u_ref, v_ref, x1_ref, gfin_ref, o_ref,
                     gate_ref, acc_ref, *, final_norm):
    e = pl.program_id(1)
    tt = xf_ref.shape[0]
    nk = PEER_KEYS
    rows_per_tile = u_ref.shape[0] // nk

    @pl.when(e == 0)
    def _build_gates():
        acc_ref[...] = jnp.zeros_like(acc_ref)
        key_id = _iota((nk, PEER_SLOTS), 0)

        def one_token(t, carry):
            r1 = i1_ref[pl.ds(t, 1), :]
            r2 = i2_ref[pl.ds(t, 1), :]
            gr = g_ref[pl.ds(t, 1), :]
            p1 = jnp.where(r1 == key_id, gr, 0.0).astype(bf16)
            p2 = jnp.where(r2 == key_id, 1.0, 0.0).astype(bf16)
            gate_ref[pl.ds(pl.multiple_of(t * nk, nk), nk), :] = _dot_nt(p1, p2)
            return carry

        lax.fori_loop(0, tt, one_token, 0)

    a = _dot_nt(xf_ref[...], u_ref[...])
    act = jax.nn.gelu(a)
    ws = []
    for r in range(rows_per_tile):
        gt = gate_ref[pl.ds(e * rows_per_tile + r, tt, stride=nk), :]
        ws.append((gt * act[:, r * nk:(r + 1) * nk]).astype(bf16))
    acc_ref[...] += _dot(jnp.concatenate(ws, axis=1), v_ref[...])

    @pl.when(e == pl.num_programs(1) - 1)
    def _finish():
        y = x1_ref[...] + acc_ref[...]
        if final_norm:
            y = y * lax.rsqrt(jnp.mean(y * y, axis=-1, keepdims=True) + EPS) * gfin_ref[...]
        o_ref[...] = y


def _peer_dense(xf, i1, i2, g, u_tab, v_tab, x1, g_final, final_norm):
    t, dm = xf.shape
    tt = TOK_TILE
    et = PEER_ETILE
    tok = lambda n: pl.BlockSpec((tt, n), lambda i, e: (i, 0))
    tab = pl.BlockSpec((et, dm), lambda i, e: (e, 0))
    return pl.pallas_call(
        functools.partial(_peer_dense_body, final_norm=final_norm),
        grid=(t // tt, u_tab.shape[0] // et),
        in_specs=[tok(dm), tok(PEER_SLOTS), tok(PEER_SLOTS), tok(PEER_SLOTS), tab, tab, tok(dm),
                  pl.BlockSpec(g_final.shape, lambda i, e: (0, 0))],
        out_specs=tok(dm),
        out_shape=jax.ShapeDtypeStruct((t, dm), f32),
        scratch_shapes=[pltpu.VMEM((tt * PEER_KEYS, PEER_KEYS), f32), pltpu.VMEM((tt, dm), f32)],
        compiler_params=_params(("arbitrary", "arbitrary"), VMEM_LIMIT),
        name="peer_dense",
    )(xf, i1, i2, g, u_tab, v_tab, x1, g_final)


def _pack_w_in(w_in):
    dm = w_in.shape[0]
    sizes = (D_FOX, D_FOX, D_FOX, H_FOX, D_GK, D_GK, D_GLA, GLA_RANK, D_GLA, D_SSM)
    pts = np.cumsum(sizes)[:-1].tolist()
    fq, fk, fv, ff, gq, gk, gv, ga, gr, su = jnp.split(w_in, pts, axis=1)
    z = lambda n: jnp.zeros((dm, n), w_in.dtype)
    tail = jnp.concatenate([ff, z(TAIL_GA - H_FOX), ga, z(LANES - TAIL_GA - GLA_RANK)], axis=1)
    packed = jnp.concatenate([fq, fk, fv, gq, z(C_GK - C_GQ - D_GK), gk, z(C_GV - C_GK - D_GK),
                              gv, gr, su, tail], axis=1)
    assert packed.shape[1] == D_INP
    return packed.astype(bf16)


def _block_diag(blocks):
    g, a, b = blocks.shape
    eye = jnp.eye(g, dtype=blocks.dtype)
    return (blocks[:, :, None, :] * eye[:, None, :, None]).reshape(g * a, g * b)


def _head_match(rows, row_w, cols, col_w):
    return (np.arange(rows)[:, None] // row_w == np.arange(cols)[None, :] // col_w)


def kernel(x_prompt, x_sample, cache_fox_k, cache_fox_v, cache_fox_logf, state_gla, state_ssm_re, state_ssm_im, page_table, g_mix, w_in, fox_bf, fox_norm, gla_wa2, gla_ba, gla_norm, ssm_a_re, ssm_a_im, ssm_log_dt, ssm_b_re, ssm_b_im, ssm_c_re, ssm_c_im, ssm_d, ssm_w_glu, ssm_b_glu, w_out, g_ffn, peer_wq, peer_keys, peer_u, peer_v, g_final):
    bp, seq, dm = x_prompt.shape
    bd, tdec, _ = x_sample.shape
    depth = w_in.shape[0]
    assert bp == 1 and seq % FOX_TILE == 0 and seq % GLA_STEP == 0
    n_s = bd * tdec
    assert n_s == TOK_TILE and tdec == 8
    n_tok = seq + n_s
    n_pool, page = cache_fox_k.shape[1], cache_fox_k.shape[2]
    assert page == LANES and page_table.shape[1] % FOX_PAGES == 0

    x = jnp.concatenate([x_prompt.reshape(seq, dm), x_sample.reshape(n_s, dm)], axis=0)
    ck = cache_fox_k.reshape(depth, n_pool, page, D_FOX)
    cv = cache_fox_v.reshape(depth, n_pool, page, D_FOX)
    clft = jnp.pad(jnp.swapaxes(cache_fox_logf, 2, 3), ((0, 0), (0, 0), (0, 8 - H_FOX), (0, 0)))

    e_mat = jnp.asarray(_head_match(D_GK, GLA_DK, D_GLA, GLA_DV), bf16)
    mask_t = jnp.asarray(_head_match(D_GLA, GLA_DV, D_GK, GLA_DK), f32)
    avg = jnp.asarray(_head_match(D_GLA, GLA_DV, D_GLA, GLA_DV), bf16)
    zeros_state = jnp.zeros((1, D_STATE), f32)

    outs_p, outs_s = [], []
    for l in range(depth):
        w_p = _pack_w_in(w_in[l])
        bfp = jnp.pad(fox_bf[l], (0, LANES - H_FOX)).reshape(1, LANES)
        wa2p = jnp.pad(gla_wa2[l], ((TAIL_GA, LANES - TAIL_GA - GLA_RANK), (0, 0))).astype(bf16)
        ba = gla_ba[l].reshape(1, D_GK)
        a_re = ssm_a_re[l].reshape(1, D_STATE)
        a_im = ssm_a_im[l].reshape(1, D_STATE)
        ldt = jnp.repeat(ssm_log_dt[l], SSM_P).reshape(1, D_STATE)
        bt_re = jnp.transpose(ssm_b_re[l], (2, 0, 1)).reshape(SSM_GC, D_STATE)
        bt_im = jnp.transpose(ssm_b_im[l], (2, 0, 1)).reshape(SSM_GC, D_STATE)
        p_re, p_im, bb_re, bb_im = _ssm_prep(a_re, a_im, ldt, bt_re, bt_im, TOK_TILE)
        to_blocks = lambda bb: jnp.transpose(bb.reshape(SSM_GC, SSM_G, SSM_P), (1, 0, 2))
        wb = jnp.concatenate([_block_diag(to_blocks(bb_re)), _block_diag(to_blocks(bb_im))], axis=1).astype(bf16)
        c_blocks = lambda cc: jnp.transpose(cc, (0, 2, 1))
        wc = jnp.concatenate([_block_diag(c_blocks(ssm_c_re[l])), -_block_diag(c_blocks(ssm_c_im[l]))],
                             axis=0).astype(bf16)
        dvec = ssm_d[l].reshape(1, D_SSM)
        wglu = ssm_w_glu[l].astype(bf16)
        bglu = ssm_b_glu[l].reshape(1, D_SSM)
        wqt = jnp.transpose(peer_wq[l].reshape(dm, PEER_HEADS * PEER_DK)).astype(bf16)
        keys = peer_keys[l].astype(bf16)
        u_tab = peer_u[l].astype(bf16)
        v_tab = peer_v[l].astype(bf16)

        (fq, fkb, fvb, fk, fv, lf, csum, gq, gk, gv, la, gr, su) = _inproj(
            x, g_mix[l].reshape(1, dm), w_p, bfp, wa2p, ba)

        c6 = csum[:seq, :H_FOX]
        ccol = jnp.transpose(c6.reshape(seq, H_FOX // 2, 2), (1, 0, 2))
        crow = jnp.transpose(c6).reshape(H_FOX // 2, 2, seq)
        fox_p = _fox_prompt(fq, fkb, fvb, ccol, crow, seq)
        lfn_t = jnp.pad(jnp.swapaxes(lf[seq:, :8].reshape(bd, tdec, 8), 1, 2), ((0, 0), (0, 0), (0, LANES - tdec)))
        fox_s = _fox_sample(fq[seq:].astype(f32), fk[seq:], fv[seq:], lfn_t, ck, cv, clft, page_table, l, tdec)
        fox_o = jnp.concatenate([fox_p, fox_s], axis=0)

        gla_p, st_p = _gla(gq, gk, la, gv, jnp.zeros((1, D_GLA, D_GK), f32), e_mat, mask_t,
                           row0=0, nseq=1, steps=seq // GLA_STEP, rows_per_step=GLA_STEP, chunk=GLA_CHUNK)
        s0 = state_gla[l].astype(f32)
        s0t = jnp.transpose(s0, (0, 1, 3, 2))
        s0t = (s0t[:, :, :, None, :] * jnp.eye(H_GLA, dtype=f32)[None, :, None, :, None]).reshape(bd, D_GLA, D_GK)
        gla_s, st_s = _gla(gq, gk, la, gv, s0t, e_mat, mask_t,
                           row0=seq, nseq=bd, steps=1, rows_per_step=tdec, chunk=tdec)
        gla_o = jnp.concatenate([gla_p, gla_s], axis=0)

        def unpack_state(st):
            st = st.reshape(-1, H_GLA, GLA_DV, H_GLA, GLA_DK)
            st = jnp.stack([st[:, h, :, h, :] for h in range(H_GLA)], axis=1)
            return jnp.transpose(st, (0, 1, 3, 2))

        ssm_p, hre_p, him_p = _ssm(su, wb, p_re, p_im, zeros_state, zeros_state, wc, dvec, wglu, bglu,
                                   row0=0, n_rows=seq, seg=TOK_TILE, tile_carry=True)
        h0r = jnp.repeat(state_ssm_re[l].reshape(bd, D_STATE).astype(f32), tdec, axis=0)
        h0i = jnp.repeat(state_ssm_im[l].reshape(bd, D_STATE).astype(f32), tdec, axis=0)
        ps_re = jnp.tile(p_re[:tdec], (bd, 1))
        ps_im = jnp.tile(p_im[:tdec], (bd, 1))
        ssm_s, hre_s, him_s = _ssm(su, wb, ps_re, ps_im, h0r, h0i, wc, dvec, wglu, bglu,
                                   row0=seq, n_rows=n_s, seg=tdec, tile_carry=False)
        ssm_o = jnp.concatenate([ssm_p, ssm_s], axis=0)

        x1, xf = _outproj(fox_o, gla_o, gr, ssm_o, x, fox_norm[l].reshape(1, D_FOX),
                          jnp.tile(gla_norm[l], H_GLA).reshape(1, D_GLA), avg,
                          w_out[l].astype(bf16), g_ffn[l].reshape(1, dm))
        i1, i2, gate = _peer_topk(xf, wqt, keys)
        x = _peer_dense(xf, i1, i2, gate, u_tab, v_tab, x1, g_final.reshape(1, dm), l == depth - 1)

        outs_p.append((fk[:seq].reshape(bp, seq, H_FOX, FOX_DH), fv[:seq].reshape(bp, seq, H_FOX, FOX_DH),
                       lf[:seq, :H_FOX].reshape(bp, seq, H_FOX), unpack_state(st_p),
                       hre_p.reshape(bp, SSM_G, SSM_P), him_p.reshape(bp, SSM_G, SSM_P)))
        outs_s.append((fk[seq:].reshape(bd, tdec, H_FOX, FOX_DH), fv[seq:].reshape(bd, tdec, H_FOX, FOX_DH),
                       lf[seq:, :H_FOX].reshape(bd, tdec, H_FOX), unpack_state(st_s),
                       hre_s[tdec - 1::tdec].reshape(bd, SSM_G, SSM_P), him_s[tdec - 1::tdec].reshape(bd, SSM_G, SSM_P)))

    p_stk = [jnp.stack(t) for t in zip(*outs_p)]
    s_stk = [jnp.stack(t) for t in zip(*outs_s)]
    y_prompt = x[:seq].reshape(bp, seq, dm)
    y_sample = x[seq:].reshape(bd, tdec, dm)
    return (y_prompt, y_sample, *p_stk, *s_stk)
```

```python
import functools

import numpy as np
import jax
import jax.numpy as jnp
from jax import lax
from jax.experimental import pallas as pl
from jax.experimental.pallas import tpu as pltpu

f32 = jnp.float32
bf16 = jnp.bfloat16
i32 = jnp.int32

EPS = 1e-6
LOG2E = 1.4426950408889634
LANES = 128
VMEM_LIMIT = 56 * 1024 * 1024

H_FOX, FOX_DH = 6, 64
D_FOX = H_FOX * FOX_DH
H_GLA, GLA_DK, GLA_DV = 6, 32, 64
D_GK = H_GLA * GLA_DK
D_GLA = H_GLA * GLA_DV
GLA_RANK = 16
GLA_TAU = 16.0
SSM_G, SSM_GC, SSM_P = 16, 16, 64
D_SSM = SSM_G * SSM_GC
D_STATE = SSM_G * SSM_P
PEER_KEYS, PEER_HEADS, PEER_TOPK, PEER_DK = 128, 8, 16, 256
PEER_SLOTS = PEER_HEADS * PEER_TOPK

C_FQ, C_FK, C_FV = 0, 384, 768
C_GQ, C_GK = 1152, 1408
C_GV, C_GR, C_SU, C_TAIL = 1664, 2048, 2432, 2688
D_INP = 2816
TAIL_FF, TAIL_GA = 0, 8

TOK_TILE = 256
FOX_TILE = 1024
FOX_PAGES = 16
GLA_STEP = 512
GLA_CHUNK = 16
PEER_ETILE = 1024
GATE_GROUP = 8


def _dot(a, b):
    return jnp.dot(a, b, preferred_element_type=f32)


def _dot_nt(a, b):
    return lax.dot_general(a, b, (((1,), (1,)), ((), ())), preferred_element_type=f32)


def _dot_tn(a, b):
    return lax.dot_general(a, b, (((0,), (0,)), ((), ())), preferred_element_type=f32)


def _split3(x):
    h = x.astype(bf16)
    r = x - h.astype(f32)
    m = r.astype(bf16)
    l = (r - m.astype(f32)).astype(bf16)
    return h, m, l


def _dot3_lhs01(sel, x):
    h, m, l = _split3(x)
    return _dot(sel, h) + _dot(sel, m) + _dot(sel, l)


def _dot3_rhs01(x, sel):
    h, m, l = _split3(x)
    return _dot(h, sel) + _dot(m, sel) + _dot(l, sel)


def _log_sigmoid(x):
    return -(jnp.maximum(-x, 0.0) + jnp.log1p(jnp.exp(-jnp.abs(x))))


def _iota(shape, dim):
    return lax.broadcasted_iota(i32, shape, dim)


def _div_pow2(x, n):
    assert n & (n - 1) == 0
    return jnp.right_shift(x, n.bit_length() - 1)


def _mod_pow2(x, n):
    assert n & (n - 1) == 0
    return jnp.bitwise_and(x, n - 1)


def _params(sem, vmem=None):
    return pltpu.CompilerParams(dimension_semantics=sem, vmem_limit_bytes=vmem)


def _fox_bias_lanes():
    place_q = np.zeros((3, LANES, H_FOX * LANES), np.float32)
    place_k = np.zeros((3, LANES, H_FOX * LANES), np.float32)
    ones_q = np.zeros((1, H_FOX * LANES), np.float32)
    ones_k = np.zeros((1, H_FOX * LANES), np.float32)
    for j in range(H_FOX):
        base = j * LANES + (FOX_DH if j % 2 == 0 else 0)
        for p in range(3):
            place_q[p, j, base + p] = 1.0
            ones_k[0, base + p] = 1.0
            ones_q[0, base + 3 + p] = 1.0
            place_k[p, j, base + 3 + p] = 1.0
    return place_q, place_k, ones_q, ones_k


def _inproj_body(x_ref, g_ref, w_ref, bfp_ref, wa2_ref, ba_ref, pq_ref, pk_ref, oq_ref, ok_ref,
                 fq_ref, qa_ref, ka_ref, fvb_ref, fk_ref, fv_ref, lf_ref,
                 gq_ref, gk_ref, gv_ref, la_ref, gr_ref, su_ref, carry_ref):
    @pl.when(pl.program_id(0) == 0)
    def _init():
        carry_ref[...] = jnp.zeros_like(carry_ref)

    x = x_ref[...]
    tt = x.shape[0]
    xn = x * lax.rsqrt(jnp.mean(x * x, axis=-1, keepdims=True) + EPS) * g_ref[...]
    z = _dot(xn.astype(bf16), w_ref[...])
    fq_ref[...] = (z[:, C_FQ:C_FQ + D_FOX] * (FOX_DH ** -0.5)).astype(bf16)
    fk = z[:, C_FK:C_FK + D_FOX]
    fv = z[:, C_FV:C_FV + D_FOX]
    fk_ref[...] = fk
    fv_ref[...] = fv
    fvb_ref[...] = fv.astype(bf16)
    gq_ref[...] = z[:, C_GQ:C_GQ + D_GK] * (GLA_DK ** -0.5)
    gk_ref[...] = z[:, C_GK:C_GK + D_GK]
    gv_ref[...] = z[:, C_GV:C_GV + D_GLA]
    gr_ref[...] = z[:, C_GR:C_GR + D_GLA]
    su_ref[...] = z[:, C_SU:C_SU + D_SSM]
    tail = z[:, C_TAIL:C_TAIL + LANES]
    lane = _iota((tt, LANES), 1)
    lf = jnp.where(lane < H_FOX, _log_sigmoid(tail + bfp_ref[...]), 0.0)
    lf_ref[...] = lf
    tri = jnp.where(_iota((tt, tt), 0) >= _iota((tt, tt), 1), 1.0, 0.0).astype(bf16)
    cs = _dot3_lhs01(tri, lf) + carry_ref[...]
    carry_ref[...] = cs[tt - 1:tt, :]
    pieces = _split3(cs * LOG2E)
    q_bias = sum(_dot(pc, pq_ref[i]) for i, pc in enumerate(pieces)) + oq_ref[...]
    k_bias = ok_ref[...] - sum(_dot(pc, pk_ref[i]) for i, pc in enumerate(pieces))
    low = lane < FOX_DH
    for hp in range(H_FOX // 2):
        qt = z[:, C_FQ + hp * LANES:C_FQ + (hp + 1) * LANES] * (LOG2E * FOX_DH ** -0.5)
        kt = z[:, C_FK + hp * LANES:C_FK + (hp + 1) * LANES]
        for hh in range(2):
            blk = slice((2 * hp + hh) * LANES, (2 * hp + hh + 1) * LANES)
            real = low if hh == 0 else jnp.logical_not(low)
            qa_ref[:, blk] = jnp.where(real, qt, q_bias[:, blk]).astype(bf16)
            ka_ref[:, blk] = jnp.where(real, kt, k_bias[:, blk]).astype(bf16)
    za = _dot(tail.astype(bf16), wa2_ref[...]) + ba_ref[...]
    la_ref[...] = _log_sigmoid(za) * (1.0 / GLA_TAU)


def _inproj(x, g_mix, w_p, bfp, wa2p, ba):
    t = x.shape[0]
    tt = TOK_TILE
    row = lambda n: pl.BlockSpec((tt, n), lambda i: (i, 0))
    const = lambda a: pl.BlockSpec(a.shape, lambda i: (0,) * a.ndim)
    outs = [
        (D_FOX, bf16), (H_FOX * LANES, bf16), (H_FOX * LANES, bf16), (D_FOX, bf16), (D_FOX, f32), (D_FOX, f32),
        (LANES, f32),
        (D_GK, f32), (D_GK, f32), (D_GLA, f32), (D_GK, f32), (D_GLA, f32), (D_SSM, f32),
    ]
    place_q, place_k, ones_q, ones_k = _fox_bias_lanes()
    consts = [g_mix, w_p, bfp, wa2p, ba, jnp.asarray(place_q, bf16), jnp.asarray(place_k, bf16),
              jnp.asarray(ones_q), jnp.asarray(ones_k)]
    return pl.pallas_call(
        _inproj_body,
        grid=(t // tt,),
        in_specs=[row(x.shape[1])] + [const(a) for a in consts],
        out_specs=[row(n) for n, _ in outs],
        out_shape=[jax.ShapeDtypeStruct((t, n), d) for n, d in outs],
        scratch_shapes=[pltpu.VMEM((1, LANES), f32)],
        compiler_params=_params(("arbitrary",), VMEM_LIMIT),
        name="inproj",
    )(x, *consts)


def _fox_prompt_body(qi_ref, ki_ref, qa_ref, qb_ref, ka_ref, kb_ref, v_ref, o_ref,
                     m_ref, l_ref, acc_ref):
    p = pl.program_id(1)
    qi = qi_ref[p]
    ki = ki_ref[p]
    tq, tk = qa_ref.shape[0], ka_ref.shape[0]

    @pl.when(ki == 0)
    def _init():
        m_ref[...] = jnp.full_like(m_ref, -jnp.inf)
        l_ref[...] = jnp.zeros_like(l_ref)
        acc_ref[...] = jnp.zeros_like(acc_ref)

    def step(diagonal):
        vext = jnp.concatenate([v_ref[...], jnp.ones((tk, LANES), bf16)], axis=1)
        for hh, (q_ref, k_ref) in enumerate(((qa_ref, ka_ref), (qb_ref, kb_ref))):
            s = _dot_nt(q_ref[...], k_ref[...])
            if diagonal:
                s = jnp.where(_iota((tq, tk), 0) >= _iota((tq, tk), 1), s, -jnp.inf)
            m_old = m_ref[hh]
            m_new = jnp.maximum(m_old, jnp.max(s, axis=-1, keepdims=True))
            pr = jnp.exp2(s - jnp.tile(m_new, (1, tk // LANES)))
            alpha = jnp.exp2(m_old - m_new)
            pv = _dot(pr.astype(bf16), vext)
            l_ref[hh] = alpha * l_ref[hh] + pv[:, LANES:]
            m_ref[hh] = m_new
            acc_ref[hh] = alpha * acc_ref[hh] + pv[:, :LANES]

    @pl.when(ki < qi)
    def _off():
        step(False)

    @pl.when(ki == qi)
    def _diag():
        step(True)
        lane = _iota((1, LANES), 1)
        o_ref[...] = jnp.where(lane < FOX_DH, acc_ref[0] / l_ref[0], acc_ref[1] / l_ref[1])


def _fox_prompt(qaug, kaug, fvb, seq):
    tq = FOX_TILE
    nq = seq // tq
    qi_tbl = np.concatenate([np.full(i + 1, i) for i in range(nq)]).astype(np.int32)
    ki_tbl = np.concatenate([np.arange(i + 1) for i in range(nq)]).astype(np.int32)
    grid_spec = pltpu.PrefetchScalarGridSpec(
        num_scalar_prefetch=2,
        grid=(H_FOX // 2, len(qi_tbl)),
        in_specs=[
            pl.BlockSpec((tq, LANES), lambda h, p, qi, ki: (qi[p], 2 * h)),
            pl.BlockSpec((tq, LANES), lambda h, p, qi, ki: (qi[p], 2 * h + 1)),
            pl.BlockSpec((tq, LANES), lambda h, p, qi, ki: (ki[p], 2 * h)),
            pl.BlockSpec((tq, LANES), lambda h, p, qi, ki: (ki[p], 2 * h + 1)),
            pl.BlockSpec((tq, LANES), lambda h, p, qi, ki: (ki[p], h)),
        ],
        out_specs=pl.BlockSpec((tq, LANES), lambda h, p, qi, ki: (qi[p], h)),
        scratch_shapes=[pltpu.VMEM((2, tq, LANES), f32), pltpu.VMEM((2, tq, LANES), f32),
                        pltpu.VMEM((2, tq, LANES), f32)],
    )
    return pl.pallas_call(
        _fox_prompt_body,
        grid_spec=grid_spec,
        out_shape=jax.ShapeDtypeStruct((seq, D_FOX), f32),
        compiler_params=_params(("arbitrary", "arbitrary"), VMEM_LIMIT),
        name="fox_prompt",
    )(jnp.asarray(qi_tbl), jnp.asarray(ki_tbl), qaug, qaug, kaug, kaug, fvb)


def _expand_heads(x, rows):
    return jnp.concatenate(
        [jnp.broadcast_to(x[h:h + 1, :], (rows, x.shape[1])) for h in range(H_FOX)], axis=0)


def _fox_sample_body(pt_ref, q_ref, kn_ref, vn_ref, lfn_ref, *rest):
    del pt_ref
    npg = FOX_PAGES
    kp_refs = rest[0:npg]
    vp_refs = rest[npg:2 * npg]
    lfp_refs = rest[2 * npg:3 * npg]
    o_ref = rest[3 * npg]
    m_ref, l_ref, acc_ref, car_ref, qbd_ref, cq_ref = rest[3 * npg + 1:]
    j = pl.program_id(1)
    t = q_ref.shape[0]
    hq = H_FOX * t
    page = kp_refs[0].shape[-1]

    def update(s, weigh):
        m_old = m_ref[...]
        m_new = jnp.maximum(m_old, jnp.max(s, axis=-1, keepdims=True))
        pr = jnp.exp(s - m_new)
        alpha = jnp.exp(m_old - m_new)
        l_ref[...] = alpha * l_ref[...] + jnp.sum(pr, axis=-1, keepdims=True)
        m_ref[...] = m_new
        acc_ref[...] = alpha * acc_ref[...] + weigh(pr.astype(bf16))

    @pl.when(j == 0)
    def _new_tokens():
        m_ref[...] = jnp.full_like(m_ref, -jnp.inf)
        l_ref[...] = jnp.zeros_like(l_ref)
        acc_ref[...] = jnp.zeros_like(acc_ref)
        car_ref[...] = jnp.zeros_like(car_ref)
        q = q_ref[...]
        q6 = jnp.concatenate([q] * H_FOX, axis=0)
        keep = _div_pow2(_iota((hq, D_FOX), 0), t) == _div_pow2(_iota((hq, D_FOX), 1), FOX_DH)
        qbd_ref[...] = jnp.where(keep, q6, 0.0).astype(bf16)
        cum = lfn_ref[...]
        lane8 = _iota(cum.shape, 1)
        sh = 1
        while sh < t:
            cum = cum + jnp.where(lane8 >= sh, pltpu.roll(cum, sh, 1), 0.0)
            sh *= 2
        cum_rows = _expand_heads(cum, t)
        qpos = _mod_pow2(_iota((hq, LANES), 0), t)
        kpos = _iota((hq, LANES), 1)
        cq = jnp.sum(jnp.where(kpos == qpos, cum_rows, 0.0), axis=-1, keepdims=True)
        cq_ref[...] = cq
        zpad = jnp.zeros((LANES - t, D_FOX), f32)
        kn = jnp.concatenate([kn_ref[...], zpad], axis=0).astype(bf16)
        vn = jnp.concatenate([vn_ref[...], zpad], axis=0).astype(bf16)
        s = _dot_nt(qbd_ref[...], kn) + cq - cum_rows
        s = jnp.where(kpos <= qpos, s, -jnp.inf)
        update(s, lambda pr: _dot(pr, vn))

    upper = jnp.where(_iota((page, page), 0) > _iota((page, page), 1), 1.0, 0.0).astype(bf16)
    qbd = qbd_ref[...]
    car = car_ref[...]
    scores = []
    for g in range(npg):
        lfp = lfp_refs[g][...]
        suffix = _dot3_rhs01(lfp, upper) + car
        car = car + jnp.sum(lfp, axis=-1, keepdims=True)
        kpt = kp_refs[g][...].reshape(D_FOX, page).astype(bf16)
        scores.append(_dot(qbd, kpt) + _expand_heads(suffix, t))
    car_ref[...] = car

    def weigh_pages(pr):
        out = None
        for g in range(npg):
            vpt = vp_refs[g][...].reshape(D_FOX, page).astype(bf16)
            part = _dot_nt(pr[:, g * page:(g + 1) * page], vpt)
            out = part if out is None else out + part
        return out

    update(jnp.concatenate(scores, axis=1) + cq_ref[...], weigh_pages)

    @pl.when(j == pl.num_programs(1) - 1)
    def _finish():
        o48 = acc_ref[...] / l_ref[...]
        head = _div_pow2(_iota((t, D_FOX), 1), FOX_DH)
        out = jnp.zeros((t, D_FOX), f32)
        for h in range(H_FOX):
            out = out + jnp.where(head == h, o48[h * t:(h + 1) * t, :], 0.0)
        o_ref[...] = out


def _fox_sample(q_s, k_s, v_s, lfn_t, cache_k, cache_v, cache_lft, page_table, layer, t):
    nb, n_pages = page_table.shape
    page = cache_k.shape[-1]
    npg = FOX_PAGES
    hq = H_FOX * t
    seq_spec = pl.BlockSpec((t, D_FOX), lambda b, j, pt: (b, 0))

    def page_spec(shape, g):
        zeros = (0,) * len(shape)
        return pl.BlockSpec((None, None) + shape,
                            lambda b, j, pt: (layer, pt[b, n_pages - 1 - (j * npg + g)]) + zeros)

    grid_spec = pltpu.PrefetchScalarGridSpec(
        num_scalar_prefetch=1,
        grid=(nb, n_pages // npg),
        in_specs=[seq_spec, seq_spec, seq_spec,
                  pl.BlockSpec((None, 8, LANES), lambda b, j, pt: (b, 0, 0))]
        + [page_spec((H_FOX, FOX_DH, page), g) for g in range(npg)]
        + [page_spec((H_FOX, FOX_DH, page), g) for g in range(npg)]
        + [page_spec((8, page), g) for g in range(npg)],
        out_specs=seq_spec,
        scratch_shapes=[pltpu.VMEM((hq, 1), f32), pltpu.VMEM((hq, 1), f32),
                        pltpu.VMEM((hq, D_FOX), f32), pltpu.VMEM((8, 1), f32),
                        pltpu.VMEM((hq, D_FOX), bf16), pltpu.VMEM((hq, 1), f32)],
    )
    return pl.pallas_call(
        _fox_sample_body,
        grid_spec=grid_spec,
        out_shape=jax.ShapeDtypeStruct((nb * t, D_FOX), f32),
        compiler_params=_params(("arbitrary", "arbitrary"), VMEM_LIMIT),
        name="fox_sample",
    )(page_table, q_s, k_s, v_s, lfn_t, *([cache_k] * npg), *([cache_v] * npg), *([cache_lft] * npg))


def _gla_body(q_ref, k_ref, la_ref, v_ref, s0_ref, e_ref, mask_ref, o_ref, sout_ref, st_ref, *, chunk):
    i = pl.program_id(1)

    @pl.when(i == 0)
    def _init():
        st_ref[...] = s0_ref[...]

    c = chunk
    n_chunks = q_ref.shape[0] // c
    row = _iota((c, D_GK), 0)

    def one_chunk(ci, carry):
        r0 = pl.multiple_of(ci * c, c)
        q = q_ref[pl.ds(r0, c), :]
        k = k_ref[pl.ds(r0, c), :]
        v = v_ref[pl.ds(r0, c), :]
        b = la_ref[pl.ds(r0, c), :]
        sh = 1
        while sh < c:
            b = b + jnp.where(row >= sh, pltpu.roll(b, sh, 0), 0.0)
            sh *= 2
        st = st_ref[...]
        o = _dot_nt((q * jnp.exp(b)).astype(bf16), st.astype(bf16))
        pieces = []
        for s in range(c):
            d = jnp.where(row >= s, b - b[s:s + 1, :], -jnp.inf)
            pieces.append(q * k[s:s + 1, :] * jnp.exp(d))
        pst = jnp.concatenate(pieces, axis=0)
        hi = pst.astype(bf16)
        lo = (pst - hi.astype(f32)).astype(bf16)
        att = _dot(hi, e_ref[...]) + _dot(lo, e_ref[...])
        for s in range(c):
            o = o + att[s * c:(s + 1) * c, :] * v[s:s + 1, :]
        o_ref[pl.ds(r0, c), :] = o
        b_last = b[c - 1:c, :]
        kd = (k * jnp.exp(b_last - b)).astype(bf16)
        st_ref[...] = st * jnp.exp(b_last) + _dot_tn(v.astype(bf16), kd) * mask_ref[...]
        return carry

    lax.fori_loop(0, n_chunks, one_chunk, 0)

    @pl.when(i == pl.num_programs(1) - 1)
    def _finish():
        sout_ref[...] = st_ref[...]


def _gla(gq, gk, la, gv, s0t, e_mat, mask_t, *, row0, nseq, steps, rows_per_step, chunk):
    blk0 = row0 // rows_per_step
    rows = lambda n: pl.BlockSpec((rows_per_step, n), lambda s, i: (blk0 + s * steps + i, 0))
    orow = pl.BlockSpec((rows_per_step, D_GLA), lambda s, i: (s * steps + i, 0))
    st_spec = pl.BlockSpec((None, D_GLA, D_GK), lambda s, i: (s, 0, 0))
    const = lambda a: pl.BlockSpec(a.shape, lambda s, i: (0,) * a.ndim)
    return pl.pallas_call(
        functools.partial(_gla_body, chunk=chunk),
        grid=(nseq, steps),
        in_specs=[rows(D_GK), rows(D_GK), rows(D_GK), rows(D_GLA), st_spec, const(e_mat), const(mask_t)],
        out_specs=[orow, st_spec],
        out_shape=[jax.ShapeDtypeStruct((nseq * steps * rows_per_step, D_GLA), f32),
                   jax.ShapeDtypeStruct((nseq, D_GLA, D_GK), f32)],
        scratch_shapes=[pltpu.VMEM((D_GLA, D_GK), f32)],
        compiler_params=_params(("arbitrary", "arbitrary"), VMEM_LIMIT),
        name="gla",
    )(gq, gk, la, gv, s0t, e_mat, mask_t)


def _ssm_prep_body(are_ref, aim_ref, ldt_ref, bre_ref, bim_ref,
                   pre_ref, pim_ref, bbre_ref, bbim_ref):
    a_re = are_ref[...]
    a_im = aim_ref[...]
    dt = jnp.exp(ldt_ref[...])
    mag = jnp.exp(a_re * dt)
    abar_re = mag * jnp.cos(a_im * dt)
    abar_im = mag * jnp.sin(a_im * dt)
    den = a_re * a_re + a_im * a_im
    nr, ni = abar_re - 1.0, abar_im
    fr = (nr * a_re + ni * a_im) / den
    fi = (ni * a_re - nr * a_im) / den
    bbre_ref[...] = fr * bre_ref[...] - fi * bim_ref[...]
    bbim_ref[...] = fr * bim_ref[...] + fi * bre_ref[...]
    n = pre_ref.shape[0]
    pre_ref[0:1, :] = abar_re
    pim_ref[0:1, :] = abar_im
    m = 1
    while m < n:
        tr = pre_ref[m - 1:m, :]
        ti = pim_ref[m - 1:m, :]
        xr = pre_ref[0:m, :]
        xi = pim_ref[0:m, :]
        pre_ref[m:2 * m, :] = xr * tr - xi * ti
        pim_ref[m:2 * m, :] = xr * ti + xi * tr
        m *= 2


def _ssm_prep(a_re, a_im, log_dt, bt_re, bt_im, n_pow):
    outs = [(n_pow, D_STATE), (n_pow, D_STATE), (SSM_GC, D_STATE), (SSM_GC, D_STATE)]
    return pl.pallas_call(
        _ssm_prep_body,
        out_shape=[jax.ShapeDtypeStruct(s, f32) for s in outs],
        name="ssm_prep",
    )(a_re, a_im, log_dt, bt_re, bt_im)


def _ssm_body(u_ref, wb_ref, pre_ref, pim_ref, hin_re_ref, hin_im_ref, wc_ref, d_ref, wg_ref, bg_ref,
              o_ref, hre_ref, him_ref, car_re_ref, car_im_ref, *, seg, tile_carry):
    i = pl.program_id(0)
    u = u_ref[...]
    tt = u.shape[0]
    d = _dot(u.astype(bf16), wb_ref[...])
    xr = d[:, :D_STATE]
    xi = d[:, D_STATE:]
    pos = _mod_pow2(_iota((tt, D_STATE), 0), seg)
    sh = 1
    while sh < seg:
        ar = pre_ref[sh - 1:sh, :]
        ai = pim_ref[sh - 1:sh, :]
        sr = jnp.where(pos >= sh, pltpu.roll(xr, sh, 0), 0.0)
        si = jnp.where(pos >= sh, pltpu.roll(xi, sh, 0), 0.0)
        xr, xi = xr + ar * sr - ai * si, xi + ar * si + ai * sr
        sh *= 2
    if tile_carry:
        @pl.when(i == 0)
        def _init():
            car_re_ref[...] = hin_re_ref[...]
            car_im_ref[...] = hin_im_ref[...]
        h0r = car_re_ref[...]
        h0i = car_im_ref[...]
    else:
        h0r = hin_re_ref[...]
        h0i = hin_im_ref[...]
    pr = pre_ref[...]
    pi = pim_ref[...]
    hr = xr + pr * h0r - pi * h0i
    hi = xi + pr * h0i + pi * h0r
    if tile_carry:
        car_re_ref[...] = hr[tt - 1:tt, :]
        car_im_ref[...] = hi[tt - 1:tt, :]
        hre_ref[...] = hr[tt - 1:tt, :]
        him_ref[...] = hi[tt - 1:tt, :]
    else:
        hre_ref[...] = hr
        him_ref[...] = hi
    hcat = jnp.concatenate([hr, hi], axis=1).astype(bf16)
    y = _dot(hcat, wc_ref[...]) + d_ref[...] * u
    zg = jax.nn.gelu(y)
    o_ref[...] = (zg * jax.nn.sigmoid(_dot(zg.astype(bf16), wg_ref[...]) + bg_ref[...])).astype(bf16)


def _ssm(su, wb, p_re, p_im, hin_re, hin_im, wc, dvec, wglu, bglu, *, row0, n_rows, seg, tile_carry):
    tt = p_re.shape[0]
    blk0 = row0 // tt
    const = lambda a: pl.BlockSpec(a.shape, lambda i: (0,) * a.ndim)
    if tile_carry:
        h_spec = const(hin_re)
        hout_spec = pl.BlockSpec((1, D_STATE), lambda i: (0, 0))
        hout_shape = jax.ShapeDtypeStruct((1, D_STATE), f32)
    else:
        h_spec = pl.BlockSpec((tt, D_STATE), lambda i: (i, 0))
        hout_spec = h_spec
        hout_shape = jax.ShapeDtypeStruct((n_rows, D_STATE), f32)
    return pl.pallas_call(
        functools.partial(_ssm_body, seg=seg, tile_carry=tile_carry),
        grid=(n_rows // tt,),
        in_specs=[pl.BlockSpec((tt, D_SSM), lambda i: (blk0 + i, 0)), const(wb), const(p_re), const(p_im),
                  h_spec, h_spec, const(wc), const(dvec), const(wglu), const(bglu)],
        out_specs=[pl.BlockSpec((tt, D_SSM), lambda i: (i, 0)), hout_spec, hout_spec],
        out_shape=[jax.ShapeDtypeStruct((n_rows, D_SSM), bf16), hout_shape, hout_shape],
        scratch_shapes=[pltpu.VMEM((1, D_STATE), f32), pltpu.VMEM((1, D_STATE), f32)],
        compiler_params=_params(("arbitrary",), VMEM_LIMIT),
        name="ssm",
    )(su, wb, p_re, p_im, hin_re, hin_im, wc, dvec, wglu, bglu)


def _outproj_body(fox_ref, gla_ref, gr_ref, ssm_ref, x_ref, fn_ref, gn_ref, avg_ref, w_ref, gf_ref,
                  x1_ref, xf_ref):
    fo = fox_ref[...]
    fo = fo * lax.rsqrt(jnp.mean(fo * fo, axis=-1, keepdims=True) + EPS) * fn_ref[...]
    go = gla_ref[...]
    ms = _dot3_rhs01(go * go, avg_ref[...]) * (1.0 / GLA_DV)
    go = go * lax.rsqrt(ms + EPS) * gn_ref[...] * jax.nn.silu(gr_ref[...])
    w = w_ref
    y = (_dot(fo.astype(bf16), w[0:D_FOX, :]) + _dot(go.astype(bf16), w[D_FOX:D_FOX + D_GLA, :])
         + _dot(ssm_ref[...], w[D_FOX + D_GLA:, :]))
    x1 = x_ref[...] + y
    x1_ref[...] = x1
    xf = x1 * lax.rsqrt(jnp.mean(x1 * x1, axis=-1, keepdims=True) + EPS) * gf_ref[...]
    xf_ref[...] = xf.astype(bf16)


def _outproj(fox_o, gla_o, gr, ssm_o, x, fox_norm, gla_norm6, avg, w_out, g_ffn):
    t, dm = x.shape
    tt = TOK_TILE
    row = lambda n: pl.BlockSpec((tt, n), lambda i: (i, 0))
    const = lambda a: pl.BlockSpec(a.shape, lambda i: (0,) * a.ndim)
    return pl.pallas_call(
        _outproj_body,
        grid=(t // tt,),
        in_specs=[row(D_FOX), row(D_GLA), row(D_GLA), row(D_SSM), row(dm),
                  const(fox_norm), const(gla_norm6), const(avg), const(w_out), const(g_ffn)],
        out_specs=[row(dm), row(dm)],
        out_shape=[jax.ShapeDtypeStruct((t, dm), f32), jax.ShapeDtypeStruct((t, dm), bf16)],
        compiler_params=_params(("arbitrary",), VMEM_LIMIT),
        name="outproj",
    )(fox_o, gla_o, gr, ssm_o, x, fox_norm, gla_norm6, avg, w_out, g_ffn)


def _topk_rows(s, k, ids=None):
    it = _iota(s.shape, 0) if ids is None else ids
    big = jnp.iinfo(jnp.int32).max
    vals, idxs = [], []
    for _ in range(k):
        m = jnp.max(s, axis=0, keepdims=True)
        idx = jnp.min(jnp.where(s == m, it, big), axis=0, keepdims=True)
        vals.append(m)
        idxs.append(idx)
        s = jnp.where(it == idx, -jnp.inf, s)
    return jnp.concatenate(vals, axis=0), jnp.concatenate(idxs, axis=0)


def _pair_candidates(v1, v2, tt):
    kk = v1.shape[0]
    blocks, ids = [], []
    for a in range(kk):
        nb = kk // (a + 1)
        rows = -(-nb // 8) * 8
        blk = v1[a:a + 1, :] + v2[0:rows, :]
        b_id = _iota((rows, tt), 0)
        if nb < rows:
            blk = jnp.where(b_id < nb, blk, -jnp.inf)
        blocks.append(blk)
        ids.append(b_id + a * kk)
    return jnp.concatenate(blocks, axis=0), jnp.concatenate(ids, axis=0)


def _take_rows(table, sel):
    return jnp.sum(jnp.where(_iota(table.shape, 0) == sel, table, 0), axis=0, keepdims=True)


def _peer_topk_body(xf_ref, wqt_ref, keys_ref, i1_ref, i2_ref, g_ref):
    kk = PEER_TOPK
    half = PEER_DK // 2
    qt = _dot_nt(wqt_ref[...], xf_ref[...])
    i1_all, i2_all, g_all = [], [], []
    for h in range(PEER_HEADS):
        q1 = qt[h * PEER_DK:h * PEER_DK + half, :].astype(bf16)
        q2 = qt[h * PEER_DK + half:(h + 1) * PEER_DK, :].astype(bf16)
        v1, i1 = _topk_rows(_dot(keys_ref[h, 0], q1), kk)
        v2, i2 = _topk_rows(_dot(keys_ref[h, 1], q2), kk)
        cand, cand_id = _pair_candidates(v1, v2, qt.shape[1])
        sc, pos = _topk_rows(cand, kk, cand_id)
        sel1 = jnp.concatenate([_take_rows(i1, _div_pow2(pos[j:j + 1, :], kk)) for j in range(kk)], axis=0)
        sel2 = jnp.concatenate([_take_rows(i2, _mod_pow2(pos[j:j + 1, :], kk)) for j in range(kk)], axis=0)
        ex = jnp.exp(sc - sc[0:1, :])
        i1_all.append(sel1)
        i2_all.append(sel2)
        g_all.append(ex / jnp.sum(ex, axis=0, keepdims=True))
    i1_ref[...] = jnp.concatenate(i1_all, axis=0).T
    i2_ref[...] = jnp.concatenate(i2_all, axis=0).T
    g_ref[...] = jnp.concatenate(g_all, axis=0).T


def _peer_topk(xf, wqt, keys):
    t, dm = xf.shape
    tt = TOK_TILE
    const = lambda a: pl.BlockSpec(a.shape, lambda i: (0,) * a.ndim)
    slot = pl.BlockSpec((tt, PEER_SLOTS), lambda i: (i, 0))
    return pl.pallas_call(
        _peer_topk_body,
        grid=(t // tt,),
        in_specs=[pl.BlockSpec((tt, dm), lambda i: (i, 0)), const(wqt), const(keys)],
        out_specs=[slot, slot, slot],
        out_shape=[jax.ShapeDtypeStruct((t, PEER_SLOTS), i32), jax.ShapeDtypeStruct((t, PEER_SLOTS), i32),
                   jax.ShapeDtypeStruct((t, PEER_SLOTS), f32)],
        compiler_params=_params(("arbitrary",), VMEM_LIMIT),
        name="peer_topk",
    )(xf, wqt, keys)


def _peer_dense_body(xf_ref, i1_ref, i2_ref, g_ref, u_ref, v_ref, x1_ref, gfin_ref, o_ref,
                     gate_ref, acc_ref, *, final_norm):
    e = pl.program_id(1)
    tt = xf_ref.shape[0]
    nk = PEER_KEYS
    rows_per_tile = u_ref.shape[0] // nk

    @pl.when(e == 0)
    def _build_gates():
        acc_ref[...] = jnp.zeros_like(acc_ref)
        key_id = _iota((nk, PEER_SLOTS), 0)

        def token_group(gi, carry):
            t0 = pl.multiple_of(gi * GATE_GROUP, GATE_GROUP)
            r1 = i1_ref[pl.ds(t0, GATE_GROUP), :]
            r2 = i2_ref[pl.ds(t0, GATE_GROUP), :]
            gr = g_ref[pl.ds(t0, GATE_GROUP), :]
            mats = []
            for j in range(GATE_GROUP):
                p1 = jnp.where(r1[j:j + 1, :] == key_id, gr[j:j + 1, :], 0.0).astype(bf16)
                p2 = jnp.where(r2[j:j + 1, :] == key_id, 1.0, 0.0).astype(bf16)
                mats.append(_dot_nt(p1, p2))
            gate_ref[:, pl.ds(t0, GATE_GROUP), :] = jnp.swapaxes(jnp.stack(mats, axis=0), 0, 1)
            return carry

        lax.fori_loop(0, tt // GATE_GROUP, token_group, 0, unroll=2)

    a = _dot_nt(xf_ref[...], u_ref[...])
    act = jax.nn.gelu(a)
    ws = []
    for r in range(rows_per_tile):
        gt = gate_ref[e * rows_per_tile + r]
        ws.append((gt * act[:, r * nk:(r + 1) * nk]).astype(bf16))
    acc_ref[...] += _dot(jnp.concatenate(ws, axis=1), v_ref[...])

    @pl.when(e == pl.num_programs(1) - 1)
    def _finish():
        y = x1_ref[...] + acc_ref[...]
        if final_norm:
            y = y * lax.rsqrt(jnp.mean(y * y, axis=-1, keepdims=True) + EPS) * gfin_ref[...]
        o_ref[...] = y


def _peer_dense(xf, i1, i2, g, u_tab, v_tab, x1, g_final, final_norm):
    t, dm = xf.shape
    tt = TOK_TILE
    et = PEER_ETILE
    tok = lambda n: pl.BlockSpec((tt, n), lambda i, e: (i, 0))
    tab = pl.BlockSpec((et, dm), lambda i, e: (e, 0))
    return pl.pallas_call(
        functools.partial(_peer_dense_body, final_norm=final_norm),
        grid=(t // tt, u_tab.shape[0] // et),
        in_specs=[tok(dm), tok(PEER_SLOTS), tok(PEER_SLOTS), tok(PEER_SLOTS), tab, tab, tok(dm),
                  pl.BlockSpec(g_final.shape, lambda i, e: (0, 0))],
        out_specs=tok(dm),
        out_shape=jax.ShapeDtypeStruct((t, dm), f32),
        scratch_shapes=[pltpu.VMEM((PEER_KEYS, tt, PEER_KEYS), f32), pltpu.VMEM((tt, dm), f32)],
        compiler_params=_params(("arbitrary", "arbitrary"), VMEM_LIMIT),
        name="peer_dense",
    )(xf, i1, i2, g, u_tab, v_tab, x1, g_final)


def _pack_w_in(w_in):
    dm = w_in.shape[0]
    sizes = (D_FOX, D_FOX, D_FOX, H_FOX, D_GK, D_GK, D_GLA, GLA_RANK, D_GLA, D_SSM)
    pts = np.cumsum(sizes)[:-1].tolist()
    fq, fk, fv, ff, gq, gk, gv, ga, gr, su = jnp.split(w_in, pts, axis=1)
    z = lambda n: jnp.zeros((dm, n), w_in.dtype)
    tail = jnp.concatenate([ff, z(TAIL_GA - H_FOX), ga, z(LANES - TAIL_GA - GLA_RANK)], axis=1)
    packed = jnp.concatenate([fq, fk, fv, gq, z(C_GK - C_GQ - D_GK), gk, z(C_GV - C_GK - D_GK),
                              gv, gr, su, tail], axis=1)
    assert packed.shape[1] == D_INP
    return packed.astype(bf16)


def _block_diag(blocks):
    g, a, b = blocks.shape
    eye = jnp.eye(g, dtype=blocks.dtype)
    return (blocks[:, :, None, :] * eye[:, None, :, None]).reshape(g * a, g * b)


def _head_match(rows, row_w, cols, col_w):
    return (np.arange(rows)[:, None] // row_w == np.arange(cols)[None, :] // col_w)


def kernel(x_prompt, x_sample, cache_fox_k, cache_fox_v, cache_fox_logf, state_gla, state_ssm_re, state_ssm_im, page_table, g_mix, w_in, fox_bf, fox_norm, gla_wa2, gla_ba, gla_norm, ssm_a_re, ssm_a_im, ssm_log_dt, ssm_b_re, ssm_b_im, ssm_c_re, ssm_c_im, ssm_d, ssm_w_glu, ssm_b_glu, w_out, g_ffn, peer_wq, peer_keys, peer_u, peer_v, g_final):
    bp, seq, dm = x_prompt.shape
    bd, tdec, _ = x_sample.shape
    depth = w_in.shape[0]
    assert bp == 1 and seq % FOX_TILE == 0 and seq % GLA_STEP == 0
    n_s = bd * tdec
    assert n_s == TOK_TILE and tdec == 8
    n_tok = seq + n_s
    n_pool, page = cache_fox_k.shape[1], cache_fox_k.shape[2]
    assert page == LANES and page_table.shape[1] % FOX_PAGES == 0

    x = jnp.concatenate([x_prompt.reshape(seq, dm), x_sample.reshape(n_s, dm)], axis=0)
    ck = jnp.transpose(cache_fox_k, (0, 1, 3, 4, 2))
    cv = jnp.transpose(cache_fox_v, (0, 1, 3, 4, 2))
    clft = jnp.pad(jnp.swapaxes(cache_fox_logf, 2, 3), ((0, 0), (0, 0), (0, 8 - H_FOX), (0, 0)))

    e_mat = jnp.asarray(_head_match(D_GK, GLA_DK, D_GLA, GLA_DV), bf16)
    mask_t = jnp.asarray(_head_match(D_GLA, GLA_DV, D_GK, GLA_DK), f32)
    avg = jnp.asarray(_head_match(D_GLA, GLA_DV, D_GLA, GLA_DV), bf16)
    zeros_state = jnp.zeros((1, D_STATE), f32)

    outs_p, outs_s = [], []
    for l in range(depth):
        w_p = _pack_w_in(w_in[l])
        bfp = jnp.pad(fox_bf[l], (0, LANES - H_FOX)).reshape(1, LANES)
        wa2p = jnp.pad(gla_wa2[l], ((TAIL_GA, LANES - TAIL_GA - GLA_RANK), (0, 0))).astype(bf16)
        ba = gla_ba[l].reshape(1, D_GK)
        a_re = ssm_a_re[l].reshape(1, D_STATE)
        a_im = ssm_a_im[l].reshape(1, D_STATE)
        ldt = jnp.repeat(ssm_log_dt[l], SSM_P).reshape(1, D_STATE)
        bt_re = jnp.transpose(ssm_b_re[l], (2, 0, 1)).reshape(SSM_GC, D_STATE)
        bt_im = jnp.transpose(ssm_b_im[l], (2, 0, 1)).reshape(SSM_GC, D_STATE)
        p_re, p_im, bb_re, bb_im = _ssm_prep(a_re, a_im, ldt, bt_re, bt_im, TOK_TILE)
        to_blocks = lambda bb: jnp.transpose(bb.reshape(SSM_GC, SSM_G, SSM_P), (1, 0, 2))
        wb = jnp.concatenate([_block_diag(to_blocks(bb_re)), _block_diag(to_blocks(bb_im))], axis=1).astype(bf16)
        c_blocks = lambda cc: jnp.transpose(cc, (0, 2, 1))
        wc = jnp.concatenate([_block_diag(c_blocks(ssm_c_re[l])), -_block_diag(c_blocks(ssm_c_im[l]))],
                             axis=0).astype(bf16)
        dvec = ssm_d[l].reshape(1, D_SSM)
        wglu = ssm_w_glu[l].astype(bf16)
        bglu = ssm_b_glu[l].reshape(1, D_SSM)
        wqt = jnp.transpose(peer_wq[l].reshape(dm, PEER_HEADS * PEER_DK)).astype(bf16)
        keys = peer_keys[l].astype(bf16)
        u_tab = peer_u[l].astype(bf16)
        v_tab = peer_v[l].astype(bf16)

        (fq, qaug, kaug, fvb, fk, fv, lf, gq, gk, gv, la, gr, su) = _inproj(
            x, g_mix[l].reshape(1, dm), w_p, bfp, wa2p, ba)

        fox_p = _fox_prompt(qaug, kaug, fvb, seq)
        lfn_t = jnp.pad(jnp.swapaxes(lf[seq:, :8].reshape(bd, tdec, 8), 1, 2), ((0, 0), (0, 0), (0, LANES - tdec)))
        fox_s = _fox_sample(fq[seq:].astype(f32), fk[seq:], fv[seq:], lfn_t, ck, cv, clft, page_table, l, tdec)
        fox_o = jnp.concatenate([fox_p, fox_s], axis=0)

        gla_p, st_p = _gla(gq, gk, la, gv, jnp.zeros((1, D_GLA, D_GK), f32), e_mat, mask_t,
                           row0=0, nseq=1, steps=seq // GLA_STEP, rows_per_step=GLA_STEP, chunk=GLA_CHUNK)
        s0 = state_gla[l].astype(f32)
        s0t = jnp.transpose(s0, (0, 1, 3, 2))
        s0t = (s0t[:, :, :, None, :] * jnp.eye(H_GLA, dtype=f32)[None, :, None, :, None]).reshape(bd, D_GLA, D_GK)
        gla_s, st_s = _gla(gq, gk, la, gv, s0t, e_mat, mask_t,
                           row0=seq, nseq=bd, steps=1, rows_per_step=tdec, chunk=tdec)
        gla_o = jnp.concatenate([gla_p, gla_s], axis=0)

        def unpack_state(st):
            st = st.reshape(-1, H_GLA, GLA_DV, H_GLA, GLA_DK)
            st = jnp.stack([st[:, h, :, h, :] for h in range(H_GLA)], axis=1)
            return jnp.transpose(st, (0, 1, 3, 2))

        ssm_p, hre_p, him_p = _ssm(su, wb, p_re, p_im, zeros_state, zeros_state, wc, dvec, wglu, bglu,
                                   row0=0, n_rows=seq, seg=TOK_TILE, tile_carry=True)
        h0r = jnp.repeat(state_ssm_re[l].reshape(bd, D_STATE).astype(f32), tdec, axis=0)
        h0i = jnp.repeat(state_ssm_im[l].reshape(bd, D_STATE).astype(f32), tdec, axis=0)
        ps_re = jnp.tile(p_re[:tdec], (bd, 1))
        ps_im = jnp.tile(p_im[:tdec], (bd, 1))
        ssm_s, hre_s, him_s = _ssm(su, wb, ps_re, ps_im, h0r, h0i, wc, dvec, wglu, bglu,
                                   row0=seq, n_rows=n_s, seg=tdec, tile_carry=False)
        ssm_o = jnp.concatenate([ssm_p, ssm_s], axis=0)

        x1, xf = _outproj(fox_o, gla_o, gr, ssm_o, x, fox_norm[l].reshape(1, D_FOX),
                          jnp.tile(gla_norm[l], H_GLA).reshape(1, D_GLA), avg,
                          w_out[l].astype(bf16), g_ffn[l].reshape(1, dm))
        i1, i2, gate = _peer_topk(xf, wqt, keys)
        x = _peer_dense(xf, i1, i2, gate, u_tab, v_tab, x1, g_final.reshape(1, dm), l == depth - 1)

        outs_p.append((fk[:seq].reshape(bp, seq, H_FOX, FOX_DH), fv[:seq].reshape(bp, seq, H_FOX, FOX_DH),
                       lf[:seq, :H_FOX].reshape(bp, seq, H_FOX), unpack_state(st_p),
                       hre_p.reshape(bp, SSM_G, SSM_P), him_p.reshape(bp, SSM_G, SSM_P)))
        outs_s.append((fk[seq:].reshape(bd, tdec, H_FOX, FOX_DH), fv[seq:].reshape(bd, tdec, H_FOX, FOX_DH),
                       lf[seq:, :H_FOX].reshape(bd, tdec, H_FOX), unpack_state(st_s),
                       hre_s[tdec - 1::tdec].reshape(bd, SSM_G, SSM_P), him_s[tdec - 1::tdec].reshape(bd, SSM_G, SSM_P)))

    p_stk = [jnp.stack(t) for t in zip(*outs_p)]
    s_stk = [jnp.stack(t) for t in zip(*outs_s)]
    y_prompt = x[:seq].reshape(bp, seq, dm)
    y_sample = x[seq:].reshape(bd, tdec, dm)
    return (y_prompt, y_sample, *p_stk, *s_stk)
```

```python
import functools

import numpy as np
import jax
import jax.numpy as jnp
from jax import lax
from jax.experimental import pallas as pl
from jax.experimental.pallas import tpu as pltpu

f32 = jnp.float32
bf16 = jnp.bfloat16
i32 = jnp.int32

EPS = 1e-6
LOG2E = 1.4426950408889634
LANES = 128
VMEM_LIMIT = 56 * 1024 * 1024

H_FOX, FOX_DH = 6, 64
D_FOX = H_FOX * FOX_DH
H_GLA, GLA_DK, GLA_DV = 6, 32, 64
D_GK = H_GLA * GLA_DK
D_GLA = H_GLA * GLA_DV
GLA_RANK = 16
GLA_TAU = 16.0
SSM_G, SSM_GC, SSM_P = 16, 16, 64
D_SSM = SSM_G * SSM_GC
D_STATE = SSM_G * SSM_P
PEER_KEYS, PEER_HEADS, PEER_TOPK, PEER_DK = 128, 8, 16, 256
PEER_SLOTS = PEER_HEADS * PEER_TOPK

C_FQ, C_FK, C_FV = 0, 384, 768
C_GQ, C_GK = 1152, 1408
C_GV, C_GR, C_SU, C_TAIL = 1664, 2048, 2432, 2688
D_INP = 2816
TAIL_FF, TAIL_GA = 0, 8

TOK_TILE = 256
FOX_TILE = 1024
FOX_PAGES = 16
GLA_STEP = 512
GLA_CHUNK = 16
PEER_TOK = 512
PEER_ETILE = 512
GATE_GROUP = 8


def _dot(a, b):
    return jnp.dot(a, b, preferred_element_type=f32)


def _dot_nt(a, b):
    return lax.dot_general(a, b, (((1,), (1,)), ((), ())), preferred_element_type=f32)


def _dot_tn(a, b):
    return lax.dot_general(a, b, (((0,), (0,)), ((), ())), preferred_element_type=f32)


def _split3(x):
    h = x.astype(bf16)
    r = x - h.astype(f32)
    m = r.astype(bf16)
    l = (r - m.astype(f32)).astype(bf16)
    return h, m, l


def _dot3_lhs01(sel, x):
    h, m, l = _split3(x)
    return _dot(sel, h) + _dot(sel, m) + _dot(sel, l)


def _dot3_rhs01(x, sel):
    h, m, l = _split3(x)
    return _dot(h, sel) + _dot(m, sel) + _dot(l, sel)


def _log_sigmoid(x):
    return -(jnp.maximum(-x, 0.0) + jnp.log1p(jnp.exp(-jnp.abs(x))))


def _iota(shape, dim):
    return lax.broadcasted_iota(i32, shape, dim)


def _div_pow2(x, n):
    assert n & (n - 1) == 0
    return jnp.right_shift(x, n.bit_length() - 1)


def _mod_pow2(x, n):
    assert n & (n - 1) == 0
    return jnp.bitwise_and(x, n - 1)


def _params(sem, vmem=None):
    return pltpu.CompilerParams(dimension_semantics=sem, vmem_limit_bytes=vmem)


def _fox_bias_lanes():
    place_q = np.zeros((3, LANES, H_FOX * LANES), np.float32)
    place_k = np.zeros((3, LANES, H_FOX * LANES), np.float32)
    ones_q = np.zeros((1, H_FOX * LANES), np.float32)
    ones_k = np.zeros((1, H_FOX * LANES), np.float32)
    for j in range(H_FOX):
        base = j * LANES + (FOX_DH if j % 2 == 0 else 0)
        for p in range(3):
            place_q[p, j, base + p] = 1.0
            ones_k[0, base + p] = 1.0
            ones_q[0, base + 3 + p] = 1.0
            place_k[p, j, base + 3 + p] = 1.0
    return place_q, place_k, ones_q, ones_k


def _inproj_body(x_ref, g_ref, w_ref, bfp_ref, wa2_ref, ba_ref, pq_ref, pk_ref, oq_ref, ok_ref,
                 fq_ref, qa_ref, ka_ref, fvb_ref, fk_ref, fv_ref, lf_ref,
                 gq_ref, gk_ref, gv_ref, la_ref, gr_ref, su_ref, carry_ref):
    @pl.when(pl.program_id(0) == 0)
    def _init():
        carry_ref[...] = jnp.zeros_like(carry_ref)

    x = x_ref[...]
    tt = x.shape[0]
    xn = x * lax.rsqrt(jnp.mean(x * x, axis=-1, keepdims=True) + EPS) * g_ref[...]
    z = _dot(xn.astype(bf16), w_ref[...])
    fq_ref[...] = (z[:, C_FQ:C_FQ + D_FOX] * (FOX_DH ** -0.5)).astype(bf16)
    fk = z[:, C_FK:C_FK + D_FOX]
    fv = z[:, C_FV:C_FV + D_FOX]
    fk_ref[...] = fk
    fv_ref[...] = fv
    fvb_ref[...] = fv.astype(bf16)
    gq_ref[...] = z[:, C_GQ:C_GQ + D_GK] * (GLA_DK ** -0.5)
    gk_ref[...] = z[:, C_GK:C_GK + D_GK]
    gv_ref[...] = z[:, C_GV:C_GV + D_GLA]
    gr_ref[...] = z[:, C_GR:C_GR + D_GLA]
    su_ref[...] = z[:, C_SU:C_SU + D_SSM]
    tail = z[:, C_TAIL:C_TAIL + LANES]
    lane = _iota((tt, LANES), 1)
    lf = jnp.where(lane < H_FOX, _log_sigmoid(tail + bfp_ref[...]), 0.0)
    lf_ref[...] = lf
    tri = jnp.where(_iota((tt, tt), 0) >= _iota((tt, tt), 1), 1.0, 0.0).astype(bf16)
    cs = _dot3_lhs01(tri, lf) + carry_ref[...]
    carry_ref[...] = cs[tt - 1:tt, :]
    pieces = _split3(cs * LOG2E)
    q_bias = sum(_dot(pc, pq_ref[i]) for i, pc in enumerate(pieces)) + oq_ref[...]
    k_bias = ok_ref[...] - sum(_dot(pc, pk_ref[i]) for i, pc in enumerate(pieces))
    low = lane < FOX_DH
    for hp in range(H_FOX // 2):
        qt = z[:, C_FQ + hp * LANES:C_FQ + (hp + 1) * LANES] * (LOG2E * FOX_DH ** -0.5)
        kt = z[:, C_FK + hp * LANES:C_FK + (hp + 1) * LANES]
        for hh in range(2):
            blk = slice((2 * hp + hh) * LANES, (2 * hp + hh + 1) * LANES)
            real = low if hh == 0 else jnp.logical_not(low)
            qa_ref[:, blk] = jnp.where(real, qt, q_bias[:, blk]).astype(bf16)
            ka_ref[:, blk] = jnp.where(real, kt, k_bias[:, blk]).astype(bf16)
    za = _dot(tail.astype(bf16), wa2_ref[...]) + ba_ref[...]
    la_ref[...] = _log_sigmoid(za) * (1.0 / GLA_TAU)


def _inproj(x, g_mix, w_p, bfp, wa2p, ba):
    t = x.shape[0]
    tt = TOK_TILE
    row = lambda n: pl.BlockSpec((tt, n), lambda i: (i, 0))
    const = lambda a: pl.BlockSpec(a.shape, lambda i: (0,) * a.ndim)
    outs = [
        (D_FOX, bf16), (H_FOX * LANES, bf16), (H_FOX * LANES, bf16), (D_FOX, bf16), (D_FOX, f32), (D_FOX, f32),
        (LANES, f32),
        (D_GK, f32), (D_GK, f32), (D_GLA, f32), (D_GK, f32), (D_GLA, f32), (D_SSM, f32),
    ]
    place_q, place_k, ones_q, ones_k = _fox_bias_lanes()
    consts = [g_mix, w_p, bfp, wa2p, ba, jnp.asarray(place_q, bf16), jnp.asarray(place_k, bf16),
              jnp.asarray(ones_q), jnp.asarray(ones_k)]
    return pl.pallas_call(
        _inproj_body,
        grid=(t // tt,),
        in_specs=[row(x.shape[1])] + [const(a) for a in consts],
        out_specs=[row(n) for n, _ in outs],
        out_shape=[jax.ShapeDtypeStruct((t, n), d) for n, d in outs],
        scratch_shapes=[pltpu.VMEM((1, LANES), f32)],
        compiler_params=_params(("arbitrary",), VMEM_LIMIT),
        name="inproj",
    )(x, *consts)


def _fox_prompt_body(qi_ref, ki_ref, qa_ref, qb_ref, ka_ref, kb_ref, v_ref, o_ref,
                     m_ref, l_ref, acc_ref):
    p = pl.program_id(1)
    qi = qi_ref[p]
    ki = ki_ref[p]
    tq, tk = qa_ref.shape[0], ka_ref.shape[0]

    @pl.when(ki == 0)
    def _init():
        m_ref[...] = jnp.full_like(m_ref, -jnp.inf)
        l_ref[...] = jnp.zeros_like(l_ref)
        acc_ref[...] = jnp.zeros_like(acc_ref)

    def step(diagonal):
        vext = jnp.concatenate([v_ref[...], jnp.ones((tk, LANES), bf16)], axis=1)
        for hh, (q_ref, k_ref) in enumerate(((qa_ref, ka_ref), (qb_ref, kb_ref))):
            s = _dot_nt(q_ref[...], k_ref[...])
            if diagonal:
                s = jnp.where(_iota((tq, tk), 0) >= _iota((tq, tk), 1), s, -jnp.inf)
            m_old = m_ref[hh]
            m_new = jnp.maximum(m_old, jnp.max(s, axis=-1, keepdims=True))
            pr = jnp.exp2(s - jnp.tile(m_new, (1, tk // LANES)))
            alpha = jnp.exp2(m_old - m_new)
            pv = _dot(pr.astype(bf16), vext)
            l_ref[hh] = alpha * l_ref[hh] + pv[:, LANES:]
            m_ref[hh] = m_new
            acc_ref[hh] = alpha * acc_ref[hh] + pv[:, :LANES]

    @pl.when(ki < qi)
    def _off():
        step(False)

    @pl.when(ki == qi)
    def _diag():
        step(True)
        lane = _iota((1, LANES), 1)
        o_ref[...] = jnp.where(lane < FOX_DH, acc_ref[0] / l_ref[0], acc_ref[1] / l_ref[1])


def _fox_prompt(qaug, kaug, fvb, seq):
    tq = FOX_TILE
    nq = seq // tq
    qi_tbl = np.concatenate([np.full(i + 1, i) for i in range(nq)]).astype(np.int32)
    ki_tbl = np.concatenate([np.arange(i + 1) for i in range(nq)]).astype(np.int32)
    grid_spec = pltpu.PrefetchScalarGridSpec(
        num_scalar_prefetch=2,
        grid=(H_FOX // 2, len(qi_tbl)),
        in_specs=[
            pl.BlockSpec((tq, LANES), lambda h, p, qi, ki: (qi[p], 2 * h)),
            pl.BlockSpec((tq, LANES), lambda h, p, qi, ki: (qi[p], 2 * h + 1)),
            pl.BlockSpec((tq, LANES), lambda h, p, qi, ki: (ki[p], 2 * h)),
            pl.BlockSpec((tq, LANES), lambda h, p, qi, ki: (ki[p], 2 * h + 1)),
            pl.BlockSpec((tq, LANES), lambda h, p, qi, ki: (ki[p], h)),
        ],
        out_specs=pl.BlockSpec((tq, LANES), lambda h, p, qi, ki: (qi[p], h)),
        scratch_shapes=[pltpu.VMEM((2, tq, LANES), f32), pltpu.VMEM((2, tq, LANES), f32),
                        pltpu.VMEM((2, tq, LANES), f32)],
    )
    return pl.pallas_call(
        _fox_prompt_body,
        grid_spec=grid_spec,
        out_shape=jax.ShapeDtypeStruct((seq, D_FOX), f32),
        compiler_params=_params(("arbitrary", "arbitrary"), VMEM_LIMIT),
        name="fox_prompt",
    )(jnp.asarray(qi_tbl), jnp.asarray(ki_tbl), qaug, qaug, kaug, kaug, fvb)


def _expand_heads(x, rows):
    return jnp.concatenate(
        [jnp.broadcast_to(x[h:h + 1, :], (rows, x.shape[1])) for h in range(H_FOX)], axis=0)


def _fox_sample_body(pt_ref, q_ref, kn_ref, vn_ref, lfn_ref, *rest):
    del pt_ref
    npg = FOX_PAGES
    kp_refs = rest[0:npg]
    vp_refs = rest[npg:2 * npg]
    lfp_refs = rest[2 * npg:3 * npg]
    o_ref = rest[3 * npg]
    m_ref, l_ref, acc_ref, car_ref, qbd_ref, cq_ref = rest[3 * npg + 1:]
    j = pl.program_id(1)
    t = q_ref.shape[0]
    hq = H_FOX * t
    page = kp_refs[0].shape[-1]

    def update(s, weigh):
        m_old = m_ref[...]
        m_new = jnp.maximum(m_old, jnp.max(s, axis=-1, keepdims=True))
        pr = jnp.exp(s - m_new)
        alpha = jnp.exp(m_old - m_new)
        l_ref[...] = alpha * l_ref[...] + jnp.sum(pr, axis=-1, keepdims=True)
        m_ref[...] = m_new
        acc_ref[...] = alpha * acc_ref[...] + weigh(pr.astype(bf16))

    @pl.when(j == 0)
    def _new_tokens():
        m_ref[...] = jnp.full_like(m_ref, -jnp.inf)
        l_ref[...] = jnp.zeros_like(l_ref)
        acc_ref[...] = jnp.zeros_like(acc_ref)
        car_ref[...] = jnp.zeros_like(car_ref)
        q = q_ref[...]
        q6 = jnp.concatenate([q] * H_FOX, axis=0)
        keep = _div_pow2(_iota((hq, D_FOX), 0), t) == _div_pow2(_iota((hq, D_FOX), 1), FOX_DH)
        qbd_ref[...] = jnp.where(keep, q6, 0.0).astype(bf16)
        cum = lfn_ref[...]
        lane8 = _iota(cum.shape, 1)
        sh = 1
        while sh < t:
            cum = cum + jnp.where(lane8 >= sh, pltpu.roll(cum, sh, 1), 0.0)
            sh *= 2
        cum_rows = _expand_heads(cum, t)
        qpos = _mod_pow2(_iota((hq, LANES), 0), t)
        kpos = _iota((hq, LANES), 1)
        cq = jnp.sum(jnp.where(kpos == qpos, cum_rows, 0.0), axis=-1, keepdims=True)
        cq_ref[...] = cq
        zpad = jnp.zeros((LANES - t, D_FOX), f32)
        kn = jnp.concatenate([kn_ref[...], zpad], axis=0).astype(bf16)
        vn = jnp.concatenate([vn_ref[...], zpad], axis=0).astype(bf16)
        s = _dot_nt(qbd_ref[...], kn) + cq - cum_rows
        s = jnp.where(kpos <= qpos, s, -jnp.inf)
        update(s, lambda pr: _dot(pr, vn))

    upper = jnp.where(_iota((page, page), 0) > _iota((page, page), 1), 1.0, 0.0).astype(bf16)
    qbd = qbd_ref[...]
    car = car_ref[...]
    scores = []
    for g in range(npg):
        lfp = lfp_refs[g][...]
        suffix = _dot3_rhs01(lfp, upper) + car
        car = car + jnp.sum(lfp, axis=-1, keepdims=True)
        kpt = kp_refs[g][...].reshape(D_FOX, page).astype(bf16)
        scores.append(_dot(qbd, kpt) + _expand_heads(suffix, t))
    car_ref[...] = car

    def weigh_pages(pr):
        out = None
        for g in range(npg):
            vpt = vp_refs[g][...].reshape(D_FOX, page).astype(bf16)
            part = _dot_nt(pr[:, g * page:(g + 1) * page], vpt)
            out = part if out is None else out + part
        return out

    update(jnp.concatenate(scores, axis=1) + cq_ref[...], weigh_pages)

    @pl.when(j == pl.num_programs(1) - 1)
    def _finish():
        o48 = acc_ref[...] / l_ref[...]
        head = _div_pow2(_iota((t, D_FOX), 1), FOX_DH)
        out = jnp.zeros((t, D_FOX), f32)
        for h in range(H_FOX):
            out = out + jnp.where(head == h, o48[h * t:(h + 1) * t, :], 0.0)
        o_ref[...] = out


def _fox_sample(q_s, k_s, v_s, lfn_t, cache_k, cache_v, cache_lft, page_table, layer, t):
    nb, n_pages = page_table.shape
    page = cache_k.shape[-1]
    npg = FOX_PAGES
    hq = H_FOX * t
    seq_spec = pl.BlockSpec((t, D_FOX), lambda b, j, pt: (b, 0))

    def page_spec(shape, g):
        zeros = (0,) * len(shape)
        return pl.BlockSpec((None, None) + shape,
                            lambda b, j, pt: (layer, pt[b, n_pages - 1 - (j * npg + g)]) + zeros)

    grid_spec = pltpu.PrefetchScalarGridSpec(
        num_scalar_prefetch=1,
        grid=(nb, n_pages // npg),
        in_specs=[seq_spec, seq_spec, seq_spec,
                  pl.BlockSpec((None, 8, LANES), lambda b, j, pt: (b, 0, 0))]
        + [page_spec((H_FOX, FOX_DH, page), g) for g in range(npg)]
        + [page_spec((H_FOX, FOX_DH, page), g) for g in range(npg)]
        + [page_spec((8, page), g) for g in range(npg)],
        out_specs=seq_spec,
        scratch_shapes=[pltpu.VMEM((hq, 1), f32), pltpu.VMEM((hq, 1), f32),
                        pltpu.VMEM((hq, D_FOX), f32), pltpu.VMEM((8, 1), f32),
                        pltpu.VMEM((hq, D_FOX), bf16), pltpu.VMEM((hq, 1), f32)],
    )
    return pl.pallas_call(
        _fox_sample_body,
        grid_spec=grid_spec,
        out_shape=jax.ShapeDtypeStruct((nb * t, D_FOX), f32),
        compiler_params=_params(("arbitrary", "arbitrary"), VMEM_LIMIT),
        name="fox_sample",
    )(page_table, q_s, k_s, v_s, lfn_t, *([cache_k] * npg), *([cache_v] * npg), *([cache_lft] * npg))


def _gla_body(q_ref, k_ref, la_ref, v_ref, s0_ref, e_ref, mask_ref, o_ref, sout_ref, st_ref, *, chunk):
    i = pl.program_id(1)

    @pl.when(i == 0)
    def _init():
        st_ref[...] = s0_ref[...]

    c = chunk
    n_chunks = q_ref.shape[0] // c
    row = _iota((c, D_GK), 0)

    def one_chunk(ci, carry):
        r0 = pl.multiple_of(ci * c, c)
        q = q_ref[pl.ds(r0, c), :]
        k = k_ref[pl.ds(r0, c), :]
        v = v_ref[pl.ds(r0, c), :]
        b = la_ref[pl.ds(r0, c), :]
        sh = 1
        while sh < c:
            b = b + jnp.where(row >= sh, pltpu.roll(b, sh, 0), 0.0)
            sh *= 2
        st = st_ref[...]
        o = _dot_nt((q * jnp.exp(b)).astype(bf16), st.astype(bf16))
        pieces = []
        for s in range(c):
            d = jnp.where(row >= s, b - b[s:s + 1, :], -jnp.inf)
            pieces.append(q * k[s:s + 1, :] * jnp.exp(d))
        pst = jnp.concatenate(pieces, axis=0)
        hi = pst.astype(bf16)
        lo = (pst - hi.astype(f32)).astype(bf16)
        att = _dot(hi, e_ref[...]) + _dot(lo, e_ref[...])
        for s in range(c):
            o = o + att[s * c:(s + 1) * c, :] * v[s:s + 1, :]
        o_ref[pl.ds(r0, c), :] = o
        b_last = b[c - 1:c, :]
        kd = (k * jnp.exp(b_last - b)).astype(bf16)
        st_ref[...] = st * jnp.exp(b_last) + _dot_tn(v.astype(bf16), kd) * mask_ref[...]
        return carry

    lax.fori_loop(0, n_chunks, one_chunk, 0)

    @pl.when(i == pl.num_programs(1) - 1)
    def _finish():
        sout_ref[...] = st_ref[...]


def _gla(gq, gk, la, gv, s0t, e_mat, mask_t, *, row0, nseq, steps, rows_per_step, chunk):
    blk0 = row0 // rows_per_step
    rows = lambda n: pl.BlockSpec((rows_per_step, n), lambda s, i: (blk0 + s * steps + i, 0))
    orow = pl.BlockSpec((rows_per_step, D_GLA), lambda s, i: (s * steps + i, 0))
    st_spec = pl.BlockSpec((None, D_GLA, D_GK), lambda s, i: (s, 0, 0))
    const = lambda a: pl.BlockSpec(a.shape, lambda s, i: (0,) * a.ndim)
    return pl.pallas_call(
        functools.partial(_gla_body, chunk=chunk),
        grid=(nseq, steps),
        in_specs=[rows(D_GK), rows(D_GK), rows(D_GK), rows(D_GLA), st_spec, const(e_mat), const(mask_t)],
        out_specs=[orow, st_spec],
        out_shape=[jax.ShapeDtypeStruct((nseq * steps * rows_per_step, D_GLA), f32),
                   jax.ShapeDtypeStruct((nseq, D_GLA, D_GK), f32)],
        scratch_shapes=[pltpu.VMEM((D_GLA, D_GK), f32)],
        compiler_params=_params(("arbitrary", "arbitrary"), VMEM_LIMIT),
        name="gla",
    )(gq, gk, la, gv, s0t, e_mat, mask_t)


def _ssm_prep_body(are_ref, aim_ref, ldt_ref, bre_ref, bim_ref,
                   pre_ref, pim_ref, bbre_ref, bbim_ref):
    a_re = are_ref[...]
    a_im = aim_ref[...]
    dt = jnp.exp(ldt_ref[...])
    mag = jnp.exp(a_re * dt)
    abar_re = mag * jnp.cos(a_im * dt)
    abar_im = mag * jnp.sin(a_im * dt)
    den = a_re * a_re + a_im * a_im
    nr, ni = abar_re - 1.0, abar_im
    fr = (nr * a_re + ni * a_im) / den
    fi = (ni * a_re - nr * a_im) / den
    bbre_ref[...] = fr * bre_ref[...] - fi * bim_ref[...]
    bbim_ref[...] = fr * bim_ref[...] + fi * bre_ref[...]
    n = pre_ref.shape[0]
    pre_ref[0:1, :] = abar_re
    pim_ref[0:1, :] = abar_im
    m = 1
    while m < n:
        tr = pre_ref[m - 1:m, :]
        ti = pim_ref[m - 1:m, :]
        xr = pre_ref[0:m, :]
        xi = pim_ref[0:m, :]
        pre_ref[m:2 * m, :] = xr * tr - xi * ti
        pim_ref[m:2 * m, :] = xr * ti + xi * tr
        m *= 2


def _ssm_prep(a_re, a_im, log_dt, bt_re, bt_im, n_pow):
    outs = [(n_pow, D_STATE), (n_pow, D_STATE), (SSM_GC, D_STATE), (SSM_GC, D_STATE)]
    return pl.pallas_call(
        _ssm_prep_body,
        out_shape=[jax.ShapeDtypeStruct(s, f32) for s in outs],
        name="ssm_prep",
    )(a_re, a_im, log_dt, bt_re, bt_im)


def _ssm_body(u_ref, wb_ref, pre_ref, pim_ref, hin_re_ref, hin_im_ref, wc_ref, d_ref, wg_ref, bg_ref,
              o_ref, hre_ref, him_ref, car_re_ref, car_im_ref, *, tile_carry):
    i = pl.program_id(0)
    u = u_ref[...]
    tt = u.shape[0]
    grp = pre_ref.shape[0]
    d = _dot(u.astype(bf16), wb_ref[...])
    xr = d[:, :D_STATE]
    xi = d[:, D_STATE:]
    pos = _mod_pow2(_iota((tt, D_STATE), 0), grp)
    sh = 1
    while sh < grp:
        ar = pre_ref[sh - 1:sh, :]
        ai = pim_ref[sh - 1:sh, :]
        sr = jnp.where(pos >= sh, pltpu.roll(xr, sh, 0), 0.0)
        si = jnp.where(pos >= sh, pltpu.roll(xi, sh, 0), 0.0)
        xr, xi = xr + ar * sr - ai * si, xi + ar * si + ai * sr
        sh *= 2
    pr = pre_ref[...]
    pi = pim_ref[...]
    if tile_carry:
        @pl.when(i == 0)
        def _init():
            car_re_ref[...] = hin_re_ref[...]
            car_im_ref[...] = hin_im_ref[...]
        cr = car_re_ref[...]
        ci = car_im_ref[...]
        hr_groups, hi_groups = [], []
        for g in range(tt // grp):
            gr = xr[g * grp:(g + 1) * grp, :] + pr * cr - pi * ci
            gi = xi[g * grp:(g + 1) * grp, :] + pr * ci + pi * cr
            cr, ci = gr[grp - 1:grp, :], gi[grp - 1:grp, :]
            hr_groups.append(gr)
            hi_groups.append(gi)
        hr = jnp.concatenate(hr_groups, axis=0)
        hi = jnp.concatenate(hi_groups, axis=0)
        car_re_ref[...] = cr
        car_im_ref[...] = ci
        hre_ref[...] = cr
        him_ref[...] = ci
    else:
        h0r = hin_re_ref[...]
        h0i = hin_im_ref[...]
        prt = jnp.tile(pr, (tt // grp, 1))
        pit = jnp.tile(pi, (tt // grp, 1))
        hr = xr + prt * h0r - pit * h0i
        hi = xi + prt * h0i + pit * h0r
        hre_ref[...] = hr
        him_ref[...] = hi
    hcat = jnp.concatenate([hr, hi], axis=1).astype(bf16)
    y = _dot(hcat, wc_ref[...]) + d_ref[...] * u
    zg = jax.nn.gelu(y)
    o_ref[...] = (zg * jax.nn.sigmoid(_dot(zg.astype(bf16), wg_ref[...]) + bg_ref[...])).astype(bf16)


def _ssm(su, wb, p_re, p_im, hin_re, hin_im, wc, dvec, wglu, bglu, *, row0, n_rows, tile_carry):
    tt = TOK_TILE
    blk0 = row0 // tt
    const = lambda a: pl.BlockSpec(a.shape, lambda i: (0,) * a.ndim)
    if tile_carry:
        h_spec = const(hin_re)
        hout_spec = pl.BlockSpec((1, D_STATE), lambda i: (0, 0))
        hout_shape = jax.ShapeDtypeStruct((1, D_STATE), f32)
    else:
        h_spec = pl.BlockSpec((tt, D_STATE), lambda i: (i, 0))
        hout_spec = h_spec
        hout_shape = jax.ShapeDtypeStruct((n_rows, D_STATE), f32)
    return pl.pallas_call(
        functools.partial(_ssm_body, tile_carry=tile_carry),
        grid=(n_rows // tt,),
        in_specs=[pl.BlockSpec((tt, D_SSM), lambda i: (blk0 + i, 0)), const(wb), const(p_re), const(p_im),
                  h_spec, h_spec, const(wc), const(dvec), const(wglu), const(bglu)],
        out_specs=[pl.BlockSpec((tt, D_SSM), lambda i: (i, 0)), hout_spec, hout_spec],
        out_shape=[jax.ShapeDtypeStruct((n_rows, D_SSM), bf16), hout_shape, hout_shape],
        scratch_shapes=[pltpu.VMEM((1, D_STATE), f32), pltpu.VMEM((1, D_STATE), f32)],
        compiler_params=_params(("arbitrary",), VMEM_LIMIT),
        name="ssm",
    )(su, wb, p_re, p_im, hin_re, hin_im, wc, dvec, wglu, bglu)


def _outproj_body(fox_ref, gla_ref, gr_ref, ssm_ref, x_ref, fn_ref, gn_ref, avg_ref, w_ref, gf_ref,
                  x1_ref, xf_ref):
    fo = fox_ref[...]
    fo = fo * lax.rsqrt(jnp.mean(fo * fo, axis=-1, keepdims=True) + EPS) * fn_ref[...]
    go = gla_ref[...]
    ms = _dot3_rhs01(go * go, avg_ref[...]) * (1.0 / GLA_DV)
    go = go * lax.rsqrt(ms + EPS) * gn_ref[...] * jax.nn.silu(gr_ref[...])
    w = w_ref
    y = (_dot(fo.astype(bf16), w[0:D_FOX, :]) + _dot(go.astype(bf16), w[D_FOX:D_FOX + D_GLA, :])
         + _dot(ssm_ref[...], w[D_FOX + D_GLA:, :]))
    x1 = x_ref[...] + y
    x1_ref[...] = x1
    xf = x1 * lax.rsqrt(jnp.mean(x1 * x1, axis=-1, keepdims=True) + EPS) * gf_ref[...]
    xf_ref[...] = xf.astype(bf16)


def _outproj(fox_o, gla_o, gr, ssm_o, x, fox_norm, gla_norm6, avg, w_out, g_ffn):
    t, dm = x.shape
    tt = TOK_TILE
    row = lambda n: pl.BlockSpec((tt, n), lambda i: (i, 0))
    const = lambda a: pl.BlockSpec(a.shape, lambda i: (0,) * a.ndim)
    return pl.pallas_call(
        _outproj_body,
        grid=(t // tt,),
        in_specs=[row(D_FOX), row(D_GLA), row(D_GLA), row(D_SSM), row(dm),
                  const(fox_norm), const(gla_norm6), const(avg), const(w_out), const(g_ffn)],
        out_specs=[row(dm), row(dm)],
        out_shape=[jax.ShapeDtypeStruct((t, dm), f32), jax.ShapeDtypeStruct((t, dm), bf16)],
        compiler_params=_params(("arbitrary",), VMEM_LIMIT),
        name="outproj",
    )(fox_o, gla_o, gr, ssm_o, x, fox_norm, gla_norm6, avg, w_out, g_ffn)


def _topk_rows(s, k, ids=None):
    it = _iota(s.shape, 0) if ids is None else ids
    big = jnp.iinfo(jnp.int32).max
    vals, idxs = [], []
    for _ in range(k):
        m = jnp.max(s, axis=0, keepdims=True)
        idx = jnp.min(jnp.where(s == m, it, big), axis=0, keepdims=True)
        vals.append(m)
        idxs.append(idx)
        s = jnp.where(it == idx, -jnp.inf, s)
    return jnp.concatenate(vals, axis=0), jnp.concatenate(idxs, axis=0)


def _pair_candidates(v1, v2, tt):
    kk = v1.shape[0]
    blocks, ids = [], []
    for a in range(kk):
        nb = kk // (a + 1)
        rows = -(-nb // 8) * 8
        blk = v1[a:a + 1, :] + v2[0:rows, :]
        b_id = _iota((rows, tt), 0)
        if nb < rows:
            blk = jnp.where(b_id < nb, blk, -jnp.inf)
        blocks.append(blk)
        ids.append(b_id + a * kk)
    return jnp.concatenate(blocks, axis=0), jnp.concatenate(ids, axis=0)


def _take_rows(table, sel):
    return jnp.sum(jnp.where(_iota(table.shape, 0) == sel, table, 0), axis=0, keepdims=True)


def _peer_topk_body(xf_ref, wqt_ref, keys_ref, i1_ref, i2_ref, g_ref):
    kk = PEER_TOPK
    half = PEER_DK // 2
    qt = _dot_nt(wqt_ref[...], xf_ref[...])
    i1_all, i2_all, g_all = [], [], []
    for h in range(PEER_HEADS):
        q1 = qt[h * PEER_DK:h * PEER_DK + half, :].astype(bf16)
        q2 = qt[h * PEER_DK + half:(h + 1) * PEER_DK, :].astype(bf16)
        v1, i1 = _topk_rows(_dot(keys_ref[h, 0], q1), kk)
        v2, i2 = _topk_rows(_dot(keys_ref[h, 1], q2), kk)
        cand, cand_id = _pair_candidates(v1, v2, qt.shape[1])
        sc, pos = _topk_rows(cand, kk, cand_id)
        sel1 = jnp.concatenate([_take_rows(i1, _div_pow2(pos[j:j + 1, :], kk)) for j in range(kk)], axis=0)
        sel2 = jnp.concatenate([_take_rows(i2, _mod_pow2(pos[j:j + 1, :], kk)) for j in range(kk)], axis=0)
        ex = jnp.exp(sc - sc[0:1, :])
        i1_all.append(sel1)
        i2_all.append(sel2)
        g_all.append(ex / jnp.sum(ex, axis=0, keepdims=True))
    i1_ref[...] = jnp.concatenate(i1_all, axis=0).T
    i2_ref[...] = jnp.concatenate(i2_all, axis=0).T
    g_ref[...] = jnp.concatenate(g_all, axis=0).T


def _peer_topk(xf, wqt, keys):
    t, dm = xf.shape
    tt = TOK_TILE
    const = lambda a: pl.BlockSpec(a.shape, lambda i: (0,) * a.ndim)
    slot = pl.BlockSpec((tt, PEER_SLOTS), lambda i: (i, 0))
    return pl.pallas_call(
        _peer_topk_body,
        grid=(t // tt,),
        in_specs=[pl.BlockSpec((tt, dm), lambda i: (i, 0)), const(wqt), const(keys)],
        out_specs=[slot, slot, slot],
        out_shape=[jax.ShapeDtypeStruct((t, PEER_SLOTS), i32), jax.ShapeDtypeStruct((t, PEER_SLOTS), i32),
                   jax.ShapeDtypeStruct((t, PEER_SLOTS), f32)],
        compiler_params=_params(("arbitrary",), VMEM_LIMIT),
        name="peer_topk",
    )(xf, wqt, keys)


def _peer_dense_body(xf_ref, i1_ref, i2_ref, g_ref, u_ref, v_ref, x1_ref, gfin_ref, o_ref,
                     gate_ref, acc_ref, *, final_norm):
    e = pl.program_id(1)
    tt = xf_ref.shape[0]
    nk = PEER_KEYS
    rows_per_tile = u_ref.shape[0] // nk

    @pl.when(e == 0)
    def _build_gates():
        acc_ref[...] = jnp.zeros_like(acc_ref)
        key_id = _iota((nk, PEER_SLOTS), 0)

        def token_group(gi, carry):
            t0 = pl.multiple_of(gi * GATE_GROUP, GATE_GROUP)
            r1 = i1_ref[pl.ds(t0, GATE_GROUP), :]
            r2 = i2_ref[pl.ds(t0, GATE_GROUP), :]
            gr = g_ref[pl.ds(t0, GATE_GROUP), :]
            mats = []
            for j in range(GATE_GROUP):
                p1 = jnp.where(r1[j:j + 1, :] == key_id, gr[j:j + 1, :], 0.0).astype(bf16)
                p2 = jnp.where(r2[j:j + 1, :] == key_id, 1.0, 0.0).astype(bf16)
                mats.append(_dot_nt(p1, p2))
            gate_ref[:, pl.ds(t0, GATE_GROUP), :] = jnp.swapaxes(jnp.stack(mats, axis=0), 0, 1)
            return carry

        lax.fori_loop(0, tt // GATE_GROUP, token_group, 0, unroll=2)

    a = _dot_nt(xf_ref[...], u_ref[...])
    act = jax.nn.gelu(a)
    ws = []
    for r in range(rows_per_tile):
        gt = gate_ref[e * rows_per_tile + r]
        ws.append((gt * act[:, r * nk:(r + 1) * nk]).astype(bf16))
    acc_ref[...] += _dot(jnp.concatenate(ws, axis=1), v_ref[...])

    @pl.when(e == pl.num_programs(1) - 1)
    def _finish():
        y = x1_ref[...] + acc_ref[...]
        if final_norm:
            y = y * lax.rsqrt(jnp.mean(y * y, axis=-1, keepdims=True) + EPS) * gfin_ref[...]
        o_ref[...] = y


def _peer_dense(xf, i1, i2, g, u_tab, v_tab, x1, g_final, final_norm):
    t, dm = xf.shape
    tt = PEER_TOK
    et = PEER_ETILE
    tok = lambda n: pl.BlockSpec((tt, n), lambda i, e: (i, 0))
    tab = pl.BlockSpec((et, dm), lambda i, e: (e, 0))
    return pl.pallas_call(
        functools.partial(_peer_dense_body, final_norm=final_norm),
        grid=(t // tt, u_tab.shape[0] // et),
        in_specs=[tok(dm), tok(PEER_SLOTS), tok(PEER_SLOTS), tok(PEER_SLOTS), tab, tab, tok(dm),
                  pl.BlockSpec(g_final.shape, lambda i, e: (0, 0))],
        out_specs=tok(dm),
        out_shape=jax.ShapeDtypeStruct((t, dm), f32),
        scratch_shapes=[pltpu.VMEM((PEER_KEYS, tt, PEER_KEYS), f32), pltpu.VMEM((tt, dm), f32)],
        compiler_params=_params(("arbitrary", "arbitrary"), VMEM_LIMIT),
        name="peer_dense",
    )(xf, i1, i2, g, u_tab, v_tab, x1, g_final)


def _pack_w_in(w_in):
    dm = w_in.shape[0]
    sizes = (D_FOX, D_FOX, D_FOX, H_FOX, D_GK, D_GK, D_GLA, GLA_RANK, D_GLA, D_SSM)
    pts = np.cumsum(sizes)[:-1].tolist()
    fq, fk, fv, ff, gq, gk, gv, ga, gr, su = jnp.split(w_in, pts, axis=1)
    z = lambda n: jnp.zeros((dm, n), w_in.dtype)
    tail = jnp.concatenate([ff, z(TAIL_GA - H_FOX), ga, z(LANES - TAIL_GA - GLA_RANK)], axis=1)
    packed = jnp.concatenate([fq, fk, fv, gq, z(C_GK - C_GQ - D_GK), gk, z(C_GV - C_GK - D_GK),
                              gv, gr, su, tail], axis=1)
    assert packed.shape[1] == D_INP
    return packed.astype(bf16)


def _block_diag(blocks):
    g, a, b = blocks.shape
    eye = jnp.eye(g, dtype=blocks.dtype)
    return (blocks[:, :, None, :] * eye[:, None, :, None]).reshape(g * a, g * b)


def _head_match(rows, row_w, cols, col_w):
    return (np.arange(rows)[:, None] // row_w == np.arange(cols)[None, :] // col_w)


def kernel(x_prompt, x_sample, cache_fox_k, cache_fox_v, cache_fox_logf, state_gla, state_ssm_re, state_ssm_im, page_table, g_mix, w_in, fox_bf, fox_norm, gla_wa2, gla_ba, gla_norm, ssm_a_re, ssm_a_im, ssm_log_dt, ssm_b_re, ssm_b_im, ssm_c_re, ssm_c_im, ssm_d, ssm_w_glu, ssm_b_glu, w_out, g_ffn, peer_wq, peer_keys, peer_u, peer_v, g_final):
    bp, seq, dm = x_prompt.shape
    bd, tdec, _ = x_sample.shape
    depth = w_in.shape[0]
    assert bp == 1 and seq % FOX_TILE == 0 and seq % GLA_STEP == 0
    n_s = bd * tdec
    assert n_s == TOK_TILE and tdec == 8
    n_tok = seq + n_s
    n_pool, page = cache_fox_k.shape[1], cache_fox_k.shape[2]
    assert page == LANES and page_table.shape[1] % FOX_PAGES == 0

    n_pad = -n_tok % PEER_TOK
    pad_rows = lambda a: jnp.zeros((n_pad, a.shape[1]), a.dtype)
    x = jnp.concatenate([x_prompt.reshape(seq, dm), x_sample.reshape(n_s, dm), jnp.zeros((n_pad, dm), f32)], axis=0)
    ck = jnp.transpose(cache_fox_k, (0, 1, 3, 4, 2))
    cv = jnp.transpose(cache_fox_v, (0, 1, 3, 4, 2))
    clft = jnp.pad(jnp.swapaxes(cache_fox_logf, 2, 3), ((0, 0), (0, 0), (0, 8 - H_FOX), (0, 0)))

    e_mat = jnp.asarray(_head_match(D_GK, GLA_DK, D_GLA, GLA_DV), bf16)
    mask_t = jnp.asarray(_head_match(D_GLA, GLA_DV, D_GK, GLA_DK), f32)
    avg = jnp.asarray(_head_match(D_GLA, GLA_DV, D_GLA, GLA_DV), bf16)
    zeros_state = jnp.zeros((1, D_STATE), f32)

    outs_p, outs_s = [], []
    for l in range(depth):
        w_p = _pack_w_in(w_in[l])
        bfp = jnp.pad(fox_bf[l], (0, LANES - H_FOX)).reshape(1, LANES)
        wa2p = jnp.pad(gla_wa2[l], ((TAIL_GA, LANES - TAIL_GA - GLA_RANK), (0, 0))).astype(bf16)
        ba = gla_ba[l].reshape(1, D_GK)
        a_re = ssm_a_re[l].reshape(1, D_STATE)
        a_im = ssm_a_im[l].reshape(1, D_STATE)
        ldt = jnp.repeat(ssm_log_dt[l], SSM_P).reshape(1, D_STATE)
        bt_re = jnp.transpose(ssm_b_re[l], (2, 0, 1)).reshape(SSM_GC, D_STATE)
        bt_im = jnp.transpose(ssm_b_im[l], (2, 0, 1)).reshape(SSM_GC, D_STATE)
        p_re, p_im, bb_re, bb_im = _ssm_prep(a_re, a_im, ldt, bt_re, bt_im, tdec)
        to_blocks = lambda bb: jnp.transpose(bb.reshape(SSM_GC, SSM_G, SSM_P), (1, 0, 2))
        wb = jnp.concatenate([_block_diag(to_blocks(bb_re)), _block_diag(to_blocks(bb_im))], axis=1).astype(bf16)
        c_blocks = lambda cc: jnp.transpose(cc, (0, 2, 1))
        wc = jnp.concatenate([_block_diag(c_blocks(ssm_c_re[l])), -_block_diag(c_blocks(ssm_c_im[l]))],
                             axis=0).astype(bf16)
        dvec = ssm_d[l].reshape(1, D_SSM)
        wglu = ssm_w_glu[l].astype(bf16)
        bglu = ssm_b_glu[l].reshape(1, D_SSM)
        wqt = jnp.transpose(peer_wq[l].reshape(dm, PEER_HEADS * PEER_DK)).astype(bf16)
        keys = peer_keys[l].astype(bf16)
        u_tab = peer_u[l].astype(bf16)
        v_tab = peer_v[l].astype(bf16)

        (fq, qaug, kaug, fvb, fk, fv, lf, gq, gk, gv, la, gr, su) = _inproj(
            x, g_mix[l].reshape(1, dm), w_p, bfp, wa2p, ba)

        fox_p = _fox_prompt(qaug, kaug, fvb, seq)
        lfn_t = jnp.pad(jnp.swapaxes(lf[seq:n_tok, :8].reshape(bd, tdec, 8), 1, 2), ((0, 0), (0, 0), (0, LANES - tdec)))
        fox_s = _fox_sample(fq[seq:n_tok].astype(f32), fk[seq:n_tok], fv[seq:n_tok], lfn_t, ck, cv, clft, page_table, l, tdec)
        fox_o = jnp.concatenate([fox_p, fox_s, pad_rows(fox_s)], axis=0)

        gla_p, st_p = _gla(gq, gk, la, gv, jnp.zeros((1, D_GLA, D_GK), f32), e_mat, mask_t,
                           row0=0, nseq=1, steps=seq // GLA_STEP, rows_per_step=GLA_STEP, chunk=GLA_CHUNK)
        s0 = state_gla[l].astype(f32)
        s0t = jnp.transpose(s0, (0, 1, 3, 2))
        s0t = (s0t[:, :, :, None, :] * jnp.eye(H_GLA, dtype=f32)[None, :, None, :, None]).reshape(bd, D_GLA, D_GK)
        gla_s, st_s = _gla(gq, gk, la, gv, s0t, e_mat, mask_t,
                           row0=seq, nseq=bd, steps=1, rows_per_step=tdec, chunk=tdec)
        gla_o = jnp.concatenate([gla_p, gla_s, pad_rows(gla_s)], axis=0)

        def unpack_state(st):
            st = st.reshape(-1, H_GLA, GLA_DV, H_GLA, GLA_DK)
            st = jnp.stack([st[:, h, :, h, :] for h in range(H_GLA)], axis=1)
            return jnp.transpose(st, (0, 1, 3, 2))

        ssm_p, hre_p, him_p = _ssm(su, wb, p_re, p_im, zeros_state, zeros_state, wc, dvec, wglu, bglu,
                                   row0=0, n_rows=seq, tile_carry=True)
        h0r = jnp.repeat(state_ssm_re[l].reshape(bd, D_STATE).astype(f32), tdec, axis=0)
        h0i = jnp.repeat(state_ssm_im[l].reshape(bd, D_STATE).astype(f32), tdec, axis=0)
        ssm_s, hre_s, him_s = _ssm(su, wb, p_re, p_im, h0r, h0i, wc, dvec, wglu, bglu,
                                   row0=seq, n_rows=n_s, tile_carry=False)
        ssm_o = jnp.concatenate([ssm_p, ssm_s, pad_rows(ssm_s)], axis=0)

        x1, xf = _outproj(fox_o, gla_o, gr, ssm_o, x, fox_norm[l].reshape(1, D_FOX),
                          jnp.tile(gla_norm[l], H_GLA).reshape(1, D_GLA), avg,
                          w_out[l].astype(bf16), g_ffn[l].reshape(1, dm))
        i1, i2, gate = _peer_topk(xf, wqt, keys)
        x = _peer_dense(xf, i1, i2, gate, u_tab, v_tab, x1, g_final.reshape(1, dm), l == depth - 1)

        outs_p.append((fk[:seq].reshape(bp, seq, H_FOX, FOX_DH), fv[:seq].reshape(bp, seq, H_FOX, FOX_DH),
                       lf[:seq, :H_FOX].reshape(bp, seq, H_FOX), unpack_state(st_p),
                       hre_p.reshape(bp, SSM_G, SSM_P), him_p.reshape(bp, SSM_G, SSM_P)))
        outs_s.append((fk[seq:n_tok].reshape(bd, tdec, H_FOX, FOX_DH), fv[seq:n_tok].reshape(bd, tdec, H_FOX, FOX_DH),
                       lf[seq:n_tok, :H_FOX].reshape(bd, tdec, H_FOX), unpack_state(st_s),
                       hre_s[tdec - 1::tdec].reshape(bd, SSM_G, SSM_P), him_s[tdec - 1::tdec].reshape(bd, SSM_G, SSM_P)))

    p_stk = [jnp.stack(t) for t in zip(*outs_p)]
    s_stk = [jnp.stack(t) for t in zip(*outs_s)]
    y_prompt = x[:seq].reshape(bp, seq, dm)
    y_sample = x[seq:n_tok].reshape(bd, tdec, dm)
    return (y_prompt, y_sample, *p_stk, *s_stk)
```

```python
import functools

import numpy as np
import jax
import jax.numpy as jnp
from jax import lax
from jax.experimental import pallas as pl
from jax.experimental.pallas import tpu as pltpu

f32 = jnp.float32
bf16 = jnp.bfloat16
i32 = jnp.int32

EPS = 1e-6
LOG2E = 1.4426950408889634
LANES = 128
VMEM_LIMIT = 56 * 1024 * 1024

H_FOX, FOX_DH = 6, 64
D_FOX = H_FOX * FOX_DH
H_GLA, GLA_DK, GLA_DV = 6, 32, 64
D_GK = H_GLA * GLA_DK
D_GLA = H_GLA * GLA_DV
GLA_RANK = 16
GLA_TAU = 16.0
SSM_G, SSM_GC, SSM_P = 16, 16, 64
D_SSM = SSM_G * SSM_GC
D_STATE = SSM_G * SSM_P
PEER_KEYS, PEER_HEADS, PEER_TOPK, PEER_DK = 128, 8, 16, 256
PEER_SLOTS = PEER_HEADS * PEER_TOPK

C_FQ, C_FK, C_FV = 0, 384, 768
C_GQ, C_GK = 1152, 1408
C_GV, C_GR, C_SU, C_TAIL = 1664, 2048, 2432, 2688
D_INP = 2816
TAIL_FF, TAIL_GA = 0, 8

TOK_TILE = 256
FOX_TILE = 1024
FOX_PAGES = 16
GLA_STEP = 512
GLA_CHUNK = 16
PEER_TOK = 512
PEER_ETILE = 1024
GATE_GROUP = 16
GLA_UNROLL = 4


def _dot(a, b):
    return jnp.dot(a, b, preferred_element_type=f32)


def _dot_nt(a, b):
    return lax.dot_general(a, b, (((1,), (1,)), ((), ())), preferred_element_type=f32)


def _dot_tn(a, b):
    return lax.dot_general(a, b, (((0,), (0,)), ((), ())), preferred_element_type=f32)


def _split3(x):
    h = x.astype(bf16)
    r = x - h.astype(f32)
    m = r.astype(bf16)
    l = (r - m.astype(f32)).astype(bf16)
    return h, m, l


def _dot3_lhs01(sel, x):
    h, m, l = _split3(x)
    return _dot(sel, h) + _dot(sel, m) + _dot(sel, l)


def _dot3_rhs01(x, sel):
    h, m, l = _split3(x)
    return _dot(h, sel) + _dot(m, sel) + _dot(l, sel)


def _log_sigmoid(x):
    return -(jnp.maximum(-x, 0.0) + jnp.log1p(jnp.exp(-jnp.abs(x))))


def _iota(shape, dim):
    return lax.broadcasted_iota(i32, shape, dim)


def _div_pow2(x, n):
    assert n & (n - 1) == 0
    return jnp.right_shift(x, n.bit_length() - 1)


def _mod_pow2(x, n):
    assert n & (n - 1) == 0
    return jnp.bitwise_and(x, n - 1)


def _params(sem, vmem=None):
    return pltpu.CompilerParams(dimension_semantics=sem, vmem_limit_bytes=vmem)


def _fox_bias_lanes():
    place_q = np.zeros((3, LANES, H_FOX * LANES), np.float32)
    place_k = np.zeros((3, LANES, H_FOX * LANES), np.float32)
    ones_q = np.zeros((1, H_FOX * LANES), np.float32)
    ones_k = np.zeros((1, H_FOX * LANES), np.float32)
    for j in range(H_FOX):
        base = j * LANES + (FOX_DH if j % 2 == 0 else 0)
        for p in range(3):
            place_q[p, j, base + p] = 1.0
            ones_k[0, base + p] = 1.0
            ones_q[0, base + 3 + p] = 1.0
            place_k[p, j, base + 3 + p] = 1.0
    return place_q, place_k, ones_q, ones_k


def _inproj_body(x_ref, g_ref, w_ref, bfp_ref, wa2_ref, ba_ref, pq_ref, pk_ref, oq_ref, ok_ref,
                 fq_ref, qa_ref, ka_ref, fvb_ref, fk_ref, fv_ref, lf_ref,
                 gq_ref, gk_ref, gv_ref, la_ref, gr_ref, su_ref, carry_ref):
    @pl.when(pl.program_id(0) == 0)
    def _init():
        carry_ref[...] = jnp.zeros_like(carry_ref)

    x = x_ref[...]
    tt = x.shape[0]
    xn = x * lax.rsqrt(jnp.mean(x * x, axis=-1, keepdims=True) + EPS) * g_ref[...]
    z = _dot(xn.astype(bf16), w_ref[...])
    fq_ref[...] = (z[:, C_FQ:C_FQ + D_FOX] * (FOX_DH ** -0.5)).astype(bf16)
    fk = z[:, C_FK:C_FK + D_FOX]
    fv = z[:, C_FV:C_FV + D_FOX]
    fk_ref[...] = fk
    fv_ref[...] = fv
    fvb_ref[...] = fv.astype(bf16)
    gq_ref[...] = z[:, C_GQ:C_GQ + D_GK] * (GLA_DK ** -0.5)
    gk_ref[...] = z[:, C_GK:C_GK + D_GK]
    gv_ref[...] = z[:, C_GV:C_GV + D_GLA]
    gr_ref[...] = z[:, C_GR:C_GR + D_GLA]
    su_ref[...] = z[:, C_SU:C_SU + D_SSM]
    tail = z[:, C_TAIL:C_TAIL + LANES]
    lane = _iota((tt, LANES), 1)
    lf = jnp.where(lane < H_FOX, _log_sigmoid(tail + bfp_ref[...]), 0.0)
    lf_ref[...] = lf
    tri = jnp.where(_iota((tt, tt), 0) >= _iota((tt, tt), 1), 1.0, 0.0).astype(bf16)
    cs = _dot3_lhs01(tri, lf) + carry_ref[...]
    carry_ref[...] = cs[tt - 1:tt, :]
    pieces = _split3(cs * LOG2E)
    q_bias = sum(_dot(pc, pq_ref[i]) for i, pc in enumerate(pieces)) + oq_ref[...]
    k_bias = ok_ref[...] - sum(_dot(pc, pk_ref[i]) for i, pc in enumerate(pieces))
    low = lane < FOX_DH
    for hp in range(H_FOX // 2):
        qt = z[:, C_FQ + hp * LANES:C_FQ + (hp + 1) * LANES] * (LOG2E * FOX_DH ** -0.5)
        kt = z[:, C_FK + hp * LANES:C_FK + (hp + 1) * LANES]
        for hh in range(2):
            blk = slice((2 * hp + hh) * LANES, (2 * hp + hh + 1) * LANES)
            real = low if hh == 0 else jnp.logical_not(low)
            qa_ref[:, blk] = jnp.where(real, qt, q_bias[:, blk]).astype(bf16)
            ka_ref[:, blk] = jnp.where(real, kt, k_bias[:, blk]).astype(bf16)
    za = _dot(tail.astype(bf16), wa2_ref[...]) + ba_ref[...]
    la_ref[...] = _log_sigmoid(za) * (1.0 / GLA_TAU)


def _inproj(x, g_mix, w_p, bfp, wa2p, ba):
    t = x.shape[0]
    tt = TOK_TILE
    row = lambda n: pl.BlockSpec((tt, n), lambda i: (i, 0))
    const = lambda a: pl.BlockSpec(a.shape, lambda i: (0,) * a.ndim)
    outs = [
        (D_FOX, bf16), (H_FOX * LANES, bf16), (H_FOX * LANES, bf16), (D_FOX, bf16), (D_FOX, f32), (D_FOX, f32),
        (LANES, f32),
        (D_GK, f32), (D_GK, f32), (D_GLA, f32), (D_GK, f32), (D_GLA, f32), (D_SSM, f32),
    ]
    place_q, place_k, ones_q, ones_k = _fox_bias_lanes()
    consts = [g_mix, w_p, bfp, wa2p, ba, jnp.asarray(place_q, bf16), jnp.asarray(place_k, bf16),
              jnp.asarray(ones_q), jnp.asarray(ones_k)]
    return pl.pallas_call(
        _inproj_body,
        grid=(t // tt,),
        in_specs=[row(x.shape[1])] + [const(a) for a in consts],
        out_specs=[row(n) for n, _ in outs],
        out_shape=[jax.ShapeDtypeStruct((t, n), d) for n, d in outs],
        scratch_shapes=[pltpu.VMEM((1, LANES), f32)],
        compiler_params=_params(("arbitrary",), VMEM_LIMIT),
        name="inproj",
    )(x, *consts)


def _fox_prompt_body(qi_ref, ki_ref, qa_ref, qb_ref, ka_ref, kb_ref, v_ref, o_ref,
                     m_ref, l_ref, acc_ref):
    p = pl.program_id(1)
    qi = qi_ref[p]
    ki = ki_ref[p]
    tq, tk = qa_ref.shape[0], ka_ref.shape[0]

    @pl.when(ki == 0)
    def _init():
        m_ref[...] = jnp.full_like(m_ref, -jnp.inf)
        l_ref[...] = jnp.zeros_like(l_ref)
        acc_ref[...] = jnp.zeros_like(acc_ref)

    def step(diagonal):
        vext = jnp.concatenate([v_ref[...], jnp.ones((tk, LANES), bf16)], axis=1)
        for hh, (q_ref, k_ref) in enumerate(((qa_ref, ka_ref), (qb_ref, kb_ref))):
            s = _dot_nt(q_ref[...], k_ref[...])
            if diagonal:
                s = jnp.where(_iota((tq, tk), 0) >= _iota((tq, tk), 1), s, -jnp.inf)
            m_old = m_ref[hh]
            m_new = jnp.maximum(m_old, jnp.max(s, axis=-1, keepdims=True))
            pr = jnp.exp2(s - jnp.tile(m_new, (1, tk // LANES)))
            alpha = jnp.exp2(m_old - m_new)
            pv = _dot(pr.astype(bf16), vext)
            l_ref[hh] = alpha * l_ref[hh] + pv[:, LANES:]
            m_ref[hh] = m_new
            acc_ref[hh] = alpha * acc_ref[hh] + pv[:, :LANES]

    @pl.when(ki < qi)
    def _off():
        step(False)

    @pl.when(ki == qi)
    def _diag():
        step(True)
        lane = _iota((1, LANES), 1)
        o_ref[...] = jnp.where(lane < FOX_DH, acc_ref[0] / l_ref[0], acc_ref[1] / l_ref[1])


def _fox_prompt(qaug, kaug, fvb, seq):
    tq = FOX_TILE
    nq = seq // tq
    qi_tbl = np.concatenate([np.full(i + 1, i) for i in range(nq)]).astype(np.int32)
    ki_tbl = np.concatenate([np.arange(i + 1) for i in range(nq)]).astype(np.int32)
    grid_spec = pltpu.PrefetchScalarGridSpec(
        num_scalar_prefetch=2,
        grid=(H_FOX // 2, len(qi_tbl)),
        in_specs=[
            pl.BlockSpec((tq, LANES), lambda h, p, qi, ki: (qi[p], 2 * h)),
            pl.BlockSpec((tq, LANES), lambda h, p, qi, ki: (qi[p], 2 * h + 1)),
            pl.BlockSpec((tq, LANES), lambda h, p, qi, ki: (ki[p], 2 * h)),
            pl.BlockSpec((tq, LANES), lambda h, p, qi, ki: (ki[p], 2 * h + 1)),
            pl.BlockSpec((tq, LANES), lambda h, p, qi, ki: (ki[p], h)),
        ],
        out_specs=pl.BlockSpec((tq, LANES), lambda h, p, qi, ki: (qi[p], h)),
        scratch_shapes=[pltpu.VMEM((2, tq, LANES), f32), pltpu.VMEM((2, tq, LANES), f32),
                        pltpu.VMEM((2, tq, LANES), f32)],
    )
    return pl.pallas_call(
        _fox_prompt_body,
        grid_spec=grid_spec,
        out_shape=jax.ShapeDtypeStruct((seq, D_FOX), f32),
        compiler_params=_params(("arbitrary", "arbitrary"), VMEM_LIMIT),
        name="fox_prompt",
    )(jnp.asarray(qi_tbl), jnp.asarray(ki_tbl), qaug, qaug, kaug, kaug, fvb)


def _expand_heads(x, rows):
    return jnp.concatenate(
        [jnp.broadcast_to(x[h:h + 1, :], (rows, x.shape[1])) for h in range(H_FOX)], axis=0)


def _fox_sample_body(pt_ref, q_ref, kn_ref, vn_ref, lfn_ref, *rest):
    del pt_ref
    npg = FOX_PAGES
    kp_refs = rest[0:npg]
    vp_refs = rest[npg:2 * npg]
    lfp_refs = rest[2 * npg:3 * npg]
    o_ref = rest[3 * npg]
    m_ref, l_ref, acc_ref, car_ref, qbd_ref, cq_ref = rest[3 * npg + 1:]
    j = pl.program_id(1)
    t = q_ref.shape[0]
    hq = H_FOX * t
    page = kp_refs[0].shape[-1]

    def update(s, weigh):
        m_old = m_ref[...]
        m_new = jnp.maximum(m_old, jnp.max(s, axis=-1, keepdims=True))
        pr = jnp.exp(s - m_new)
        alpha = jnp.exp(m_old - m_new)
        l_ref[...] = alpha * l_ref[...] + jnp.sum(pr, axis=-1, keepdims=True)
        m_ref[...] = m_new
        acc_ref[...] = alpha * acc_ref[...] + weigh(pr.astype(bf16))

    @pl.when(j == 0)
    def _new_tokens():
        m_ref[...] = jnp.full_like(m_ref, -jnp.inf)
        l_ref[...] = jnp.zeros_like(l_ref)
        acc_ref[...] = jnp.zeros_like(acc_ref)
        car_ref[...] = jnp.zeros_like(car_ref)
        q = q_ref[...]
        q6 = jnp.concatenate([q] * H_FOX, axis=0)
        keep = _div_pow2(_iota((hq, D_FOX), 0), t) == _div_pow2(_iota((hq, D_FOX), 1), FOX_DH)
        qbd_ref[...] = jnp.where(keep, q6, 0.0).astype(bf16)
        cum = lfn_ref[...]
        lane8 = _iota(cum.shape, 1)
        sh = 1
        while sh < t:
            cum = cum + jnp.where(lane8 >= sh, pltpu.roll(cum, sh, 1), 0.0)
            sh *= 2
        cum_rows = _expand_heads(cum, t)
        qpos = _mod_pow2(_iota((hq, LANES), 0), t)
        kpos = _iota((hq, LANES), 1)
        cq = jnp.sum(jnp.where(kpos == qpos, cum_rows, 0.0), axis=-1, keepdims=True)
        cq_ref[...] = cq
        zpad = jnp.zeros((LANES - t, D_FOX), f32)
        kn = jnp.concatenate([kn_ref[...], zpad], axis=0).astype(bf16)
        vn = jnp.concatenate([vn_ref[...], zpad], axis=0).astype(bf16)
        s = _dot_nt(qbd_ref[...], kn) + cq - cum_rows
        s = jnp.where(kpos <= qpos, s, -jnp.inf)
        update(s, lambda pr: _dot(pr, vn))

    upper = jnp.where(_iota((page, page), 0) > _iota((page, page), 1), 1.0, 0.0).astype(bf16)
    qbd = qbd_ref[...]
    car = car_ref[...]
    scores = []
    for g in range(npg):
        lfp = lfp_refs[g][...]
        suffix = _dot3_rhs01(lfp, upper) + car
        car = car + jnp.sum(lfp, axis=-1, keepdims=True)
        kpt = kp_refs[g][...].reshape(D_FOX, page).astype(bf16)
        scores.append(_dot(qbd, kpt) + _expand_heads(suffix, t))
    car_ref[...] = car

    def weigh_pages(pr):
        out = None
        for g in range(npg):
            vpt = vp_refs[g][...].reshape(D_FOX, page).astype(bf16)
            part = _dot_nt(pr[:, g * page:(g + 1) * page], vpt)
            out = part if out is None else out + part
        return out

    update(jnp.concatenate(scores, axis=1) + cq_ref[...], weigh_pages)

    @pl.when(j == pl.num_programs(1) - 1)
    def _finish():
        o48 = acc_ref[...] / l_ref[...]
        head = _div_pow2(_iota((t, D_FOX), 1), FOX_DH)
        out = jnp.zeros((t, D_FOX), f32)
        for h in range(H_FOX):
            out = out + jnp.where(head == h, o48[h * t:(h + 1) * t, :], 0.0)
        o_ref[...] = out


def _fox_sample(q_s, k_s, v_s, lfn_t, cache_k, cache_v, cache_lft, page_table, layer, t):
    nb, n_pages = page_table.shape
    page = cache_k.shape[-1]
    npg = FOX_PAGES
    hq = H_FOX * t
    seq_spec = pl.BlockSpec((t, D_FOX), lambda b, j, pt: (b, 0))

    def page_spec(shape, g):
        zeros = (0,) * len(shape)
        return pl.BlockSpec((None, None) + shape,
                            lambda b, j, pt: (layer, pt[b, n_pages - 1 - (j * npg + g)]) + zeros)

    grid_spec = pltpu.PrefetchScalarGridSpec(
        num_scalar_prefetch=1,
        grid=(nb, n_pages // npg),
        in_specs=[seq_spec, seq_spec, seq_spec,
                  pl.BlockSpec((None, 8, LANES), lambda b, j, pt: (b, 0, 0))]
        + [page_spec((H_FOX, FOX_DH, page), g) for g in range(npg)]
        + [page_spec((H_FOX, FOX_DH, page), g) for g in range(npg)]
        + [page_spec((8, page), g) for g in range(npg)],
        out_specs=seq_spec,
        scratch_shapes=[pltpu.VMEM((hq, 1), f32), pltpu.VMEM((hq, 1), f32),
                        pltpu.VMEM((hq, D_FOX), f32), pltpu.VMEM((8, 1), f32),
                        pltpu.VMEM((hq, D_FOX), bf16), pltpu.VMEM((hq, 1), f32)],
    )
    return pl.pallas_call(
        _fox_sample_body,
        grid_spec=grid_spec,
        out_shape=jax.ShapeDtypeStruct((nb * t, D_FOX), f32),
        compiler_params=_params(("arbitrary", "arbitrary"), VMEM_LIMIT),
        name="fox_sample",
    )(page_table, q_s, k_s, v_s, lfn_t, *([cache_k] * npg), *([cache_v] * npg), *([cache_lft] * npg))


def _gla_body(q_ref, k_ref, la_ref, v_ref, s0_ref, e_ref, mask_ref, o_ref, sout_ref, st_ref, *, chunk):
    i = pl.program_id(1)

    @pl.when(i == 0)
    def _init():
        st_ref[...] = s0_ref[...]

    c = chunk
    n_chunks = q_ref.shape[0] // c
    row = _iota((c, D_GK), 0)

    def one_chunk(ci, carry):
        r0 = pl.multiple_of(ci * c, c)
        q = q_ref[pl.ds(r0, c), :]
        k = k_ref[pl.ds(r0, c), :]
        v = v_ref[pl.ds(r0, c), :]
        b = la_ref[pl.ds(r0, c), :]
        sh = 1
        while sh < c:
            b = b + jnp.where(row >= sh, pltpu.roll(b, sh, 0), 0.0)
            sh *= 2
        st = st_ref[...]
        o = _dot_nt((q * jnp.exp(b)).astype(bf16), st.astype(bf16))
        pieces = []
        for s in range(c):
            d = jnp.where(row >= s, b - b[s:s + 1, :], -jnp.inf)
            pieces.append(q * k[s:s + 1, :] * jnp.exp(d))
        pst = jnp.concatenate(pieces, axis=0)
        hi = pst.astype(bf16)
        lo = (pst - hi.astype(f32)).astype(bf16)
        att = _dot(hi, e_ref[...]) + _dot(lo, e_ref[...])
        for s in range(c):
            o = o + att[s * c:(s + 1) * c, :] * v[s:s + 1, :]
        o_ref[pl.ds(r0, c), :] = o
        b_last = b[c - 1:c, :]
        kd = (k * jnp.exp(b_last - b)).astype(bf16)
        st_ref[...] = st * jnp.exp(b_last) + _dot_tn(v.astype(bf16), kd) * mask_ref[...]
        return carry

    lax.fori_loop(0, n_chunks, one_chunk, 0, unroll=min(GLA_UNROLL, n_chunks))

    @pl.when(i == pl.num_programs(1) - 1)
    def _finish():
        sout_ref[...] = st_ref[...]


def _gla(gq, gk, la, gv, s0t, e_mat, mask_t, *, row0, nseq, steps, rows_per_step, chunk):
    blk0 = row0 // rows_per_step
    rows = lambda n: pl.BlockSpec((rows_per_step, n), lambda s, i: (blk0 + s * steps + i, 0))
    orow = pl.BlockSpec((rows_per_step, D_GLA), lambda s, i: (s * steps + i, 0))
    st_spec = pl.BlockSpec((None, D_GLA, D_GK), lambda s, i: (s, 0, 0))
    const = lambda a: pl.BlockSpec(a.shape, lambda s, i: (0,) * a.ndim)
    return pl.pallas_call(
        functools.partial(_gla_body, chunk=chunk),
        grid=(nseq, steps),
        in_specs=[rows(D_GK), rows(D_GK), rows(D_GK), rows(D_GLA), st_spec, const(e_mat), const(mask_t)],
        out_specs=[orow, st_spec],
        out_shape=[jax.ShapeDtypeStruct((nseq * steps * rows_per_step, D_GLA), f32),
                   jax.ShapeDtypeStruct((nseq, D_GLA, D_GK), f32)],
        scratch_shapes=[pltpu.VMEM((D_GLA, D_GK), f32)],
        compiler_params=_params(("arbitrary", "arbitrary"), VMEM_LIMIT),
        name="gla",
    )(gq, gk, la, gv, s0t, e_mat, mask_t)


def _ssm_prep_body(are_ref, aim_ref, ldt_ref, bre_ref, bim_ref,
                   pre_ref, pim_ref, bbre_ref, bbim_ref):
    a_re = are_ref[...]
    a_im = aim_ref[...]
    dt = jnp.exp(ldt_ref[...])
    mag = jnp.exp(a_re * dt)
    abar_re = mag * jnp.cos(a_im * dt)
    abar_im = mag * jnp.sin(a_im * dt)
    den = a_re * a_re + a_im * a_im
    nr, ni = abar_re - 1.0, abar_im
    fr = (nr * a_re + ni * a_im) / den
    fi = (ni * a_re - nr * a_im) / den
    bbre_ref[...] = fr * bre_ref[...] - fi * bim_ref[...]
    bbim_ref[...] = fr * bim_ref[...] + fi * bre_ref[...]
    n = pre_ref.shape[0]
    pre_ref[0:1, :] = abar_re
    pim_ref[0:1, :] = abar_im
    m = 1
    while m < n:
        tr = pre_ref[m - 1:m, :]
        ti = pim_ref[m - 1:m, :]
        xr = pre_ref[0:m, :]
        xi = pim_ref[0:m, :]
        pre_ref[m:2 * m, :] = xr * tr - xi * ti
        pim_ref[m:2 * m, :] = xr * ti + xi * tr
        m *= 2


def _ssm_prep(a_re, a_im, log_dt, bt_re, bt_im, n_pow):
    outs = [(n_pow, D_STATE), (n_pow, D_STATE), (SSM_GC, D_STATE), (SSM_GC, D_STATE)]
    return pl.pallas_call(
        _ssm_prep_body,
        out_shape=[jax.ShapeDtypeStruct(s, f32) for s in outs],
        name="ssm_prep",
    )(a_re, a_im, log_dt, bt_re, bt_im)


def _ssm_body(u_ref, wb_ref, pre_ref, pim_ref, hin_re_ref, hin_im_ref, wc_ref, d_ref, wg_ref, bg_ref,
              o_ref, hre_ref, him_ref, car_re_ref, car_im_ref, *, tile_carry):
    i = pl.program_id(0)
    u = u_ref[...]
    tt = u.shape[0]
    grp = pre_ref.shape[0]
    d = _dot(u.astype(bf16), wb_ref[...])
    xr = d[:, :D_STATE]
    xi = d[:, D_STATE:]
    pos = _mod_pow2(_iota((tt, D_STATE), 0), grp)
    sh = 1
    while sh < grp:
        ar = pre_ref[sh - 1:sh, :]
        ai = pim_ref[sh - 1:sh, :]
        sr = jnp.where(pos >= sh, pltpu.roll(xr, sh, 0), 0.0)
        si = jnp.where(pos >= sh, pltpu.roll(xi, sh, 0), 0.0)
        xr, xi = xr + ar * sr - ai * si, xi + ar * si + ai * sr
        sh *= 2
    pr = pre_ref[...]
    pi = pim_ref[...]
    if tile_carry:
        @pl.when(i == 0)
        def _init():
            car_re_ref[...] = hin_re_ref[...]
            car_im_ref[...] = hin_im_ref[...]
        cr = car_re_ref[...]
        ci = car_im_ref[...]
        hr_groups, hi_groups = [], []
        for g in range(tt // grp):
            gr = xr[g * grp:(g + 1) * grp, :] + pr * cr - pi * ci
            gi = xi[g * grp:(g + 1) * grp, :] + pr * ci + pi * cr
            cr, ci = gr[grp - 1:grp, :], gi[grp - 1:grp, :]
            hr_groups.append(gr)
            hi_groups.append(gi)
        hr = jnp.concatenate(hr_groups, axis=0)
        hi = jnp.concatenate(hi_groups, axis=0)
        car_re_ref[...] = cr
        car_im_ref[...] = ci
        hre_ref[...] = cr
        him_ref[...] = ci
    else:
        h0r = hin_re_ref[...]
        h0i = hin_im_ref[...]
        prt = jnp.tile(pr, (tt // grp, 1))
        pit = jnp.tile(pi, (tt // grp, 1))
        hr = xr + prt * h0r - pit * h0i
        hi = xi + prt * h0i + pit * h0r
        hre_ref[...] = hr
        him_ref[...] = hi
    hcat = jnp.concatenate([hr, hi], axis=1).astype(bf16)
    y = _dot(hcat, wc_ref[...]) + d_ref[...] * u
    zg = jax.nn.gelu(y)
    o_ref[...] = (zg * jax.nn.sigmoid(_dot(zg.astype(bf16), wg_ref[...]) + bg_ref[...])).astype(bf16)


def _ssm(su, wb, p_re, p_im, hin_re, hin_im, wc, dvec, wglu, bglu, *, row0, n_rows, tile_carry):
    tt = TOK_TILE
    blk0 = row0 // tt
    const = lambda a: pl.BlockSpec(a.shape, lambda i: (0,) * a.ndim)
    if tile_carry:
        h_spec = const(hin_re)
        hout_spec = pl.BlockSpec((1, D_STATE), lambda i: (0, 0))
        hout_shape = jax.ShapeDtypeStruct((1, D_STATE), f32)
    else:
        h_spec = pl.BlockSpec((tt, D_STATE), lambda i: (i, 0))
        hout_spec = h_spec
        hout_shape = jax.ShapeDtypeStruct((n_rows, D_STATE), f32)
    return pl.pallas_call(
        functools.partial(_ssm_body, tile_carry=tile_carry),
        grid=(n_rows // tt,),
        in_specs=[pl.BlockSpec((tt, D_SSM), lambda i: (blk0 + i, 0)), const(wb), const(p_re), const(p_im),
                  h_spec, h_spec, const(wc), const(dvec), const(wglu), const(bglu)],
        out_specs=[pl.BlockSpec((tt, D_SSM), lambda i: (i, 0)), hout_spec, hout_spec],
        out_shape=[jax.ShapeDtypeStruct((n_rows, D_SSM), bf16), hout_shape, hout_shape],
        scratch_shapes=[pltpu.VMEM((1, D_STATE), f32), pltpu.VMEM((1, D_STATE), f32)],
        compiler_params=_params(("arbitrary",), VMEM_LIMIT),
        name="ssm",
    )(su, wb, p_re, p_im, hin_re, hin_im, wc, dvec, wglu, bglu)


def _outproj_body(fox_ref, gla_ref, gr_ref, ssm_ref, x_ref, fn_ref, gn_ref, avg_ref, w_ref, gf_ref,
                  x1_ref, xf_ref):
    fo = fox_ref[...]
    fo = fo * lax.rsqrt(jnp.mean(fo * fo, axis=-1, keepdims=True) + EPS) * fn_ref[...]
    go = gla_ref[...]
    ms = _dot3_rhs01(go * go, avg_ref[...]) * (1.0 / GLA_DV)
    go = go * lax.rsqrt(ms + EPS) * gn_ref[...] * jax.nn.silu(gr_ref[...])
    w = w_ref
    y = (_dot(fo.astype(bf16), w[0:D_FOX, :]) + _dot(go.astype(bf16), w[D_FOX:D_FOX + D_GLA, :])
         + _dot(ssm_ref[...], w[D_FOX + D_GLA:, :]))
    x1 = x_ref[...] + y
    x1_ref[...] = x1
    xf = x1 * lax.rsqrt(jnp.mean(x1 * x1, axis=-1, keepdims=True) + EPS) * gf_ref[...]
    xf_ref[...] = xf.astype(bf16)


def _outproj(fox_o, gla_o, gr, ssm_o, x, fox_norm, gla_norm6, avg, w_out, g_ffn):
    t, dm = x.shape
    tt = TOK_TILE
    row = lambda n: pl.BlockSpec((tt, n), lambda i: (i, 0))
    const = lambda a: pl.BlockSpec(a.shape, lambda i: (0,) * a.ndim)
    return pl.pallas_call(
        _outproj_body,
        grid=(t // tt,),
        in_specs=[row(D_FOX), row(D_GLA), row(D_GLA), row(D_SSM), row(dm),
                  const(fox_norm), const(gla_norm6), const(avg), const(w_out), const(g_ffn)],
        out_specs=[row(dm), row(dm)],
        out_shape=[jax.ShapeDtypeStruct((t, dm), f32), jax.ShapeDtypeStruct((t, dm), bf16)],
        compiler_params=_params(("arbitrary",), VMEM_LIMIT),
        name="outproj",
    )(fox_o, gla_o, gr, ssm_o, x, fox_norm, gla_norm6, avg, w_out, g_ffn)


def _topk_rows(s, k, ids=None):
    it = _iota(s.shape, 0) if ids is None else ids
    big = jnp.iinfo(jnp.int32).max
    vals, idxs = [], []
    for _ in range(k):
        m = jnp.max(s, axis=0, keepdims=True)
        idx = jnp.min(jnp.where(s == m, it, big), axis=0, keepdims=True)
        vals.append(m)
        idxs.append(idx)
        s = jnp.where(it == idx, -jnp.inf, s)
    return jnp.concatenate(vals, axis=0), jnp.concatenate(idxs, axis=0)


def _pair_candidates(v1, v2, tt):
    kk = v1.shape[0]
    blocks, ids = [], []
    for a in range(kk):
        nb = kk // (a + 1)
        rows = -(-nb // 8) * 8
        blk = v1[a:a + 1, :] + v2[0:rows, :]
        b_id = _iota((rows, tt), 0)
        if nb < rows:
            blk = jnp.where(b_id < nb, blk, -jnp.inf)
        blocks.append(blk)
        ids.append(b_id + a * kk)
    return jnp.concatenate(blocks, axis=0), jnp.concatenate(ids, axis=0)


def _take_rows(table, sel):
    return jnp.sum(jnp.where(_iota(table.shape, 0) == sel, table, 0), axis=0, keepdims=True)


def _peer_select(qt, keys1, keys2):
    kk = PEER_TOPK
    half = PEER_DK // 2
    v1, i1 = _topk_rows(_dot(keys1, qt[:half, :].astype(bf16)), kk)
    v2, i2 = _topk_rows(_dot(keys2, qt[half:, :].astype(bf16)), kk)
    cand, cand_id = _pair_candidates(v1, v2, qt.shape[1])
    sc, pos = _topk_rows(cand, kk, cand_id)
    sel1 = jnp.concatenate([_take_rows(i1, _div_pow2(pos[j:j + 1, :], kk)) for j in range(kk)], axis=0)
    sel2 = jnp.concatenate([_take_rows(i2, _mod_pow2(pos[j:j + 1, :], kk)) for j in range(kk)], axis=0)
    ex = jnp.exp(sc - sc[0:1, :])
    return sel1, sel2, ex / jnp.sum(ex, axis=0, keepdims=True)


def _peer_topk_body(xf_ref, wqt_ref, keys_ref, i1_ref, i2_ref, g_ref):
    qt = _dot_nt(wqt_ref[...], xf_ref[...])
    picks = [_peer_select(qt[h * PEER_DK:(h + 1) * PEER_DK, :], keys_ref[h, 0], keys_ref[h, 1])
             for h in range(PEER_HEADS)]
    i1_ref[...] = jnp.concatenate([p[0] for p in picks], axis=0).T
    i2_ref[...] = jnp.concatenate([p[1] for p in picks], axis=0).T
    g_ref[...] = jnp.concatenate([p[2] for p in picks], axis=0).T


def _peer_topk(xf, wqt, keys):
    t, dm = xf.shape
    tt = TOK_TILE
    const = lambda a: pl.BlockSpec(a.shape, lambda i: (0,) * a.ndim)
    slot = pl.BlockSpec((tt, PEER_SLOTS), lambda i: (i, 0))
    return pl.pallas_call(
        _peer_topk_body,
        grid=(t // tt,),
        in_specs=[pl.BlockSpec((tt, dm), lambda i: (i, 0)), const(wqt), const(keys)],
        out_specs=[slot, slot, slot],
        out_shape=[jax.ShapeDtypeStruct((t, PEER_SLOTS), i32), jax.ShapeDtypeStruct((t, PEER_SLOTS), i32),
                   jax.ShapeDtypeStruct((t, PEER_SLOTS), f32)],
        compiler_params=_params(("arbitrary",), VMEM_LIMIT),
        name="peer_topk",
    )(xf, wqt, keys)


def _peer_dense_body(xf_ref, i1_ref, i2_ref, g_ref, u_ref, v_ref, x1_ref, gfin_ref, o_ref,
                     gate_ref, acc_ref, *, final_norm):
    e = pl.program_id(1)
    tt = xf_ref.shape[0]
    nk = PEER_KEYS
    rows_per_tile = u_ref.shape[0] // nk

    @pl.when(e == 0)
    def _build_gates():
        acc_ref[...] = jnp.zeros_like(acc_ref)
        key_id = _iota((nk, PEER_SLOTS), 0)

        def token_group(gi, carry):
            t0 = pl.multiple_of(gi * GATE_GROUP, GATE_GROUP)
            r1 = i1_ref[pl.ds(t0, GATE_GROUP), :]
            r2 = i2_ref[pl.ds(t0, GATE_GROUP), :]
            gr = g_ref[pl.ds(t0, GATE_GROUP), :]
            mats = []
            for j in range(GATE_GROUP):
                p1 = jnp.where(r1[j:j + 1, :] == key_id, gr[j:j + 1, :], 0.0).astype(bf16)
                p2 = jnp.where(r2[j:j + 1, :] == key_id, 1.0, 0.0).astype(bf16)
                mats.append(_dot_nt(p1, p2))
            gate_ref[:, pl.ds(t0, GATE_GROUP), :] = jnp.swapaxes(jnp.stack(mats, axis=0), 0, 1).astype(bf16)
            return carry

        lax.fori_loop(0, tt // GATE_GROUP, token_group, 0)

    a = _dot_nt(xf_ref[...], u_ref[...])
    act = jax.nn.gelu(a)
    ws = []
    for r in range(rows_per_tile):
        gt = gate_ref[e * rows_per_tile + r]
        ws.append((gt.astype(f32) * act[:, r * nk:(r + 1) * nk]).astype(bf16))
    acc_ref[...] += _dot(jnp.concatenate(ws, axis=1), v_ref[...])

    @pl.when(e == pl.num_programs(1) - 1)
    def _finish():
        y = x1_ref[...] + acc_ref[...]
        if final_norm:
            y = y * lax.rsqrt(jnp.mean(y * y, axis=-1, keepdims=True) + EPS) * gfin_ref[...]
        o_ref[...] = y


def _peer_dense(xf, i1, i2, g, u_tab, v_tab, x1, g_final, final_norm):
    t, dm = xf.shape
    tt = PEER_TOK
    et = PEER_ETILE
    tok = lambda n: pl.BlockSpec((tt, n), lambda i, e: (i, 0))
    tab = pl.BlockSpec((et, dm), lambda i, e: (e, 0))
    return pl.pallas_call(
        functools.partial(_peer_dense_body, final_norm=final_norm),
        grid=(t // tt, u_tab.shape[0] // et),
        in_specs=[tok(dm), tok(PEER_SLOTS), tok(PEER_SLOTS), tok(PEER_SLOTS), tab, tab, tok(dm),
                  pl.BlockSpec(g_final.shape, lambda i, e: (0, 0))],
        out_specs=tok(dm),
        out_shape=jax.ShapeDtypeStruct((t, dm), f32),
        scratch_shapes=[pltpu.VMEM((PEER_KEYS, tt, PEER_KEYS), bf16), pltpu.VMEM((tt, dm), f32)],
        compiler_params=_params(("arbitrary", "arbitrary"), VMEM_LIMIT),
        name="peer_dense",
    )(xf, i1, i2, g, u_tab, v_tab, x1, g_final)


def _pack_w_in(w_in):
    dm = w_in.shape[0]
    sizes = (D_FOX, D_FOX, D_FOX, H_FOX, D_GK, D_GK, D_GLA, GLA_RANK, D_GLA, D_SSM)
    pts = np.cumsum(sizes)[:-1].tolist()
    fq, fk, fv, ff, gq, gk, gv, ga, gr, su = jnp.split(w_in, pts, axis=1)
    z = lambda n: jnp.zeros((dm, n), w_in.dtype)
    tail = jnp.concatenate([ff, z(TAIL_GA - H_FOX), ga, z(LANES - TAIL_GA - GLA_RANK)], axis=1)
    packed = jnp.concatenate([fq, fk, fv, gq, z(C_GK - C_GQ - D_GK), gk, z(C_GV - C_GK - D_GK),
                              gv, gr, su, tail], axis=1)
    assert packed.shape[1] == D_INP
    return packed.astype(bf16)


def _block_diag(blocks):
    g, a, b = blocks.shape
    eye = jnp.eye(g, dtype=blocks.dtype)
    return (blocks[:, :, None, :] * eye[:, None, :, None]).reshape(g * a, g * b)


def _head_match(rows, row_w, cols, col_w):
    return (np.arange(rows)[:, None] // row_w == np.arange(cols)[None, :] // col_w)


def kernel(x_prompt, x_sample, cache_fox_k, cache_fox_v, cache_fox_logf, state_gla, state_ssm_re, state_ssm_im, page_table, g_mix, w_in, fox_bf, fox_norm, gla_wa2, gla_ba, gla_norm, ssm_a_re, ssm_a_im, ssm_log_dt, ssm_b_re, ssm_b_im, ssm_c_re, ssm_c_im, ssm_d, ssm_w_glu, ssm_b_glu, w_out, g_ffn, peer_wq, peer_keys, peer_u, peer_v, g_final):
    bp, seq, dm = x_prompt.shape
    bd, tdec, _ = x_sample.shape
    depth = w_in.shape[0]
    assert bp == 1 and seq % FOX_TILE == 0 and seq % GLA_STEP == 0
    n_s = bd * tdec
    assert n_s == TOK_TILE and tdec == 8
    n_tok = seq + n_s
    n_pool, page = cache_fox_k.shape[1], cache_fox_k.shape[2]
    assert page == LANES and page_table.shape[1] % FOX_PAGES == 0

    n_pad = -n_tok % PEER_TOK
    pad_rows = lambda a: jnp.zeros((n_pad, a.shape[1]), a.dtype)
    x = jnp.concatenate([x_prompt.reshape(seq, dm), x_sample.reshape(n_s, dm), jnp.zeros((n_pad, dm), f32)], axis=0)
    ck = jnp.transpose(cache_fox_k, (0, 1, 3, 4, 2))
    cv = jnp.transpose(cache_fox_v, (0, 1, 3, 4, 2))
    clft = jnp.pad(jnp.swapaxes(cache_fox_logf, 2, 3), ((0, 0), (0, 0), (0, 8 - H_FOX), (0, 0)))

    e_mat = jnp.asarray(_head_match(D_GK, GLA_DK, D_GLA, GLA_DV), bf16)
    mask_t = jnp.asarray(_head_match(D_GLA, GLA_DV, D_GK, GLA_DK), f32)
    avg = jnp.asarray(_head_match(D_GLA, GLA_DV, D_GLA, GLA_DV), bf16)
    zeros_state = jnp.zeros((1, D_STATE), f32)

    outs_p, outs_s = [], []
    for l in range(depth):
        w_p = _pack_w_in(w_in[l])
        bfp = jnp.pad(fox_bf[l], (0, LANES - H_FOX)).reshape(1, LANES)
        wa2p = jnp.pad(gla_wa2[l], ((TAIL_GA, LANES - TAIL_GA - GLA_RANK), (0, 0))).astype(bf16)
        ba = gla_ba[l].reshape(1, D_GK)
        a_re = ssm_a_re[l].reshape(1, D_STATE)
        a_im = ssm_a_im[l].reshape(1, D_STATE)
        ldt = jnp.repeat(ssm_log_dt[l], SSM_P).reshape(1, D_STATE)
        bt_re = jnp.transpose(ssm_b_re[l], (2, 0, 1)).reshape(SSM_GC, D_STATE)
        bt_im = jnp.transpose(ssm_b_im[l], (2, 0, 1)).reshape(SSM_GC, D_STATE)
        p_re, p_im, bb_re, bb_im = _ssm_prep(a_re, a_im, ldt, bt_re, bt_im, tdec)
        to_blocks = lambda bb: jnp.transpose(bb.reshape(SSM_GC, SSM_G, SSM_P), (1, 0, 2))
        wb = jnp.concatenate([_block_diag(to_blocks(bb_re)), _block_diag(to_blocks(bb_im))], axis=1).astype(bf16)
        c_blocks = lambda cc: jnp.transpose(cc, (0, 2, 1))
        wc = jnp.concatenate([_block_diag(c_blocks(ssm_c_re[l])), -_block_diag(c_blocks(ssm_c_im[l]))],
                             axis=0).astype(bf16)
        dvec = ssm_d[l].reshape(1, D_SSM)
        wglu = ssm_w_glu[l].astype(bf16)
        bglu = ssm_b_glu[l].reshape(1, D_SSM)
        wqt = jnp.transpose(peer_wq[l].reshape(dm, PEER_HEADS * PEER_DK)).astype(bf16)
        keys = peer_keys[l].astype(bf16)
        u_tab = peer_u[l].astype(bf16)
        v_tab = peer_v[l].astype(bf16)

        (fq, qaug, kaug, fvb, fk, fv, lf, gq, gk, gv, la, gr, su) = _inproj(
            x, g_mix[l].reshape(1, dm), w_p, bfp, wa2p, ba)

        fox_p = _fox_prompt(qaug, kaug, fvb, seq)
        lfn_t = jnp.pad(jnp.swapaxes(lf[seq:n_tok, :8].reshape(bd, tdec, 8), 1, 2), ((0, 0), (0, 0), (0, LANES - tdec)))
        fox_s = _fox_sample(fq[seq:n_tok].astype(f32), fk[seq:n_tok], fv[seq:n_tok], lfn_t, ck, cv, clft, page_table, l, tdec)
        fox_o = jnp.concatenate([fox_p, fox_s, pad_rows(fox_s)], axis=0)

        gla_p, st_p = _gla(gq, gk, la, gv, jnp.zeros((1, D_GLA, D_GK), f32), e_mat, mask_t,
                           row0=0, nseq=1, steps=seq // GLA_STEP, rows_per_step=GLA_STEP, chunk=GLA_CHUNK)
        s0 = state_gla[l].astype(f32)
        s0t = jnp.transpose(s0, (0, 1, 3, 2))
        s0t = (s0t[:, :, :, None, :] * jnp.eye(H_GLA, dtype=f32)[None, :, None, :, None]).reshape(bd, D_GLA, D_GK)
        gla_s, st_s = _gla(gq, gk, la, gv, s0t, e_mat, mask_t,
                           row0=seq, nseq=bd, steps=1, rows_per_step=tdec, chunk=tdec)
        gla_o = jnp.concatenate([gla_p, gla_s, pad_rows(gla_s)], axis=0)

        def unpack_state(st):
            st = st.reshape(-1, H_GLA, GLA_DV, H_GLA, GLA_DK)
            st = jnp.stack([st[:, h, :, h, :] for h in range(H_GLA)], axis=1)
            return jnp.transpose(st, (0, 1, 3, 2))

        ssm_p, hre_p, him_p = _ssm(su, wb, p_re, p_im, zeros_state, zeros_state, wc, dvec, wglu, bglu,
                                   row0=0, n_rows=seq, tile_carry=True)
        h0r = jnp.repeat(state_ssm_re[l].reshape(bd, D_STATE).astype(f32), tdec, axis=0)
        h0i = jnp.repeat(state_ssm_im[l].reshape(bd, D_STATE).astype(f32), tdec, axis=0)
        ssm_s, hre_s, him_s = _ssm(su, wb, p_re, p_im, h0r, h0i, wc, dvec, wglu, bglu,
                                   row0=seq, n_rows=n_s, tile_carry=False)
        ssm_o = jnp.concatenate([ssm_p, ssm_s, pad_rows(ssm_s)], axis=0)

        x1, xf = _outproj(fox_o, gla_o, gr, ssm_o, x, fox_norm[l].reshape(1, D_FOX),
                          jnp.tile(gla_norm[l], H_GLA).reshape(1, D_GLA), avg,
                          w_out[l].astype(bf16), g_ffn[l].reshape(1, dm))
        i1, i2, gate = _peer_topk(xf, wqt, keys)
        x = _peer_dense(xf, i1, i2, gate, u_tab, v_tab, x1, g_final.reshape(1, dm), l == depth - 1)

        outs_p.append((fk[:seq].reshape(bp, seq, H_FOX, FOX_DH), fv[:seq].reshape(bp, seq, H_FOX, FOX_DH),
                       lf[:seq, :H_FOX].reshape(bp, seq, H_FOX), unpack_state(st_p),
                       hre_p.reshape(bp, SSM_G, SSM_P), him_p.reshape(bp, SSM_G, SSM_P)))
        outs_s.append((fk[seq:n_tok].reshape(bd, tdec, H_FOX, FOX_DH), fv[seq:n_tok].reshape(bd, tdec, H_FOX, FOX_DH),
                       lf[seq:n_tok, :H_FOX].reshape(bd, tdec, H_FOX), unpack_state(st_s),
                       hre_s[tdec - 1::tdec].reshape(bd, SSM_G, SSM_P), him_s[tdec - 1::tdec].reshape(bd, SSM_G, SSM_P)))

    p_stk = [jnp.stack(t) for t in zip(*outs_p)]
    s_stk = [jnp.stack(t) for t in zip(*outs_s)]
    y_prompt = x[:seq].reshape(bp, seq, dm)
    y_sample = x[seq:n_tok].reshape(bd, tdec, dm)
    return (y_prompt, y_sample, *p_stk, *s_stk)
```

```python
import functools

import numpy as np
import jax
import jax.numpy as jnp
from jax import lax
from jax.experimental import pallas as pl
from jax.experimental.pallas import tpu as pltpu

f32 = jnp.float32
bf16 = jnp.bfloat16
i32 = jnp.int32

EPS = 1e-6
LOG2E = 1.4426950408889634
LANES = 128
VMEM_LIMIT = 56 * 1024 * 1024

H_FOX, FOX_DH = 6, 64
D_FOX = H_FOX * FOX_DH
H_GLA, GLA_DK, GLA_DV = 6, 32, 64
D_GK = H_GLA * GLA_DK
D_GLA = H_GLA * GLA_DV
GLA_RANK = 16
GLA_TAU = 16.0
SSM_G, SSM_GC, SSM_P = 16, 16, 64
D_SSM = SSM_G * SSM_GC
D_STATE = SSM_G * SSM_P
PEER_KEYS, PEER_HEADS, PEER_TOPK, PEER_DK = 128, 8, 16, 256
PEER_SLOTS = PEER_HEADS * PEER_TOPK

C_FQ, C_FK, C_FV = 0, 384, 768
C_GQ, C_GK = 1152, 1408
C_GV, C_GR, C_SU, C_TAIL = 1664, 2048, 2432, 2688
D_INP = 2816
TAIL_FF, TAIL_GA = 0, 8

TOK_TILE = 256
FOX_TILE = 1024
FOX_PAGES = 16
GLA_STEP = 512
GLA_CHUNK = 16
PEER_TOK = 512
PEER_ETILE = 1024
GATE_GROUP = 16
GLA_UNROLL = 4


def _dot(a, b):
    return jnp.dot(a, b, preferred_element_type=f32)


def _dot_nt(a, b):
    return lax.dot_general(a, b, (((1,), (1,)), ((), ())), preferred_element_type=f32)


def _dot_tn(a, b):
    return lax.dot_general(a, b, (((0,), (0,)), ((), ())), preferred_element_type=f32)


def _split3(x):
    h = x.astype(bf16)
    r = x - h.astype(f32)
    m = r.astype(bf16)
    l = (r - m.astype(f32)).astype(bf16)
    return h, m, l


def _dot3_lhs01(sel, x):
    h, m, l = _split3(x)
    return _dot(sel, h) + _dot(sel, m) + _dot(sel, l)


def _dot3_rhs01(x, sel):
    h, m, l = _split3(x)
    return _dot(h, sel) + _dot(m, sel) + _dot(l, sel)


def _log_sigmoid(x):
    return -(jnp.maximum(-x, 0.0) + jnp.log1p(jnp.exp(-jnp.abs(x))))


def _iota(shape, dim):
    return lax.broadcasted_iota(i32, shape, dim)


def _div_pow2(x, n):
    assert n & (n - 1) == 0
    return jnp.right_shift(x, n.bit_length() - 1)


def _mod_pow2(x, n):
    assert n & (n - 1) == 0
    return jnp.bitwise_and(x, n - 1)


def _params(sem, vmem=None):
    return pltpu.CompilerParams(dimension_semantics=sem, vmem_limit_bytes=vmem)


def _fox_bias_lanes():
    place_q = np.zeros((3, LANES, H_FOX * LANES), np.float32)
    place_k = np.zeros((3, LANES, H_FOX * LANES), np.float32)
    ones_q = np.zeros((1, H_FOX * LANES), np.float32)
    ones_k = np.zeros((1, H_FOX * LANES), np.float32)
    for j in range(H_FOX):
        base = j * LANES + (FOX_DH if j % 2 == 0 else 0)
        for p in range(3):
            place_q[p, j, base + p] = 1.0
            ones_k[0, base + p] = 1.0
            ones_q[0, base + 3 + p] = 1.0
            place_k[p, j, base + 3 + p] = 1.0
    return place_q, place_k, ones_q, ones_k


def _inproj_body(x_ref, g_ref, w_ref, bfp_ref, wa2_ref, ba_ref, pq_ref, pk_ref, oq_ref, ok_ref,
                 fq_ref, qa_ref, ka_ref, fvb_ref, fk_ref, fv_ref, lf_ref,
                 gq_ref, gk_ref, gv_ref, la_ref, gr_ref, su_ref, carry_ref):
    @pl.when(pl.program_id(0) == 0)
    def _init():
        carry_ref[...] = jnp.zeros_like(carry_ref)

    x = x_ref[...]
    tt = x.shape[0]
    xn = x * lax.rsqrt(jnp.mean(x * x, axis=-1, keepdims=True) + EPS) * g_ref[...]
    z = _dot(xn.astype(bf16), w_ref[...])
    fq_ref[...] = (z[:, C_FQ:C_FQ + D_FOX] * (FOX_DH ** -0.5)).astype(bf16)
    fk = z[:, C_FK:C_FK + D_FOX]
    fv = z[:, C_FV:C_FV + D_FOX]
    fk_ref[...] = fk
    fv_ref[...] = fv
    fvb_ref[...] = fv.astype(bf16)
    gq_ref[...] = z[:, C_GQ:C_GQ + D_GK] * (GLA_DK ** -0.5)
    gk_ref[...] = z[:, C_GK:C_GK + D_GK]
    gv_ref[...] = z[:, C_GV:C_GV + D_GLA]
    gr_ref[...] = z[:, C_GR:C_GR + D_GLA]
    su_ref[...] = z[:, C_SU:C_SU + D_SSM]
    tail = z[:, C_TAIL:C_TAIL + LANES]
    lane = _iota((tt, LANES), 1)
    lf = jnp.where(lane < H_FOX, _log_sigmoid(tail + bfp_ref[...]), 0.0)
    lf_ref[...] = lf
    tri = jnp.where(_iota((tt, tt), 0) >= _iota((tt, tt), 1), 1.0, 0.0).astype(bf16)
    cs = _dot3_lhs01(tri, lf) + carry_ref[...]
    carry_ref[...] = cs[tt - 1:tt, :]
    pieces = _split3(cs * LOG2E)
    q_bias = sum(_dot(pc, pq_ref[i]) for i, pc in enumerate(pieces)) + oq_ref[...]
    k_bias = ok_ref[...] - sum(_dot(pc, pk_ref[i]) for i, pc in enumerate(pieces))
    low = lane < FOX_DH
    for hp in range(H_FOX // 2):
        qt = z[:, C_FQ + hp * LANES:C_FQ + (hp + 1) * LANES] * (LOG2E * FOX_DH ** -0.5)
        kt = z[:, C_FK + hp * LANES:C_FK + (hp + 1) * LANES]
        for hh in range(2):
            blk = slice((2 * hp + hh) * LANES, (2 * hp + hh + 1) * LANES)
            real = low if hh == 0 else jnp.logical_not(low)
            qa_ref[:, blk] = jnp.where(real, qt, q_bias[:, blk]).astype(bf16)
            ka_ref[:, blk] = jnp.where(real, kt, k_bias[:, blk]).astype(bf16)
    za = _dot(tail.astype(bf16), wa2_ref[...]) + ba_ref[...]
    la_ref[...] = _log_sigmoid(za) * (1.0 / GLA_TAU)


def _inproj(x, g_mix, w_p, bfp, wa2p, ba):
    t = x.shape[0]
    tt = TOK_TILE
    row = lambda n: pl.BlockSpec((tt, n), lambda i: (i, 0))
    const = lambda a: pl.BlockSpec(a.shape, lambda i: (0,) * a.ndim)
    outs = [
        (D_FOX, bf16), (H_FOX * LANES, bf16), (H_FOX * LANES, bf16), (D_FOX, bf16), (D_FOX, f32), (D_FOX, f32),
        (LANES, f32),
        (D_GK, f32), (D_GK, f32), (D_GLA, f32), (D_GK, f32), (D_GLA, f32), (D_SSM, f32),
    ]
    place_q, place_k, ones_q, ones_k = _fox_bias_lanes()
    consts = [g_mix, w_p, bfp, wa2p, ba, jnp.asarray(place_q, bf16), jnp.asarray(place_k, bf16),
              jnp.asarray(ones_q), jnp.asarray(ones_k)]
    return pl.pallas_call(
        _inproj_body,
        grid=(t // tt,),
        in_specs=[row(x.shape[1])] + [const(a) for a in consts],
        out_specs=[row(n) for n, _ in outs],
        out_shape=[jax.ShapeDtypeStruct((t, n), d) for n, d in outs],
        scratch_shapes=[pltpu.VMEM((1, LANES), f32)],
        compiler_params=_params(("arbitrary",), VMEM_LIMIT),
        name="inproj",
    )(x, *consts)


def _fox_prompt_body(qi_ref, ki_ref, qa_ref, qb_ref, ka_ref, kb_ref, v_ref, o_ref,
                     m_ref, l_ref, acc_ref):
    p = pl.program_id(1)
    qi = qi_ref[p]
    ki = ki_ref[p]
    tq, tk = qa_ref.shape[0], ka_ref.shape[0]

    @pl.when(ki == 0)
    def _init():
        m_ref[...] = jnp.full_like(m_ref, -jnp.inf)
        l_ref[...] = jnp.zeros_like(l_ref)
        acc_ref[...] = jnp.zeros_like(acc_ref)

    def step(diagonal):
        vext = jnp.concatenate([v_ref[...], jnp.ones((tk, LANES), bf16)], axis=1)
        scores = [_dot_nt(qa_ref[...], ka_ref[...]), _dot_nt(qb_ref[...], kb_ref[...])]
        for hh in range(2):
            s = scores[hh]
            if diagonal:
                s = jnp.where(_iota((tq, tk), 0) >= _iota((tq, tk), 1), s, -jnp.inf)
            m_old = m_ref[hh]
            m_new = jnp.maximum(m_old, jnp.max(s, axis=-1, keepdims=True))
            pr = jnp.exp2(s - jnp.tile(m_new, (1, tk // LANES)))
            alpha = jnp.exp2(m_old - m_new)
            pv = _dot(pr.astype(bf16), vext)
            l_ref[hh] = alpha * l_ref[hh] + pv[:, LANES:]
            m_ref[hh] = m_new
            acc_ref[hh] = alpha * acc_ref[hh] + pv[:, :LANES]

    @pl.when(ki < qi)
    def _off():
        step(False)

    @pl.when(ki == qi)
    def _diag():
        step(True)
        lane = _iota((1, LANES), 1)
        o_ref[...] = jnp.where(lane < FOX_DH, acc_ref[0] / l_ref[0], acc_ref[1] / l_ref[1])


def _fox_prompt(qaug, kaug, fvb, seq):
    tq = FOX_TILE
    nq = seq // tq
    qi_tbl = np.concatenate([np.full(i + 1, i) for i in range(nq)]).astype(np.int32)
    ki_tbl = np.concatenate([np.arange(i + 1) for i in range(nq)]).astype(np.int32)
    grid_spec = pltpu.PrefetchScalarGridSpec(
        num_scalar_prefetch=2,
        grid=(H_FOX // 2, len(qi_tbl)),
        in_specs=[
            pl.BlockSpec((tq, LANES), lambda h, p, qi, ki: (qi[p], 2 * h)),
            pl.BlockSpec((tq, LANES), lambda h, p, qi, ki: (qi[p], 2 * h + 1)),
            pl.BlockSpec((tq, LANES), lambda h, p, qi, ki: (ki[p], 2 * h)),
            pl.BlockSpec((tq, LANES), lambda h, p, qi, ki: (ki[p], 2 * h + 1)),
            pl.BlockSpec((tq, LANES), lambda h, p, qi, ki: (ki[p], h)),
        ],
        out_specs=pl.BlockSpec((tq, LANES), lambda h, p, qi, ki: (qi[p], h)),
        scratch_shapes=[pltpu.VMEM((2, tq, LANES), f32), pltpu.VMEM((2, tq, LANES), f32),
                        pltpu.VMEM((2, tq, LANES), f32)],
    )
    return pl.pallas_call(
        _fox_prompt_body,
        grid_spec=grid_spec,
        out_shape=jax.ShapeDtypeStruct((seq, D_FOX), f32),
        compiler_params=_params(("arbitrary", "arbitrary"), VMEM_LIMIT),
        name="fox_prompt",
    )(jnp.asarray(qi_tbl), jnp.asarray(ki_tbl), qaug, qaug, kaug, kaug, fvb)


def _expand_heads(x, rows):
    return jnp.concatenate(
        [jnp.broadcast_to(x[h:h + 1, :], (rows, x.shape[1])) for h in range(H_FOX)], axis=0)


def _fox_sample_body(pt_ref, q_ref, kn_ref, vn_ref, lfn_ref, *rest):
    del pt_ref
    npg = FOX_PAGES
    kp_refs = rest[0:npg]
    vp_refs = rest[npg:2 * npg]
    lfp_refs = rest[2 * npg:3 * npg]
    o_ref = rest[3 * npg]
    m_ref, l_ref, acc_ref, car_ref, qbd_ref, cq_ref = rest[3 * npg + 1:]
    j = pl.program_id(1)
    t = q_ref.shape[0]
    hq = H_FOX * t
    page = kp_refs[0].shape[-1]

    def update(s, weigh):
        m_old = m_ref[...]
        m_new = jnp.maximum(m_old, jnp.max(s, axis=-1, keepdims=True))
        pr = jnp.exp(s - m_new)
        alpha = jnp.exp(m_old - m_new)
        l_ref[...] = alpha * l_ref[...] + jnp.sum(pr, axis=-1, keepdims=True)
        m_ref[...] = m_new
        acc_ref[...] = alpha * acc_ref[...] + weigh(pr.astype(bf16))

    @pl.when(j == 0)
    def _new_tokens():
        m_ref[...] = jnp.full_like(m_ref, -jnp.inf)
        l_ref[...] = jnp.zeros_like(l_ref)
        acc_ref[...] = jnp.zeros_like(acc_ref)
        car_ref[...] = jnp.zeros_like(car_ref)
        q = q_ref[...]
        q6 = jnp.concatenate([q] * H_FOX, axis=0)
        keep = _div_pow2(_iota((hq, D_FOX), 0), t) == _div_pow2(_iota((hq, D_FOX), 1), FOX_DH)
        qbd_ref[...] = jnp.where(keep, q6, 0.0).astype(bf16)
        cum = lfn_ref[...]
        lane8 = _iota(cum.shape, 1)
        sh = 1
        while sh < t:
            cum = cum + jnp.where(lane8 >= sh, pltpu.roll(cum, sh, 1), 0.0)
            sh *= 2
        cum_rows = _expand_heads(cum, t)
        qpos = _mod_pow2(_iota((hq, LANES), 0), t)
        kpos = _iota((hq, LANES), 1)
        cq = jnp.sum(jnp.where(kpos == qpos, cum_rows, 0.0), axis=-1, keepdims=True)
        cq_ref[...] = cq
        zpad = jnp.zeros((LANES - t, D_FOX), f32)
        kn = jnp.concatenate([kn_ref[...], zpad], axis=0).astype(bf16)
        vn = jnp.concatenate([vn_ref[...], zpad], axis=0).astype(bf16)
        s = _dot_nt(qbd_ref[...], kn) + cq - cum_rows
        s = jnp.where(kpos <= qpos, s, -jnp.inf)
        update(s, lambda pr: _dot(pr, vn))

    upper = jnp.where(_iota((page, page), 0) > _iota((page, page), 1), 1.0, 0.0).astype(bf16)
    qbd = qbd_ref[...]
    qk = [_dot(qbd, kp_refs[g][...].reshape(D_FOX, page).astype(bf16)) for g in range(npg)]
    lf_all = jnp.concatenate([lfp_refs[g][...] for g in range(npg)], axis=0)
    suf_all = _dot3_rhs01(lf_all, upper)
    tot_all = jnp.sum(lf_all, axis=-1, keepdims=True)
    car = car_ref[...]
    scores = []
    for g in range(npg):
        scores.append(qk[g] + _expand_heads(suf_all[g * 8:(g + 1) * 8, :] + car, t))
        car = car + tot_all[g * 8:(g + 1) * 8, :]
    car_ref[...] = car

    def weigh_pages(pr):
        out = None
        for g in range(npg):
            vpt = vp_refs[g][...].reshape(D_FOX, page).astype(bf16)
            part = _dot_nt(pr[:, g * page:(g + 1) * page], vpt)
            out = part if out is None else out + part
        return out

    update(jnp.concatenate(scores, axis=1) + cq_ref[...], weigh_pages)

    @pl.when(j == pl.num_programs(1) - 1)
    def _finish():
        o48 = acc_ref[...] / l_ref[...]
        head = _div_pow2(_iota((t, D_FOX), 1), FOX_DH)
        out = jnp.zeros((t, D_FOX), f32)
        for h in range(H_FOX):
            out = out + jnp.where(head == h, o48[h * t:(h + 1) * t, :], 0.0)
        o_ref[...] = out


def _fox_sample(q_s, k_s, v_s, lfn_t, cache_k, cache_v, cache_lft, page_table, layer, t):
    nb, n_pages = page_table.shape
    page = cache_k.shape[-1]
    npg = FOX_PAGES
    hq = H_FOX * t
    seq_spec = pl.BlockSpec((t, D_FOX), lambda b, j, pt: (b, 0))

    def page_spec(shape, g):
        zeros = (0,) * len(shape)
        return pl.BlockSpec((None, None) + shape,
                            lambda b, j, pt: (layer, pt[b, n_pages - 1 - (j * npg + g)]) + zeros)

    grid_spec = pltpu.PrefetchScalarGridSpec(
        num_scalar_prefetch=1,
        grid=(nb, n_pages // npg),
        in_specs=[seq_spec, seq_spec, seq_spec,
                  pl.BlockSpec((None, 8, LANES), lambda b, j, pt: (b, 0, 0))]
        + [page_spec((H_FOX, FOX_DH, page), g) for g in range(npg)]
        + [page_spec((H_FOX, FOX_DH, page), g) for g in range(npg)]
        + [page_spec((8, page), g) for g in range(npg)],
        out_specs=seq_spec,
        scratch_shapes=[pltpu.VMEM((hq, 1), f32), pltpu.VMEM((hq, 1), f32),
                        pltpu.VMEM((hq, D_FOX), f32), pltpu.VMEM((8, 1), f32),
                        pltpu.VMEM((hq, D_FOX), bf16), pltpu.VMEM((hq, 1), f32)],
    )
    return pl.pallas_call(
        _fox_sample_body,
        grid_spec=grid_spec,
        out_shape=jax.ShapeDtypeStruct((nb * t, D_FOX), f32),
        compiler_params=_params(("arbitrary", "arbitrary"), VMEM_LIMIT),
        name="fox_sample",
    )(page_table, q_s, k_s, v_s, lfn_t, *([cache_k] * npg), *([cache_v] * npg), *([cache_lft] * npg))


def _gla_body(q_ref, k_ref, la_ref, v_ref, s0_ref, e_ref, mask_ref, o_ref, sout_ref, st_ref, *, chunk):
    i = pl.program_id(1)

    @pl.when(i == 0)
    def _init():
        st_ref[...] = s0_ref[...]

    c = chunk
    n_chunks = q_ref.shape[0] // c
    row = _iota((c, D_GK), 0)

    def one_chunk(ci, carry):
        r0 = pl.multiple_of(ci * c, c)
        q = q_ref[pl.ds(r0, c), :]
        k = k_ref[pl.ds(r0, c), :]
        v = v_ref[pl.ds(r0, c), :]
        b = la_ref[pl.ds(r0, c), :]
        sh = 1
        while sh < c:
            b = b + jnp.where(row >= sh, pltpu.roll(b, sh, 0), 0.0)
            sh *= 2
        pieces = []
        for s in range(c):
            d = jnp.where(row >= s, b - b[s:s + 1, :], -jnp.inf)
            pieces.append(q * k[s:s + 1, :] * jnp.exp(d))
        pst = jnp.concatenate(pieces, axis=0)
        hi = pst.astype(bf16)
        lo = (pst - hi.astype(f32)).astype(bf16)
        att = _dot(hi, e_ref[...]) + _dot(lo, e_ref[...])
        b_last = b[c - 1:c, :]
        kd = (k * jnp.exp(b_last - b)).astype(bf16)
        kv = _dot_tn(v.astype(bf16), kd) * mask_ref[...]
        qe = (q * jnp.exp(b)).astype(bf16)
        o = att[0:c, :] * v[0:1, :]
        for s in range(1, c):
            o = o + att[s * c:(s + 1) * c, :] * v[s:s + 1, :]
        st = st_ref[...]
        o_ref[pl.ds(r0, c), :] = o + _dot_nt(qe, st.astype(bf16))
        st_ref[...] = st * jnp.exp(b_last) + kv
        return carry

    lax.fori_loop(0, n_chunks, one_chunk, 0, unroll=min(GLA_UNROLL, n_chunks))

    @pl.when(i == pl.num_programs(1) - 1)
    def _finish():
        sout_ref[...] = st_ref[...]


def _gla(gq, gk, la, gv, s0t, e_mat, mask_t, *, row0, nseq, steps, rows_per_step, chunk):
    blk0 = row0 // rows_per_step
    rows = lambda n: pl.BlockSpec((rows_per_step, n), lambda s, i: (blk0 + s * steps + i, 0))
    orow = pl.BlockSpec((rows_per_step, D_GLA), lambda s, i: (s * steps + i, 0))
    st_spec = pl.BlockSpec((None, D_GLA, D_GK), lambda s, i: (s, 0, 0))
    const = lambda a: pl.BlockSpec(a.shape, lambda s, i: (0,) * a.ndim)
    return pl.pallas_call(
        functools.partial(_gla_body, chunk=chunk),
        grid=(nseq, steps),
        in_specs=[rows(D_GK), rows(D_GK), rows(D_GK), rows(D_GLA), st_spec, const(e_mat), const(mask_t)],
        out_specs=[orow, st_spec],
        out_shape=[jax.ShapeDtypeStruct((nseq * steps * rows_per_step, D_GLA), f32),
                   jax.ShapeDtypeStruct((nseq, D_GLA, D_GK), f32)],
        scratch_shapes=[pltpu.VMEM((D_GLA, D_GK), f32)],
        compiler_params=_params(("arbitrary", "arbitrary"), VMEM_LIMIT),
        name="gla",
    )(gq, gk, la, gv, s0t, e_mat, mask_t)


def _ssm_prep_body(are_ref, aim_ref, ldt_ref, bre_ref, bim_ref,
                   pre_ref, pim_ref, bbre_ref, bbim_ref):
    a_re = are_ref[...]
    a_im = aim_ref[...]
    dt = jnp.exp(ldt_ref[...])
    mag = jnp.exp(a_re * dt)
    abar_re = mag * jnp.cos(a_im * dt)
    abar_im = mag * jnp.sin(a_im * dt)
    den = a_re * a_re + a_im * a_im
    nr, ni = abar_re - 1.0, abar_im
    fr = (nr * a_re + ni * a_im) / den
    fi = (ni * a_re - nr * a_im) / den
    bbre_ref[...] = fr * bre_ref[...] - fi * bim_ref[...]
    bbim_ref[...] = fr * bim_ref[...] + fi * bre_ref[...]
    n = pre_ref.shape[0]
    pre_ref[0:1, :] = abar_re
    pim_ref[0:1, :] = abar_im
    m = 1
    while m < n:
        tr = pre_ref[m - 1:m, :]
        ti = pim_ref[m - 1:m, :]
        xr = pre_ref[0:m, :]
        xi = pim_ref[0:m, :]
        pre_ref[m:2 * m, :] = xr * tr - xi * ti
        pim_ref[m:2 * m, :] = xr * ti + xi * tr
        m *= 2


def _ssm_prep(a_re, a_im, log_dt, bt_re, bt_im, n_pow):
    outs = [(n_pow, D_STATE), (n_pow, D_STATE), (SSM_GC, D_STATE), (SSM_GC, D_STATE)]
    return pl.pallas_call(
        _ssm_prep_body,
        out_shape=[jax.ShapeDtypeStruct(s, f32) for s in outs],
        name="ssm_prep",
    )(a_re, a_im, log_dt, bt_re, bt_im)


def _ssm_body(u_ref, wb_ref, pre_ref, pim_ref, hin_re_ref, hin_im_ref, wc_ref, d_ref, wg_ref, bg_ref,
              o_ref, hre_ref, him_ref, car_re_ref, car_im_ref, *, tile_carry):
    i = pl.program_id(0)
    u = u_ref[...]
    tt = u.shape[0]
    grp = pre_ref.shape[0]
    d = _dot(u.astype(bf16), wb_ref[...])
    xr = d[:, :D_STATE]
    xi = d[:, D_STATE:]
    pos = _mod_pow2(_iota((tt, D_STATE), 0), grp)
    sh = 1
    while sh < grp:
        ar = pre_ref[sh - 1:sh, :]
        ai = pim_ref[sh - 1:sh, :]
        sr = jnp.where(pos >= sh, pltpu.roll(xr, sh, 0), 0.0)
        si = jnp.where(pos >= sh, pltpu.roll(xi, sh, 0), 0.0)
        xr, xi = xr + ar * sr - ai * si, xi + ar * si + ai * sr
        sh *= 2
    pr = pre_ref[...]
    pi = pim_ref[...]
    if tile_carry:
        @pl.when(i == 0)
        def _init():
            car_re_ref[...] = hin_re_ref[...]
            car_im_ref[...] = hin_im_ref[...]
        cr = car_re_ref[...]
        ci = car_im_ref[...]
        hr_groups, hi_groups = [], []
        for g in range(tt // grp):
            gr = xr[g * grp:(g + 1) * grp, :] + pr * cr - pi * ci
            gi = xi[g * grp:(g + 1) * grp, :] + pr * ci + pi * cr
            cr, ci = gr[grp - 1:grp, :], gi[grp - 1:grp, :]
            hr_groups.append(gr)
            hi_groups.append(gi)
        hr = jnp.concatenate(hr_groups, axis=0)
        hi = jnp.concatenate(hi_groups, axis=0)
        car_re_ref[...] = cr
        car_im_ref[...] = ci
        hre_ref[...] = cr
        him_ref[...] = ci
    else:
        h0r = hin_re_ref[...]
        h0i = hin_im_ref[...]
        prt = jnp.tile(pr, (tt // grp, 1))
        pit = jnp.tile(pi, (tt // grp, 1))
        hr = xr + prt * h0r - pit * h0i
        hi = xi + prt * h0i + pit * h0r
        hre_ref[...] = hr
        him_ref[...] = hi
    hcat = jnp.concatenate([hr, hi], axis=1).astype(bf16)
    y = _dot(hcat, wc_ref[...]) + d_ref[...] * u
    zg = jax.nn.gelu(y)
    o_ref[...] = (zg * jax.nn.sigmoid(_dot(zg.astype(bf16), wg_ref[...]) + bg_ref[...])).astype(bf16)


def _ssm(su, wb, p_re, p_im, hin_re, hin_im, wc, dvec, wglu, bglu, *, row0, n_rows, tile_carry):
    tt = TOK_TILE
    blk0 = row0 // tt
    const = lambda a: pl.BlockSpec(a.shape, lambda i: (0,) * a.ndim)
    if tile_carry:
        h_spec = const(hin_re)
        hout_spec = pl.BlockSpec((1, D_STATE), lambda i: (0, 0))
        hout_shape = jax.ShapeDtypeStruct((1, D_STATE), f32)
    else:
        h_spec = pl.BlockSpec((tt, D_STATE), lambda i: (i, 0))
        hout_spec = h_spec
        hout_shape = jax.ShapeDtypeStruct((n_rows, D_STATE), f32)
    return pl.pallas_call(
        functools.partial(_ssm_body, tile_carry=tile_carry),
        grid=(n_rows // tt,),
        in_specs=[pl.BlockSpec((tt, D_SSM), lambda i: (blk0 + i, 0)), const(wb), const(p_re), const(p_im),
                  h_spec, h_spec, const(wc), const(dvec), const(wglu), const(bglu)],
        out_specs=[pl.BlockSpec((tt, D_SSM), lambda i: (i, 0)), hout_spec, hout_spec],
        out_shape=[jax.ShapeDtypeStruct((n_rows, D_SSM), bf16), hout_shape, hout_shape],
        scratch_shapes=[pltpu.VMEM((1, D_STATE), f32), pltpu.VMEM((1, D_STATE), f32)],
        compiler_params=_params(("arbitrary",), VMEM_LIMIT),
        name="ssm",
    )(su, wb, p_re, p_im, hin_re, hin_im, wc, dvec, wglu, bglu)


def _outproj_body(fox_ref, gla_ref, gr_ref, ssm_ref, x_ref, fn_ref, gn_ref, avg_ref, w_ref, gf_ref,
                  x1_ref, xf_ref):
    fo = fox_ref[...]
    fo = fo * lax.rsqrt(jnp.mean(fo * fo, axis=-1, keepdims=True) + EPS) * fn_ref[...]
    go = gla_ref[...]
    ms = _dot3_rhs01(go * go, avg_ref[...]) * (1.0 / GLA_DV)
    go = go * lax.rsqrt(ms + EPS) * gn_ref[...] * jax.nn.silu(gr_ref[...])
    w = w_ref
    y = (_dot(fo.astype(bf16), w[0:D_FOX, :]) + _dot(go.astype(bf16), w[D_FOX:D_FOX + D_GLA, :])
         + _dot(ssm_ref[...], w[D_FOX + D_GLA:, :]))
    x1 = x_ref[...] + y
    x1_ref[...] = x1
    xf = x1 * lax.rsqrt(jnp.mean(x1 * x1, axis=-1, keepdims=True) + EPS) * gf_ref[...]
    xf_ref[...] = xf.astype(bf16)


def _outproj(fox_o, gla_o, gr, ssm_o, x, fox_norm, gla_norm6, avg, w_out, g_ffn):
    t, dm = x.shape
    tt = TOK_TILE
    row = lambda n: pl.BlockSpec((tt, n), lambda i: (i, 0))
    const = lambda a: pl.BlockSpec(a.shape, lambda i: (0,) * a.ndim)
    return pl.pallas_call(
        _outproj_body,
        grid=(t // tt,),
        in_specs=[row(D_FOX), row(D_GLA), row(D_GLA), row(D_SSM), row(dm),
                  const(fox_norm), const(gla_norm6), const(avg), const(w_out), const(g_ffn)],
        out_specs=[row(dm), row(dm)],
        out_shape=[jax.ShapeDtypeStruct((t, dm), f32), jax.ShapeDtypeStruct((t, dm), bf16)],
        compiler_params=_params(("arbitrary",), VMEM_LIMIT),
        name="outproj",
    )(fox_o, gla_o, gr, ssm_o, x, fox_norm, gla_norm6, avg, w_out, g_ffn)


def _topk_rows(s, k, ids=None):
    it = _iota(s.shape, 0) if ids is None else ids
    big = jnp.iinfo(jnp.int32).max
    vals, idxs = [], []
    for _ in range(k):
        m = jnp.max(s, axis=0, keepdims=True)
        idx = jnp.min(jnp.where(s == m, it, big), axis=0, keepdims=True)
        vals.append(m)
        idxs.append(idx)
        s = jnp.where(it == idx, -jnp.inf, s)
    return jnp.concatenate(vals, axis=0), jnp.concatenate(idxs, axis=0)


def _pair_candidates(v1, v2, tt):
    kk = v1.shape[0]
    blocks, ids = [], []
    for a in range(kk):
        nb = kk // (a + 1)
        rows = -(-nb // 8) * 8
        blk = v1[a:a + 1, :] + v2[0:rows, :]
        b_id = _iota((rows, tt), 0)
        if nb < rows:
            blk = jnp.where(b_id < nb, blk, -jnp.inf)
        blocks.append(blk)
        ids.append(b_id + a * kk)
    return jnp.concatenate(blocks, axis=0), jnp.concatenate(ids, axis=0)


def _take_rows(table, sel):
    return jnp.sum(jnp.where(_iota(table.shape, 0) == sel, table, 0), axis=0, keepdims=True)


def _peer_select(qt, keys1, keys2):
    kk = PEER_TOPK
    half = PEER_DK // 2
    v1, i1 = _topk_rows(_dot(keys1, qt[:half, :].astype(bf16)), kk)
    v2, i2 = _topk_rows(_dot(keys2, qt[half:, :].astype(bf16)), kk)
    cand, cand_id = _pair_candidates(v1, v2, qt.shape[1])
    sc, pos = _topk_rows(cand, kk, cand_id)
    sel1 = jnp.concatenate([_take_rows(i1, _div_pow2(pos[j:j + 1, :], kk)) for j in range(kk)], axis=0)
    sel2 = jnp.concatenate([_take_rows(i2, _mod_pow2(pos[j:j + 1, :], kk)) for j in range(kk)], axis=0)
    ex = jnp.exp(sc - sc[0:1, :])
    return sel1, sel2, ex / jnp.sum(ex, axis=0, keepdims=True)


def _peer_topk_body(xf_ref, wqt_ref, keys_ref, i1_ref, i2_ref, g_ref):
    qt = _dot_nt(wqt_ref[...], xf_ref[...])
    picks = [_peer_select(qt[h * PEER_DK:(h + 1) * PEER_DK, :], keys_ref[h, 0], keys_ref[h, 1])
             for h in range(PEER_HEADS)]
    i1_ref[...] = jnp.concatenate([p[0] for p in picks], axis=0).T
    i2_ref[...] = jnp.concatenate([p[1] for p in picks], axis=0).T
    g_ref[...] = jnp.concatenate([p[2] for p in picks], axis=0).T


def _peer_topk(xf, wqt, keys):
    t, dm = xf.shape
    tt = TOK_TILE
    const = lambda a: pl.BlockSpec(a.shape, lambda i: (0,) * a.ndim)
    slot = pl.BlockSpec((tt, PEER_SLOTS), lambda i: (i, 0))
    return pl.pallas_call(
        _peer_topk_body,
        grid=(t // tt,),
        in_specs=[pl.BlockSpec((tt, dm), lambda i: (i, 0)), const(wqt), const(keys)],
        out_specs=[slot, slot, slot],
        out_shape=[jax.ShapeDtypeStruct((t, PEER_SLOTS), i32), jax.ShapeDtypeStruct((t, PEER_SLOTS), i32),
                   jax.ShapeDtypeStruct((t, PEER_SLOTS), f32)],
        compiler_params=_params(("arbitrary",), VMEM_LIMIT),
        name="peer_topk",
    )(xf, wqt, keys)


def _peer_dense_body(xf_ref, i1_ref, i2_ref, g_ref, u_ref, v_ref, x1_ref, gfin_ref, o_ref,
                     gate_ref, acc_ref, *, final_norm):
    e = pl.program_id(1)
    tt = xf_ref.shape[0]
    nk = PEER_KEYS
    rows_per_tile = u_ref.shape[0] // nk

    @pl.when(e == 0)
    def _build_gates():
        acc_ref[...] = jnp.zeros_like(acc_ref)
        key_id = _iota((nk, PEER_SLOTS), 0)

        def token_group(gi, carry):
            t0 = pl.multiple_of(gi * GATE_GROUP, GATE_GROUP)
            r1 = i1_ref[pl.ds(t0, GATE_GROUP), :]
            r2 = i2_ref[pl.ds(t0, GATE_GROUP), :]
            gr = g_ref[pl.ds(t0, GATE_GROUP), :]
            mats = []
            for j in range(GATE_GROUP):
                p1 = jnp.where(r1[j:j + 1, :] == key_id, gr[j:j + 1, :], 0.0).astype(bf16)
                p2 = jnp.where(r2[j:j + 1, :] == key_id, 1.0, 0.0).astype(bf16)
                mats.append(_dot_nt(p1, p2))
            gate_ref[:, pl.ds(t0, GATE_GROUP), :] = jnp.swapaxes(jnp.stack(mats, axis=0), 0, 1).astype(bf16)
            return carry

        lax.fori_loop(0, tt // GATE_GROUP, token_group, 0)

    a = _dot_nt(xf_ref[...], u_ref[...])
    act = jax.nn.gelu(a)
    ws = []
    for r in range(rows_per_tile):
        gt = gate_ref[e * rows_per_tile + r]
        ws.append((gt.astype(f32) * act[:, r * nk:(r + 1) * nk]).astype(bf16))
    acc_ref[...] += _dot(jnp.concatenate(ws, axis=1), v_ref[...])

    @pl.when(e == pl.num_programs(1) - 1)
    def _finish():
        y = x1_ref[...] + acc_ref[...]
        if final_norm:
            y = y * lax.rsqrt(jnp.mean(y * y, axis=-1, keepdims=True) + EPS) * gfin_ref[...]
        o_ref[...] = y


def _peer_dense(xf, i1, i2, g, u_tab, v_tab, x1, g_final, final_norm):
    t, dm = xf.shape
    tt = PEER_TOK
    et = PEER_ETILE
    tok = lambda n: pl.BlockSpec((tt, n), lambda i, e: (i, 0))
    tab = pl.BlockSpec((et, dm), lambda i, e: (e, 0))
    return pl.pallas_call(
        functools.partial(_peer_dense_body, final_norm=final_norm),
        grid=(t // tt, u_tab.shape[0] // et),
        in_specs=[tok(dm), tok(PEER_SLOTS), tok(PEER_SLOTS), tok(PEER_SLOTS), tab, tab, tok(dm),
                  pl.BlockSpec(g_final.shape, lambda i, e: (0, 0))],
        out_specs=tok(dm),
        out_shape=jax.ShapeDtypeStruct((t, dm), f32),
        scratch_shapes=[pltpu.VMEM((PEER_KEYS, tt, PEER_KEYS), bf16), pltpu.VMEM((tt, dm), f32)],
        compiler_params=_params(("arbitrary", "arbitrary"), VMEM_LIMIT),
        name="peer_dense",
    )(xf, i1, i2, g, u_tab, v_tab, x1, g_final)


def _pack_w_in(w_in):
    dm = w_in.shape[0]
    sizes = (D_FOX, D_FOX, D_FOX, H_FOX, D_GK, D_GK, D_GLA, GLA_RANK, D_GLA, D_SSM)
    pts = np.cumsum(sizes)[:-1].tolist()
    fq, fk, fv, ff, gq, gk, gv, ga, gr, su = jnp.split(w_in, pts, axis=1)
    z = lambda n: jnp.zeros((dm, n), w_in.dtype)
    tail = jnp.concatenate([ff, z(TAIL_GA - H_FOX), ga, z(LANES - TAIL_GA - GLA_RANK)], axis=1)
    packed = jnp.concatenate([fq, fk, fv, gq, z(C_GK - C_GQ - D_GK), gk, z(C_GV - C_GK - D_GK),
                              gv, gr, su, tail], axis=1)
    assert packed.shape[1] == D_INP
    return packed.astype(bf16)


def _block_diag(blocks):
    g, a, b = blocks.shape
    eye = jnp.eye(g, dtype=blocks.dtype)
    return (blocks[:, :, None, :] * eye[:, None, :, None]).reshape(g * a, g * b)


def _head_match(rows, row_w, cols, col_w):
    return (np.arange(rows)[:, None] // row_w == np.arange(cols)[None, :] // col_w)


def kernel(x_prompt, x_sample, cache_fox_k, cache_fox_v, cache_fox_logf, state_gla, state_ssm_re, state_ssm_im, page_table, g_mix, w_in, fox_bf, fox_norm, gla_wa2, gla_ba, gla_norm, ssm_a_re, ssm_a_im, ssm_log_dt, ssm_b_re, ssm_b_im, ssm_c_re, ssm_c_im, ssm_d, ssm_w_glu, ssm_b_glu, w_out, g_ffn, peer_wq, peer_keys, peer_u, peer_v, g_final):
    bp, seq, dm = x_prompt.shape
    bd, tdec, _ = x_sample.shape
    depth = w_in.shape[0]
    assert bp == 1 and seq % FOX_TILE == 0 and seq % GLA_STEP == 0
    n_s = bd * tdec
    assert n_s == TOK_TILE and tdec == 8
    n_tok = seq + n_s
    n_pool, page = cache_fox_k.shape[1], cache_fox_k.shape[2]
    assert page == LANES and page_table.shape[1] % FOX_PAGES == 0

    n_pad = -n_tok % PEER_TOK
    pad_rows = lambda a: jnp.zeros((n_pad, a.shape[1]), a.dtype)
    x = jnp.concatenate([x_prompt.reshape(seq, dm), x_sample.reshape(n_s, dm), jnp.zeros((n_pad, dm), f32)], axis=0)
    ck = jnp.transpose(cache_fox_k, (0, 1, 3, 4, 2))
    cv = jnp.transpose(cache_fox_v, (0, 1, 3, 4, 2))
    clft = jnp.pad(jnp.swapaxes(cache_fox_logf, 2, 3), ((0, 0), (0, 0), (0, 8 - H_FOX), (0, 0)))

    e_mat = jnp.asarray(_head_match(D_GK, GLA_DK, D_GLA, GLA_DV), bf16)
    mask_t = jnp.asarray(_head_match(D_GLA, GLA_DV, D_GK, GLA_DK), f32)
    avg = jnp.asarray(_head_match(D_GLA, GLA_DV, D_GLA, GLA_DV), bf16)
    zeros_state = jnp.zeros((1, D_STATE), f32)

    outs_p, outs_s = [], []
    for l in range(depth):
        w_p = _pack_w_in(w_in[l])
        bfp = jnp.pad(fox_bf[l], (0, LANES - H_FOX)).reshape(1, LANES)
        wa2p = jnp.pad(gla_wa2[l], ((TAIL_GA, LANES - TAIL_GA - GLA_RANK), (0, 0))).astype(bf16)
        ba = gla_ba[l].reshape(1, D_GK)
        a_re = ssm_a_re[l].reshape(1, D_STATE)
        a_im = ssm_a_im[l].reshape(1, D_STATE)
        ldt = jnp.repeat(ssm_log_dt[l], SSM_P).reshape(1, D_STATE)
        bt_re = jnp.transpose(ssm_b_re[l], (2, 0, 1)).reshape(SSM_GC, D_STATE)
        bt_im = jnp.transpose(ssm_b_im[l], (2, 0, 1)).reshape(SSM_GC, D_STATE)
        p_re, p_im, bb_re, bb_im = _ssm_prep(a_re, a_im, ldt, bt_re, bt_im, tdec)
        to_blocks = lambda bb: jnp.transpose(bb.reshape(SSM_GC, SSM_G, SSM_P), (1, 0, 2))
        wb = jnp.concatenate([_block_diag(to_blocks(bb_re)), _block_diag(to_blocks(bb_im))], axis=1).astype(bf16)
        c_blocks = lambda cc: jnp.transpose(cc, (0, 2, 1))
        wc = jnp.concatenate([_block_diag(c_blocks(ssm_c_re[l])), -_block_diag(c_blocks(ssm_c_im[l]))],
                             axis=0).astype(bf16)
        dvec = ssm_d[l].reshape(1, D_SSM)
        wglu = ssm_w_glu[l].astype(bf16)
        bglu = ssm_b_glu[l].reshape(1, D_SSM)
        wqt = jnp.transpose(peer_wq[l].reshape(dm, PEER_HEADS * PEER_DK)).astype(bf16)
        keys = peer_keys[l].astype(bf16)
        u_tab = peer_u[l].astype(bf16)
        v_tab = peer_v[l].astype(bf16)

        (fq, qaug, kaug, fvb, fk, fv, lf, gq, gk, gv, la, gr, su) = _inproj(
            x, g_mix[l].reshape(1, dm), w_p, bfp, wa2p, ba)

        fox_p = _fox_prompt(qaug, kaug, fvb, seq)
        lfn_t = jnp.pad(jnp.swapaxes(lf[seq:n_tok, :8].reshape(bd, tdec, 8), 1, 2), ((0, 0), (0, 0), (0, LANES - tdec)))
        fox_s = _fox_sample(fq[seq:n_tok].astype(f32), fk[seq:n_tok], fv[seq:n_tok], lfn_t, ck, cv, clft, page_table, l, tdec)
        fox_o = jnp.concatenate([fox_p, fox_s, pad_rows(fox_s)], axis=0)

        gla_p, st_p = _gla(gq, gk, la, gv, jnp.zeros((1, D_GLA, D_GK), f32), e_mat, mask_t,
                           row0=0, nseq=1, steps=seq // GLA_STEP, rows_per_step=GLA_STEP, chunk=GLA_CHUNK)
        s0 = state_gla[l].astype(f32)
        s0t = jnp.transpose(s0, (0, 1, 3, 2))
        s0t = (s0t[:, :, :, None, :] * jnp.eye(H_GLA, dtype=f32)[None, :, None, :, None]).reshape(bd, D_GLA, D_GK)
        gla_s, st_s = _gla(gq, gk, la, gv, s0t, e_mat, mask_t,
                           row0=seq, nseq=bd, steps=1, rows_per_step=tdec, chunk=tdec)
        gla_o = jnp.concatenate([gla_p, gla_s, pad_rows(gla_s)], axis=0)

        def unpack_state(st):
            st = st.reshape(-1, H_GLA, GLA_DV, H_GLA, GLA_DK)
            st = jnp.stack([st[:, h, :, h, :] for h in range(H_GLA)], axis=1)
            return jnp.transpose(st, (0, 1, 3, 2))

        ssm_p, hre_p, him_p = _ssm(su, wb, p_re, p_im, zeros_state, zeros_state, wc, dvec, wglu, bglu,
                                   row0=0, n_rows=seq, tile_carry=True)
        h0r = jnp.repeat(state_ssm_re[l].reshape(bd, D_STATE).astype(f32), tdec, axis=0)
        h0i = jnp.repeat(state_ssm_im[l].reshape(bd, D_STATE).astype(f32), tdec, axis=0)
        ssm_s, hre_s, him_s = _ssm(su, wb, p_re, p_im, h0r, h0i, wc, dvec, wglu, bglu,
                                   row0=seq, n_rows=n_s, tile_carry=False)
        ssm_o = jnp.concatenate([ssm_p, ssm_s, pad_rows(ssm_s)], axis=0)

        x1, xf = _outproj(fox_o, gla_o, gr, ssm_o, x, fox_norm[l].reshape(1, D_FOX),
                          jnp.tile(gla_norm[l], H_GLA).reshape(1, D_GLA), avg,
                          w_out[l].astype(bf16), g_ffn[l].reshape(1, dm))
        i1, i2, gate = _peer_topk(xf, wqt, keys)
        x = _peer_dense(xf, i1, i2, gate, u_tab, v_tab, x1, g_final.reshape(1, dm), l == depth - 1)

        outs_p.append((fk[:seq].reshape(bp, seq, H_FOX, FOX_DH), fv[:seq].reshape(bp, seq, H_FOX, FOX_DH),
                       lf[:seq, :H_FOX].reshape(bp, seq, H_FOX), unpack_state(st_p),
                       hre_p.reshape(bp, SSM_G, SSM_P), him_p.reshape(bp, SSM_G, SSM_P)))
        outs_s.append((fk[seq:n_tok].reshape(bd, tdec, H_FOX, FOX_DH), fv[seq:n_tok].reshape(bd, tdec, H_FOX, FOX_DH),
                       lf[seq:n_tok, :H_FOX].reshape(bd, tdec, H_FOX), unpack_state(st_s),
                       hre_s[tdec - 1::tdec].reshape(bd, SSM_G, SSM_P), him_s[tdec - 1::tdec].reshape(bd, SSM_G, SSM_P)))

    p_stk = [jnp.stack(t) for t in zip(*outs_p)]
    s_stk = [jnp.stack(t) for t in zip(*outs_s)]
    y_prompt = x[:seq].reshape(bp, seq, dm)
    y_sample = x[seq:n_tok].reshape(bd, tdec, dm)
    return (y_prompt, y_sample, *p_stk, *s_stk)
```

```python
import functools

import numpy as np
import jax
import jax.numpy as jnp
from jax import lax
from jax.experimental import pallas as pl
from jax.experimental.pallas import tpu as pltpu

f32 = jnp.float32
bf16 = jnp.bfloat16
i32 = jnp.int32

EPS = 1e-6
LOG2E = 1.4426950408889634
LANES = 128
VMEM_LIMIT = 56 * 1024 * 1024

H_FOX, FOX_DH = 6, 64
D_FOX = H_FOX * FOX_DH
H_GLA, GLA_DK, GLA_DV = 6, 32, 64
D_GK = H_GLA * GLA_DK
D_GLA = H_GLA * GLA_DV
GLA_RANK = 16
GLA_TAU = 16.0
SSM_G, SSM_GC, SSM_P = 16, 16, 64
D_SSM = SSM_G * SSM_GC
D_STATE = SSM_G * SSM_P
PEER_KEYS, PEER_HEADS, PEER_TOPK, PEER_DK = 128, 8, 16, 256
PEER_SLOTS = PEER_HEADS * PEER_TOPK

C_FQ, C_FK, C_FV = 0, 384, 768
C_GQ, C_GK = 1152, 1408
C_GV, C_GR, C_SU, C_TAIL = 1664, 2048, 2432, 2688
D_INP = 2816
TAIL_FF, TAIL_GA = 0, 8

TOK_TILE = 256
FOX_TILE = 1024
FOX_PAGES = 32
GLA_STEP = 512
GLA_CHUNK = 16
PEER_TOK = 512
PEER_ETILE = 1024
GATE_GROUP = 16
GLA_UNROLL = 4


def _dot(a, b):
    return jnp.dot(a, b, preferred_element_type=f32)


def _dot_nt(a, b):
    return lax.dot_general(a, b, (((1,), (1,)), ((), ())), preferred_element_type=f32)


def _dot_tn(a, b):
    return lax.dot_general(a, b, (((0,), (0,)), ((), ())), preferred_element_type=f32)


def _split3(x):
    h = x.astype(bf16)
    r = x - h.astype(f32)
    m = r.astype(bf16)
    l = (r - m.astype(f32)).astype(bf16)
    return h, m, l


def _dot3_lhs01(sel, x):
    h, m, l = _split3(x)
    return _dot(sel, h) + _dot(sel, m) + _dot(sel, l)


def _dot3_rhs01(x, sel):
    h, m, l = _split3(x)
    return _dot(h, sel) + _dot(m, sel) + _dot(l, sel)


def _log_sigmoid(x):
    return -(jnp.maximum(-x, 0.0) + jnp.log1p(jnp.exp(-jnp.abs(x))))


def _iota(shape, dim):
    return lax.broadcasted_iota(i32, shape, dim)


def _div_pow2(x, n):
    assert n & (n - 1) == 0
    return jnp.right_shift(x, n.bit_length() - 1)


def _mod_pow2(x, n):
    assert n & (n - 1) == 0
    return jnp.bitwise_and(x, n - 1)


def _params(sem, vmem=None):
    return pltpu.CompilerParams(dimension_semantics=sem, vmem_limit_bytes=vmem)


def _fox_bias_lanes():
    place_q = np.zeros((3, LANES, H_FOX * LANES), np.float32)
    place_k = np.zeros((3, LANES, H_FOX * LANES), np.float32)
    ones_q = np.zeros((1, H_FOX * LANES), np.float32)
    ones_k = np.zeros((1, H_FOX * LANES), np.float32)
    for j in range(H_FOX):
        base = j * LANES + (FOX_DH if j % 2 == 0 else 0)
        for p in range(3):
            place_q[p, j, base + p] = 1.0
            ones_k[0, base + p] = 1.0
            ones_q[0, base + 3 + p] = 1.0
            place_k[p, j, base + 3 + p] = 1.0
    return place_q, place_k, ones_q, ones_k


def _inproj_body(x_ref, g_ref, w_ref, bfp_ref, wa2_ref, ba_ref, pq_ref, pk_ref, oq_ref, ok_ref,
                 fq_ref, qa_ref, ka_ref, fvb_ref, fk_ref, fv_ref, lf_ref,
                 gq_ref, gk_ref, gv_ref, la_ref, gr_ref, su_ref, carry_ref):
    @pl.when(pl.program_id(0) == 0)
    def _init():
        carry_ref[...] = jnp.zeros_like(carry_ref)

    x = x_ref[...]
    tt = x.shape[0]
    xn = x * lax.rsqrt(jnp.mean(x * x, axis=-1, keepdims=True) + EPS) * g_ref[...]
    z = _dot(xn.astype(bf16), w_ref[...])
    fq_ref[...] = (z[:, C_FQ:C_FQ + D_FOX] * (FOX_DH ** -0.5)).astype(bf16)
    fk = z[:, C_FK:C_FK + D_FOX]
    fv = z[:, C_FV:C_FV + D_FOX]
    fk_ref[...] = fk
    fv_ref[...] = fv
    fvb_ref[...] = fv.astype(bf16)
    gq_ref[...] = z[:, C_GQ:C_GQ + D_GK] * (GLA_DK ** -0.5)
    gk_ref[...] = z[:, C_GK:C_GK + D_GK]
    gv_ref[...] = z[:, C_GV:C_GV + D_GLA]
    gr_ref[...] = z[:, C_GR:C_GR + D_GLA]
    su_ref[...] = z[:, C_SU:C_SU + D_SSM]
    tail = z[:, C_TAIL:C_TAIL + LANES]
    lane = _iota((tt, LANES), 1)
    lf = jnp.where(lane < H_FOX, _log_sigmoid(tail + bfp_ref[...]), 0.0)
    lf_ref[...] = lf
    tri = jnp.where(_iota((tt, tt), 0) >= _iota((tt, tt), 1), 1.0, 0.0).astype(bf16)
    cs = _dot3_lhs01(tri, lf) + carry_ref[...]
    carry_ref[...] = cs[tt - 1:tt, :]
    pieces = _split3(cs * LOG2E)
    q_bias = sum(_dot(pc, pq_ref[i]) for i, pc in enumerate(pieces)) + oq_ref[...]
    k_bias = ok_ref[...] - sum(_dot(pc, pk_ref[i]) for i, pc in enumerate(pieces))
    low = lane < FOX_DH
    for hp in range(H_FOX // 2):
        qt = z[:, C_FQ + hp * LANES:C_FQ + (hp + 1) * LANES] * (LOG2E * FOX_DH ** -0.5)
        kt = z[:, C_FK + hp * LANES:C_FK + (hp + 1) * LANES]
        for hh in range(2):
            blk = slice((2 * hp + hh) * LANES, (2 * hp + hh + 1) * LANES)
            real = low if hh == 0 else jnp.logical_not(low)
            qa_ref[:, blk] = jnp.where(real, qt, q_bias[:, blk]).astype(bf16)
            ka_ref[:, blk] = jnp.where(real, kt, k_bias[:, blk]).astype(bf16)
    za = _dot(tail.astype(bf16), wa2_ref[...]) + ba_ref[...]
    la_ref[...] = _log_sigmoid(za) * (1.0 / GLA_TAU)


def _inproj(x, g_mix, w_p, bfp, wa2p, ba):
    t = x.shape[0]
    tt = TOK_TILE
    row = lambda n: pl.BlockSpec((tt, n), lambda i: (i, 0))
    const = lambda a: pl.BlockSpec(a.shape, lambda i: (0,) * a.ndim)
    outs = [
        (D_FOX, bf16), (H_FOX * LANES, bf16), (H_FOX * LANES, bf16), (D_FOX, bf16), (D_FOX, f32), (D_FOX, f32),
        (LANES, f32),
        (D_GK, f32), (D_GK, f32), (D_GLA, f32), (D_GK, f32), (D_GLA, f32), (D_SSM, f32),
    ]
    place_q, place_k, ones_q, ones_k = _fox_bias_lanes()
    consts = [g_mix, w_p, bfp, wa2p, ba, jnp.asarray(place_q, bf16), jnp.asarray(place_k, bf16),
              jnp.asarray(ones_q), jnp.asarray(ones_k)]
    return pl.pallas_call(
        _inproj_body,
        grid=(t // tt,),
        in_specs=[row(x.shape[1])] + [const(a) for a in consts],
        out_specs=[row(n) for n, _ in outs],
        out_shape=[jax.ShapeDtypeStruct((t, n), d) for n, d in outs],
        scratch_shapes=[pltpu.VMEM((1, LANES), f32)],
        compiler_params=_params(("arbitrary",), VMEM_LIMIT),
        name="inproj",
    )(x, *consts)


def _fox_prompt_body(qi_ref, ki_ref, qa_ref, qb_ref, ka_ref, kb_ref, v_ref, o_ref,
                     m_ref, l_ref, acc_ref):
    p = pl.program_id(1)
    qi = qi_ref[p]
    ki = ki_ref[p]
    tq, tk = qa_ref.shape[0], ka_ref.shape[0]

    @pl.when(ki == 0)
    def _init():
        m_ref[...] = jnp.full_like(m_ref, -jnp.inf)
        l_ref[...] = jnp.zeros_like(l_ref)
        acc_ref[...] = jnp.zeros_like(acc_ref)

    def step(diagonal):
        vext = jnp.concatenate([v_ref[...], jnp.ones((tk, LANES), bf16)], axis=1)
        scores = [_dot_nt(qa_ref[...], ka_ref[...]), _dot_nt(qb_ref[...], kb_ref[...])]
        for hh in range(2):
            s = scores[hh]
            if diagonal:
                s = jnp.where(_iota((tq, tk), 0) >= _iota((tq, tk), 1), s, -jnp.inf)
            m_old = m_ref[hh]
            m_new = jnp.maximum(m_old, jnp.max(s, axis=-1, keepdims=True))
            pr = jnp.exp2(s - jnp.tile(m_new, (1, tk // LANES)))
            alpha = jnp.exp2(m_old - m_new)
            pv = _dot(pr.astype(bf16), vext)
            l_ref[hh] = alpha * l_ref[hh] + pv[:, LANES:]
            m_ref[hh] = m_new
            acc_ref[hh] = alpha * acc_ref[hh] + pv[:, :LANES]

    @pl.when(ki < qi)
    def _off():
        step(False)

    @pl.when(ki == qi)
    def _diag():
        step(True)
        lane = _iota((1, LANES), 1)
        o_ref[...] = jnp.where(lane < FOX_DH, acc_ref[0] / l_ref[0], acc_ref[1] / l_ref[1])


def _fox_prompt(qaug, kaug, fvb, seq):
    tq = FOX_TILE
    nq = seq // tq
    qi_tbl = np.concatenate([np.full(i + 1, i) for i in range(nq)]).astype(np.int32)
    ki_tbl = np.concatenate([np.arange(i + 1) for i in range(nq)]).astype(np.int32)
    grid_spec = pltpu.PrefetchScalarGridSpec(
        num_scalar_prefetch=2,
        grid=(H_FOX // 2, len(qi_tbl)),
        in_specs=[
            pl.BlockSpec((tq, LANES), lambda h, p, qi, ki: (qi[p], 2 * h)),
            pl.BlockSpec((tq, LANES), lambda h, p, qi, ki: (qi[p], 2 * h + 1)),
            pl.BlockSpec((tq, LANES), lambda h, p, qi, ki: (ki[p], 2 * h)),
            pl.BlockSpec((tq, LANES), lambda h, p, qi, ki: (ki[p], 2 * h + 1)),
            pl.BlockSpec((tq, LANES), lambda h, p, qi, ki: (ki[p], h)),
        ],
        out_specs=pl.BlockSpec((tq, LANES), lambda h, p, qi, ki: (qi[p], h)),
        scratch_shapes=[pltpu.VMEM((2, tq, LANES), f32), pltpu.VMEM((2, tq, LANES), f32),
                        pltpu.VMEM((2, tq, LANES), f32)],
    )
    return pl.pallas_call(
        _fox_prompt_body,
        grid_spec=grid_spec,
        out_shape=jax.ShapeDtypeStruct((seq, D_FOX), f32),
        compiler_params=_params(("arbitrary", "arbitrary"), VMEM_LIMIT),
        name="fox_prompt",
    )(jnp.asarray(qi_tbl), jnp.asarray(ki_tbl), qaug, qaug, kaug, kaug, fvb)


def _expand_heads(x, rows):
    return jnp.concatenate(
        [jnp.broadcast_to(x[h:h + 1, :], (rows, x.shape[1])) for h in range(H_FOX)], axis=0)


def _fox_sample_body(pt_ref, q_ref, kn_ref, vn_ref, lfn_ref, *rest):
    del pt_ref
    npg = FOX_PAGES
    kp_refs = rest[0:npg]
    vp_refs = rest[npg:2 * npg]
    lfp_refs = rest[2 * npg:3 * npg]
    o_ref = rest[3 * npg]
    m_ref, l_ref, acc_ref, car_ref, qbd_ref, cq_ref = rest[3 * npg + 1:]
    j = pl.program_id(1)
    t = q_ref.shape[0]
    hq = H_FOX * t
    page = kp_refs[0].shape[-1]

    def update(s, weigh):
        m_old = m_ref[...]
        m_new = jnp.maximum(m_old, jnp.max(s, axis=-1, keepdims=True))
        pr = jnp.exp(s - m_new)
        alpha = jnp.exp(m_old - m_new)
        l_ref[...] = alpha * l_ref[...] + jnp.sum(pr, axis=-1, keepdims=True)
        m_ref[...] = m_new
        acc_ref[...] = alpha * acc_ref[...] + weigh(pr.astype(bf16))

    @pl.when(j == 0)
    def _new_tokens():
        m_ref[...] = jnp.full_like(m_ref, -jnp.inf)
        l_ref[...] = jnp.zeros_like(l_ref)
        acc_ref[...] = jnp.zeros_like(acc_ref)
        car_ref[...] = jnp.zeros_like(car_ref)
        q = q_ref[...]
        q6 = jnp.concatenate([q] * H_FOX, axis=0)
        keep = _div_pow2(_iota((hq, D_FOX), 0), t) == _div_pow2(_iota((hq, D_FOX), 1), FOX_DH)
        qbd_ref[...] = jnp.where(keep, q6, 0.0).astype(bf16)
        cum = lfn_ref[...]
        lane8 = _iota(cum.shape, 1)
        sh = 1
        while sh < t:
            cum = cum + jnp.where(lane8 >= sh, pltpu.roll(cum, sh, 1), 0.0)
            sh *= 2
        cum_rows = _expand_heads(cum, t)
        qpos = _mod_pow2(_iota((hq, LANES), 0), t)
        kpos = _iota((hq, LANES), 1)
        cq = jnp.sum(jnp.where(kpos == qpos, cum_rows, 0.0), axis=-1, keepdims=True)
        cq_ref[...] = cq
        zpad = jnp.zeros((LANES - t, D_FOX), f32)
        kn = jnp.concatenate([kn_ref[...], zpad], axis=0).astype(bf16)
        vn = jnp.concatenate([vn_ref[...], zpad], axis=0).astype(bf16)
        s = _dot_nt(qbd_ref[...], kn) + cq - cum_rows
        s = jnp.where(kpos <= qpos, s, -jnp.inf)
        update(s, lambda pr: _dot(pr, vn))

    upper = jnp.where(_iota((page, page), 0) > _iota((page, page), 1), 1.0, 0.0).astype(bf16)
    qbd = qbd_ref[...]
    qk = [_dot(qbd, kp_refs[g][...].reshape(D_FOX, page).astype(bf16)) for g in range(npg)]
    lf_all = jnp.concatenate([lfp_refs[g][...] for g in range(npg)], axis=0)
    suf_all = _dot3_rhs01(lf_all, upper)
    tot_all = jnp.sum(lf_all, axis=-1, keepdims=True)
    car = car_ref[...]
    scores = []
    for g in range(npg):
        scores.append(qk[g] + _expand_heads(suf_all[g * 8:(g + 1) * 8, :] + car, t))
        car = car + tot_all[g * 8:(g + 1) * 8, :]
    car_ref[...] = car

    def weigh_pages(pr):
        out = None
        for g in range(npg):
            vpt = vp_refs[g][...].reshape(D_FOX, page).astype(bf16)
            part = _dot_nt(pr[:, g * page:(g + 1) * page], vpt)
            out = part if out is None else out + part
        return out

    update(jnp.concatenate(scores, axis=1) + cq_ref[...], weigh_pages)

    @pl.when(j == pl.num_programs(1) - 1)
    def _finish():
        o48 = acc_ref[...] / l_ref[...]
        head = _div_pow2(_iota((t, D_FOX), 1), FOX_DH)
        out = jnp.zeros((t, D_FOX), f32)
        for h in range(H_FOX):
            out = out + jnp.where(head == h, o48[h * t:(h + 1) * t, :], 0.0)
        o_ref[...] = out


def _fox_sample(q_s, k_s, v_s, lfn_t, cache_k, cache_v, cache_lft, page_table, layer, t):
    nb, n_pages = page_table.shape
    page = cache_k.shape[-1]
    npg = FOX_PAGES
    hq = H_FOX * t
    seq_spec = pl.BlockSpec((t, D_FOX), lambda b, j, pt: (b, 0))

    def page_spec(shape, g):
        zeros = (0,) * len(shape)
        return pl.BlockSpec((None, None) + shape,
                            lambda b, j, pt: (layer, pt[b, n_pages - 1 - (j * npg + g)]) + zeros)

    grid_spec = pltpu.PrefetchScalarGridSpec(
        num_scalar_prefetch=1,
        grid=(nb, n_pages // npg),
        in_specs=[seq_spec, seq_spec, seq_spec,
                  pl.BlockSpec((None, 8, LANES), lambda b, j, pt: (b, 0, 0))]
        + [page_spec((H_FOX, FOX_DH, page), g) for g in range(npg)]
        + [page_spec((H_FOX, FOX_DH, page), g) for g in range(npg)]
        + [page_spec((8, page), g) for g in range(npg)],
        out_specs=seq_spec,
        scratch_shapes=[pltpu.VMEM((hq, 1), f32), pltpu.VMEM((hq, 1), f32),
                        pltpu.VMEM((hq, D_FOX), f32), pltpu.VMEM((8, 1), f32),
                        pltpu.VMEM((hq, D_FOX), bf16), pltpu.VMEM((hq, 1), f32)],
    )
    return pl.pallas_call(
        _fox_sample_body,
        grid_spec=grid_spec,
        out_shape=jax.ShapeDtypeStruct((nb * t, D_FOX), f32),
        compiler_params=_params(("arbitrary", "arbitrary"), VMEM_LIMIT),
        name="fox_sample",
    )(page_table, q_s, k_s, v_s, lfn_t, *([cache_k] * npg), *([cache_v] * npg), *([cache_lft] * npg))


def _gla_body(q_ref, k_ref, la_ref, v_ref, s0_ref, e_ref, mask_ref, o_ref, sout_ref, st_ref, *, chunk):
    i = pl.program_id(1)

    @pl.when(i == 0)
    def _init():
        st_ref[...] = s0_ref[...]

    c = chunk
    n_chunks = q_ref.shape[0] // c
    row = _iota((c, D_GK), 0)

    def one_chunk(ci, carry):
        r0 = pl.multiple_of(ci * c, c)
        q = q_ref[pl.ds(r0, c), :]
        k = k_ref[pl.ds(r0, c), :]
        v = v_ref[pl.ds(r0, c), :]
        b = la_ref[pl.ds(r0, c), :]
        sh = 1
        while sh < c:
            b = b + jnp.where(row >= sh, pltpu.roll(b, sh, 0), 0.0)
            sh *= 2
        pieces = []
        for s in range(c):
            d = jnp.where(row >= s, b - b[s:s + 1, :], -jnp.inf)
            pieces.append(q * k[s:s + 1, :] * jnp.exp(d))
        pst = jnp.concatenate(pieces, axis=0)
        hi = pst.astype(bf16)
        lo = (pst - hi.astype(f32)).astype(bf16)
        att = _dot(hi, e_ref[...]) + _dot(lo, e_ref[...])
        b_last = b[c - 1:c, :]
        kd = (k * jnp.exp(b_last - b)).astype(bf16)
        kv = _dot_tn(v.astype(bf16), kd) * mask_ref[...]
        qe = (q * jnp.exp(b)).astype(bf16)
        o = att[0:c, :] * v[0:1, :]
        for s in range(1, c):
            o = o + att[s * c:(s + 1) * c, :] * v[s:s + 1, :]
        st = st_ref[...]
        o_ref[pl.ds(r0, c), :] = o + _dot_nt(qe, st.astype(bf16))
        st_ref[...] = st * jnp.exp(b_last) + kv
        return carry

    lax.fori_loop(0, n_chunks, one_chunk, 0, unroll=min(GLA_UNROLL, n_chunks))

    @pl.when(i == pl.num_programs(1) - 1)
    def _finish():
        sout_ref[...] = st_ref[...]


def _gla(gq, gk, la, gv, s0t, e_mat, mask_t, *, row0, nseq, steps, rows_per_step, chunk):
    blk0 = row0 // rows_per_step
    rows = lambda n: pl.BlockSpec((rows_per_step, n), lambda s, i: (blk0 + s * steps + i, 0))
    orow = pl.BlockSpec((rows_per_step, D_GLA), lambda s, i: (s * steps + i, 0))
    st_spec = pl.BlockSpec((None, D_GLA, D_GK), lambda s, i: (s, 0, 0))
    const = lambda a: pl.BlockSpec(a.shape, lambda s, i: (0,) * a.ndim)
    return pl.pallas_call(
        functools.partial(_gla_body, chunk=chunk),
        grid=(nseq, steps),
        in_specs=[rows(D_GK), rows(D_GK), rows(D_GK), rows(D_GLA), st_spec, const(e_mat), const(mask_t)],
        out_specs=[orow, st_spec],
        out_shape=[jax.ShapeDtypeStruct((nseq * steps * rows_per_step, D_GLA), f32),
                   jax.ShapeDtypeStruct((nseq, D_GLA, D_GK), f32)],
        scratch_shapes=[pltpu.VMEM((D_GLA, D_GK), f32)],
        compiler_params=_params(("arbitrary", "arbitrary"), VMEM_LIMIT),
        name="gla",
    )(gq, gk, la, gv, s0t, e_mat, mask_t)


def _ssm_prep_body(are_ref, aim_ref, ldt_ref, bre_ref, bim_ref,
                   pre_ref, pim_ref, bbre_ref, bbim_ref):
    a_re = are_ref[...]
    a_im = aim_ref[...]
    dt = jnp.exp(ldt_ref[...])
    mag = jnp.exp(a_re * dt)
    abar_re = mag * jnp.cos(a_im * dt)
    abar_im = mag * jnp.sin(a_im * dt)
    den = a_re * a_re + a_im * a_im
    nr, ni = abar_re - 1.0, abar_im
    fr = (nr * a_re + ni * a_im) / den
    fi = (ni * a_re - nr * a_im) / den
    bbre_ref[...] = fr * bre_ref[...] - fi * bim_ref[...]
    bbim_ref[...] = fr * bim_ref[...] + fi * bre_ref[...]
    n = pre_ref.shape[0]
    pre_ref[0:1, :] = abar_re
    pim_ref[0:1, :] = abar_im
    m = 1
    while m < n:
        tr = pre_ref[m - 1:m, :]
        ti = pim_ref[m - 1:m, :]
        xr = pre_ref[0:m, :]
        xi = pim_ref[0:m, :]
        pre_ref[m:2 * m, :] = xr * tr - xi * ti
        pim_ref[m:2 * m, :] = xr * ti + xi * tr
        m *= 2


def _ssm_prep(a_re, a_im, log_dt, bt_re, bt_im, n_pow):
    outs = [(n_pow, D_STATE), (n_pow, D_STATE), (SSM_GC, D_STATE), (SSM_GC, D_STATE)]
    return pl.pallas_call(
        _ssm_prep_body,
        out_shape=[jax.ShapeDtypeStruct(s, f32) for s in outs],
        name="ssm_prep",
    )(a_re, a_im, log_dt, bt_re, bt_im)


def _ssm_body(u_ref, wb_ref, pre_ref, pim_ref, hin_re_ref, hin_im_ref, wc_ref, d_ref, wg_ref, bg_ref,
              o_ref, hre_ref, him_ref, car_re_ref, car_im_ref, *, tile_carry):
    i = pl.program_id(0)
    u = u_ref[...]
    tt = u.shape[0]
    grp = pre_ref.shape[0]
    d = _dot(u.astype(bf16), wb_ref[...])
    xr = d[:, :D_STATE]
    xi = d[:, D_STATE:]
    pos = _mod_pow2(_iota((tt, D_STATE), 0), grp)
    sh = 1
    while sh < grp:
        ar = pre_ref[sh - 1:sh, :]
        ai = pim_ref[sh - 1:sh, :]
        sr = jnp.where(pos >= sh, pltpu.roll(xr, sh, 0), 0.0)
        si = jnp.where(pos >= sh, pltpu.roll(xi, sh, 0), 0.0)
        xr, xi = xr + ar * sr - ai * si, xi + ar * si + ai * sr
        sh *= 2
    pr = pre_ref[...]
    pi = pim_ref[...]
    if tile_carry:
        @pl.when(i == 0)
        def _init():
            car_re_ref[...] = hin_re_ref[...]
            car_im_ref[...] = hin_im_ref[...]
        cr = car_re_ref[...]
        ci = car_im_ref[...]
        hr_groups, hi_groups = [], []
        for g in range(tt // grp):
            gr = xr[g * grp:(g + 1) * grp, :] + pr * cr - pi * ci
            gi = xi[g * grp:(g + 1) * grp, :] + pr * ci + pi * cr
            cr, ci = gr[grp - 1:grp, :], gi[grp - 1:grp, :]
            hr_groups.append(gr)
            hi_groups.append(gi)
        hr = jnp.concatenate(hr_groups, axis=0)
        hi = jnp.concatenate(hi_groups, axis=0)
        car_re_ref[...] = cr
        car_im_ref[...] = ci
        hre_ref[...] = cr
        him_ref[...] = ci
    else:
        h0r = hin_re_ref[...]
        h0i = hin_im_ref[...]
        prt = jnp.tile(pr, (tt // grp, 1))
        pit = jnp.tile(pi, (tt // grp, 1))
        hr = xr + prt * h0r - pit * h0i
        hi = xi + prt * h0i + pit * h0r
        hre_ref[...] = hr
        him_ref[...] = hi
    hcat = jnp.concatenate([hr, hi], axis=1).astype(bf16)
    y = _dot(hcat, wc_ref[...]) + d_ref[...] * u
    zg = jax.nn.gelu(y)
    o_ref[...] = (zg * jax.nn.sigmoid(_dot(zg.astype(bf16), wg_ref[...]) + bg_ref[...])).astype(bf16)


def _ssm(su, wb, p_re, p_im, hin_re, hin_im, wc, dvec, wglu, bglu, *, row0, n_rows, tile_carry):
    tt = TOK_TILE
    blk0 = row0 // tt
    const = lambda a: pl.BlockSpec(a.shape, lambda i: (0,) * a.ndim)
    if tile_carry:
        h_spec = const(hin_re)
        hout_spec = pl.BlockSpec((1, D_STATE), lambda i: (0, 0))
        hout_shape = jax.ShapeDtypeStruct((1, D_STATE), f32)
    else:
        h_spec = pl.BlockSpec((tt, D_STATE), lambda i: (i, 0))
        hout_spec = h_spec
        hout_shape = jax.ShapeDtypeStruct((n_rows, D_STATE), f32)
    return pl.pallas_call(
        functools.partial(_ssm_body, tile_carry=tile_carry),
        grid=(n_rows // tt,),
        in_specs=[pl.BlockSpec((tt, D_SSM), lambda i: (blk0 + i, 0)), const(wb), const(p_re), const(p_im),
                  h_spec, h_spec, const(wc), const(dvec), const(wglu), const(bglu)],
        out_specs=[pl.BlockSpec((tt, D_SSM), lambda i: (i, 0)), hout_spec, hout_spec],
        out_shape=[jax.ShapeDtypeStruct((n_rows, D_SSM), bf16), hout_shape, hout_shape],
        scratch_shapes=[pltpu.VMEM((1, D_STATE), f32), pltpu.VMEM((1, D_STATE), f32)],
        compiler_params=_params(("arbitrary",), VMEM_LIMIT),
        name="ssm",
    )(su, wb, p_re, p_im, hin_re, hin_im, wc, dvec, wglu, bglu)


def _outproj_body(fox_ref, gla_ref, gr_ref, ssm_ref, x_ref, fn_ref, gn_ref, avg_ref, w_ref, gf_ref,
                  x1_ref, xf_ref):
    fo = fox_ref[...]
    fo = fo * lax.rsqrt(jnp.mean(fo * fo, axis=-1, keepdims=True) + EPS) * fn_ref[...]
    go = gla_ref[...]
    ms = _dot3_rhs01(go * go, avg_ref[...]) * (1.0 / GLA_DV)
    go = go * lax.rsqrt(ms + EPS) * gn_ref[...] * jax.nn.silu(gr_ref[...])
    w = w_ref
    y = (_dot(fo.astype(bf16), w[0:D_FOX, :]) + _dot(go.astype(bf16), w[D_FOX:D_FOX + D_GLA, :])
         + _dot(ssm_ref[...], w[D_FOX + D_GLA:, :]))
    x1 = x_ref[...] + y
    x1_ref[...] = x1
    xf = x1 * lax.rsqrt(jnp.mean(x1 * x1, axis=-1, keepdims=True) + EPS) * gf_ref[...]
    xf_ref[...] = xf.astype(bf16)


def _outproj(fox_o, gla_o, gr, ssm_o, x, fox_norm, gla_norm6, avg, w_out, g_ffn):
    t, dm = x.shape
    tt = TOK_TILE
    row = lambda n: pl.BlockSpec((tt, n), lambda i: (i, 0))
    const = lambda a: pl.BlockSpec(a.shape, lambda i: (0,) * a.ndim)
    return pl.pallas_call(
        _outproj_body,
        grid=(t // tt,),
        in_specs=[row(D_FOX), row(D_GLA), row(D_GLA), row(D_SSM), row(dm),
                  const(fox_norm), const(gla_norm6), const(avg), const(w_out), const(g_ffn)],
        out_specs=[row(dm), row(dm)],
        out_shape=[jax.ShapeDtypeStruct((t, dm), f32), jax.ShapeDtypeStruct((t, dm), bf16)],
        compiler_params=_params(("arbitrary",), VMEM_LIMIT),
        name="outproj",
    )(fox_o, gla_o, gr, ssm_o, x, fox_norm, gla_norm6, avg, w_out, g_ffn)


def _topk_rows(s, k, ids=None):
    it = _iota(s.shape, 0) if ids is None else ids
    big = jnp.iinfo(jnp.int32).max
    vals, idxs = [], []
    for _ in range(k):
        m = jnp.max(s, axis=0, keepdims=True)
        idx = jnp.min(jnp.where(s == m, it, big), axis=0, keepdims=True)
        vals.append(m)
        idxs.append(idx)
        s = jnp.where(it == idx, -jnp.inf, s)
    return jnp.concatenate(vals, axis=0), jnp.concatenate(idxs, axis=0)


def _pair_candidates(v1, v2, tt):
    kk = v1.shape[0]
    blocks, ids = [], []
    for a in range(kk):
        nb = kk // (a + 1)
        rows = -(-nb // 8) * 8
        blk = v1[a:a + 1, :] + v2[0:rows, :]
        b_id = _iota((rows, tt), 0)
        if nb < rows:
            blk = jnp.where(b_id < nb, blk, -jnp.inf)
        blocks.append(blk)
        ids.append(b_id + a * kk)
    return jnp.concatenate(blocks, axis=0), jnp.concatenate(ids, axis=0)


def _take_rows(table, sel):
    return jnp.sum(jnp.where(_iota(table.shape, 0) == sel, table, 0), axis=0, keepdims=True)


def _peer_select(qt, keys1, keys2):
    kk = PEER_TOPK
    half = PEER_DK // 2
    v1, i1 = _topk_rows(_dot(keys1, qt[:half, :].astype(bf16)), kk)
    v2, i2 = _topk_rows(_dot(keys2, qt[half:, :].astype(bf16)), kk)
    cand, cand_id = _pair_candidates(v1, v2, qt.shape[1])
    sc, pos = _topk_rows(cand, kk, cand_id)
    sel1 = jnp.concatenate([_take_rows(i1, _div_pow2(pos[j:j + 1, :], kk)) for j in range(kk)], axis=0)
    sel2 = jnp.concatenate([_take_rows(i2, _mod_pow2(pos[j:j + 1, :], kk)) for j in range(kk)], axis=0)
    ex = jnp.exp(sc - sc[0:1, :])
    return sel1, sel2, ex / jnp.sum(ex, axis=0, keepdims=True)


def _peer_topk_body(xf_ref, wqt_ref, keys_ref, i1_ref, i2_ref, g_ref):
    qt = _dot_nt(wqt_ref[...], xf_ref[...])
    picks = [_peer_select(qt[h * PEER_DK:(h + 1) * PEER_DK, :], keys_ref[h, 0], keys_ref[h, 1])
             for h in range(PEER_HEADS)]
    i1_ref[...] = jnp.concatenate([p[0] for p in picks], axis=0).T
    i2_ref[...] = jnp.concatenate([p[1] for p in picks], axis=0).T
    g_ref[...] = jnp.concatenate([p[2] for p in picks], axis=0).T


def _peer_topk(xf, wqt, keys):
    t, dm = xf.shape
    tt = TOK_TILE
    const = lambda a: pl.BlockSpec(a.shape, lambda i: (0,) * a.ndim)
    slot = pl.BlockSpec((tt, PEER_SLOTS), lambda i: (i, 0))
    return pl.pallas_call(
        _peer_topk_body,
        grid=(t // tt,),
        in_specs=[pl.BlockSpec((tt, dm), lambda i: (i, 0)), const(wqt), const(keys)],
        out_specs=[slot, slot, slot],
        out_shape=[jax.ShapeDtypeStruct((t, PEER_SLOTS), i32), jax.ShapeDtypeStruct((t, PEER_SLOTS), i32),
                   jax.ShapeDtypeStruct((t, PEER_SLOTS), f32)],
        compiler_params=_params(("arbitrary",), VMEM_LIMIT),
        name="peer_topk",
    )(xf, wqt, keys)


def _peer_dense_body(xf_ref, i1_ref, i2_ref, g_ref, u_ref, v_ref, x1_ref, gfin_ref, o_ref,
                     gate_ref, acc_ref, *, final_norm):
    e = pl.program_id(1)
    tt = xf_ref.shape[0]
    nk = PEER_KEYS
    rows_per_tile = u_ref.shape[0] // nk

    @pl.when(e == 0)
    def _build_gates():
        acc_ref[...] = jnp.zeros_like(acc_ref)
        key_id = _iota((nk, PEER_SLOTS), 0)

        def token_group(gi, carry):
            t0 = pl.multiple_of(gi * GATE_GROUP, GATE_GROUP)
            r1 = i1_ref[pl.ds(t0, GATE_GROUP), :]
            r2 = i2_ref[pl.ds(t0, GATE_GROUP), :]
            gr = g_ref[pl.ds(t0, GATE_GROUP), :]
            mats = []
            for j in range(GATE_GROUP):
                p1 = jnp.where(r1[j:j + 1, :] == key_id, gr[j:j + 1, :], 0.0).astype(bf16)
                p2 = jnp.where(r2[j:j + 1, :] == key_id, 1.0, 0.0).astype(bf16)
                mats.append(_dot_nt(p1, p2))
            gate_ref[:, pl.ds(t0, GATE_GROUP), :] = jnp.swapaxes(jnp.stack(mats, axis=0), 0, 1).astype(bf16)
            return carry

        lax.fori_loop(0, tt // GATE_GROUP, token_group, 0, unroll=2)

    a = _dot_nt(xf_ref[...], u_ref[...])
    act = jax.nn.gelu(a)
    ws = []
    for r in range(rows_per_tile):
        gt = gate_ref[e * rows_per_tile + r]
        ws.append((gt.astype(f32) * act[:, r * nk:(r + 1) * nk]).astype(bf16))
    acc_ref[...] += _dot(jnp.concatenate(ws, axis=1), v_ref[...])

    @pl.when(e == pl.num_programs(1) - 1)
    def _finish():
        y = x1_ref[...] + acc_ref[...]
        if final_norm:
            y = y * lax.rsqrt(jnp.mean(y * y, axis=-1, keepdims=True) + EPS) * gfin_ref[...]
        o_ref[...] = y


def _peer_dense(xf, i1, i2, g, u_tab, v_tab, layer, x1, g_final, final_norm):
    t, dm = xf.shape
    tt = PEER_TOK
    et = PEER_ETILE
    tok = lambda n: pl.BlockSpec((tt, n), lambda i, e: (i, 0))
    tab = pl.BlockSpec((None, et, dm), lambda i, e: (layer, e, 0))
    return pl.pallas_call(
        functools.partial(_peer_dense_body, final_norm=final_norm),
        grid=(t // tt, u_tab.shape[1] // et),
        in_specs=[tok(dm), tok(PEER_SLOTS), tok(PEER_SLOTS), tok(PEER_SLOTS), tab, tab, tok(dm),
                  pl.BlockSpec(g_final.shape, lambda i, e: (0, 0))],
        out_specs=tok(dm),
        out_shape=jax.ShapeDtypeStruct((t, dm), f32),
        scratch_shapes=[pltpu.VMEM((PEER_KEYS, tt, PEER_KEYS), bf16), pltpu.VMEM((tt, dm), f32)],
        compiler_params=_params(("arbitrary", "arbitrary"), VMEM_LIMIT),
        name="peer_dense",
    )(xf, i1, i2, g, u_tab, v_tab, x1, g_final)


def _pack_w_in(w_in):
    dm = w_in.shape[0]
    sizes = (D_FOX, D_FOX, D_FOX, H_FOX, D_GK, D_GK, D_GLA, GLA_RANK, D_GLA, D_SSM)
    pts = np.cumsum(sizes)[:-1].tolist()
    fq, fk, fv, ff, gq, gk, gv, ga, gr, su = jnp.split(w_in, pts, axis=1)
    z = lambda n: jnp.zeros((dm, n), w_in.dtype)
    tail = jnp.concatenate([ff, z(TAIL_GA - H_FOX), ga, z(LANES - TAIL_GA - GLA_RANK)], axis=1)
    packed = jnp.concatenate([fq, fk, fv, gq, z(C_GK - C_GQ - D_GK), gk, z(C_GV - C_GK - D_GK),
                              gv, gr, su, tail], axis=1)
    assert packed.shape[1] == D_INP
    return packed.astype(bf16)


def _block_diag(blocks):
    g, a, b = blocks.shape
    eye = jnp.eye(g, dtype=blocks.dtype)
    return (blocks[:, :, None, :] * eye[:, None, :, None]).reshape(g * a, g * b)


def _head_match(rows, row_w, cols, col_w):
    return (np.arange(rows)[:, None] // row_w == np.arange(cols)[None, :] // col_w)


def kernel(x_prompt, x_sample, cache_fox_k, cache_fox_v, cache_fox_logf, state_gla, state_ssm_re, state_ssm_im, page_table, g_mix, w_in, fox_bf, fox_norm, gla_wa2, gla_ba, gla_norm, ssm_a_re, ssm_a_im, ssm_log_dt, ssm_b_re, ssm_b_im, ssm_c_re, ssm_c_im, ssm_d, ssm_w_glu, ssm_b_glu, w_out, g_ffn, peer_wq, peer_keys, peer_u, peer_v, g_final):
    bp, seq, dm = x_prompt.shape
    bd, tdec, _ = x_sample.shape
    depth = w_in.shape[0]
    assert bp == 1 and seq % FOX_TILE == 0 and seq % GLA_STEP == 0
    n_s = bd * tdec
    assert n_s == TOK_TILE and tdec == 8
    n_tok = seq + n_s
    n_pool, page = cache_fox_k.shape[1], cache_fox_k.shape[2]
    assert page == LANES and page_table.shape[1] % FOX_PAGES == 0

    n_pad = -n_tok % PEER_TOK
    pad_rows = lambda a: jnp.zeros((n_pad, a.shape[1]), a.dtype)
    x = jnp.concatenate([x_prompt.reshape(seq, dm), x_sample.reshape(n_s, dm), jnp.zeros((n_pad, dm), f32)], axis=0)
    ck = jnp.transpose(cache_fox_k, (0, 1, 3, 4, 2))
    cv = jnp.transpose(cache_fox_v, (0, 1, 3, 4, 2))
    clft = jnp.pad(jnp.swapaxes(cache_fox_logf, 2, 3), ((0, 0), (0, 0), (0, 8 - H_FOX), (0, 0)))

    e_mat = jnp.asarray(_head_match(D_GK, GLA_DK, D_GLA, GLA_DV), bf16)
    mask_t = jnp.asarray(_head_match(D_GLA, GLA_DV, D_GK, GLA_DK), f32)
    avg = jnp.asarray(_head_match(D_GLA, GLA_DV, D_GLA, GLA_DV), bf16)
    zeros_state = jnp.zeros((1, D_STATE), f32)
    u_tab = peer_u.astype(bf16)
    v_tab = peer_v.astype(bf16)

    outs_p, outs_s = [], []
    for l in range(depth):
        w_p = _pack_w_in(w_in[l])
        bfp = jnp.pad(fox_bf[l], (0, LANES - H_FOX)).reshape(1, LANES)
        wa2p = jnp.pad(gla_wa2[l], ((TAIL_GA, LANES - TAIL_GA - GLA_RANK), (0, 0))).astype(bf16)
        ba = gla_ba[l].reshape(1, D_GK)
        a_re = ssm_a_re[l].reshape(1, D_STATE)
        a_im = ssm_a_im[l].reshape(1, D_STATE)
        ldt = jnp.repeat(ssm_log_dt[l], SSM_P).reshape(1, D_STATE)
        bt_re = jnp.transpose(ssm_b_re[l], (2, 0, 1)).reshape(SSM_GC, D_STATE)
        bt_im = jnp.transpose(ssm_b_im[l], (2, 0, 1)).reshape(SSM_GC, D_STATE)
        p_re, p_im, bb_re, bb_im = _ssm_prep(a_re, a_im, ldt, bt_re, bt_im, tdec)
        to_blocks = lambda bb: jnp.transpose(bb.reshape(SSM_GC, SSM_G, SSM_P), (1, 0, 2))
        wb = jnp.concatenate([_block_diag(to_blocks(bb_re)), _block_diag(to_blocks(bb_im))], axis=1).astype(bf16)
        c_blocks = lambda cc: jnp.transpose(cc, (0, 2, 1))
        wc = jnp.concatenate([_block_diag(c_blocks(ssm_c_re[l])), -_block_diag(c_blocks(ssm_c_im[l]))],
                             axis=0).astype(bf16)
        dvec = ssm_d[l].reshape(1, D_SSM)
        wglu = ssm_w_glu[l].astype(bf16)
        bglu = ssm_b_glu[l].reshape(1, D_SSM)
        wqt = jnp.transpose(peer_wq[l].reshape(dm, PEER_HEADS * PEER_DK)).astype(bf16)
        keys = peer_keys[l].astype(bf16)

        (fq, qaug, kaug, fvb, fk, fv, lf, gq, gk, gv, la, gr, su) = _inproj(
            x, g_mix[l].reshape(1, dm), w_p, bfp, wa2p, ba)

        fox_p = _fox_prompt(qaug, kaug, fvb, seq)
        lfn_t = jnp.pad(jnp.swapaxes(lf[seq:n_tok, :8].reshape(bd, tdec, 8), 1, 2), ((0, 0), (0, 0), (0, LANES - tdec)))
        fox_s = _fox_sample(fq[seq:n_tok].astype(f32), fk[seq:n_tok], fv[seq:n_tok], lfn_t, ck, cv, clft, page_table, l, tdec)
        fox_o = jnp.concatenate([fox_p, fox_s, pad_rows(fox_s)], axis=0)

        gla_p, st_p = _gla(gq, gk, la, gv, jnp.zeros((1, D_GLA, D_GK), f32), e_mat, mask_t,
                           row0=0, nseq=1, steps=seq // GLA_STEP, rows_per_step=GLA_STEP, chunk=GLA_CHUNK)
        s0 = state_gla[l].astype(f32)
        s0t = jnp.transpose(s0, (0, 1, 3, 2))
        s0t = (s0t[:, :, :, None, :] * jnp.eye(H_GLA, dtype=f32)[None, :, None, :, None]).reshape(bd, D_GLA, D_GK)
        gla_s, st_s = _gla(gq, gk, la, gv, s0t, e_mat, mask_t,
                           row0=seq, nseq=bd, steps=1, rows_per_step=tdec, chunk=tdec)
        gla_o = jnp.concatenate([gla_p, gla_s, pad_rows(gla_s)], axis=0)

        def unpack_state(st):
            st = st.reshape(-1, H_GLA, GLA_DV, H_GLA, GLA_DK)
            st = jnp.stack([st[:, h, :, h, :] for h in range(H_GLA)], axis=1)
            return jnp.transpose(st, (0, 1, 3, 2))

        ssm_p, hre_p, him_p = _ssm(su, wb, p_re, p_im, zeros_state, zeros_state, wc, dvec, wglu, bglu,
                                   row0=0, n_rows=seq, tile_carry=True)
        h0r = jnp.repeat(state_ssm_re[l].reshape(bd, D_STATE).astype(f32), tdec, axis=0)
        h0i = jnp.repeat(state_ssm_im[l].reshape(bd, D_STATE).astype(f32), tdec, axis=0)
        ssm_s, hre_s, him_s = _ssm(su, wb, p_re, p_im, h0r, h0i, wc, dvec, wglu, bglu,
                                   row0=seq, n_rows=n_s, tile_carry=False)
        ssm_o = jnp.concatenate([ssm_p, ssm_s, pad_rows(ssm_s)], axis=0)

        x1, xf = _outproj(fox_o, gla_o, gr, ssm_o, x, fox_norm[l].reshape(1, D_FOX),
                          jnp.tile(gla_norm[l], H_GLA).reshape(1, D_GLA), avg,
                          w_out[l].astype(bf16), g_ffn[l].reshape(1, dm))
        i1, i2, gate = _peer_topk(xf, wqt, keys)
        x = _peer_dense(xf, i1, i2, gate, u_tab, v_tab, l, x1, g_final.reshape(1, dm), l == depth - 1)

        outs_p.append((fk[:seq].reshape(bp, seq, H_FOX, FOX_DH), fv[:seq].reshape(bp, seq, H_FOX, FOX_DH),
                       lf[:seq, :H_FOX].reshape(bp, seq, H_FOX), unpack_state(st_p),
                       hre_p.reshape(bp, SSM_G, SSM_P), him_p.reshape(bp, SSM_G, SSM_P)))
        outs_s.append((fk[seq:n_tok].reshape(bd, tdec, H_FOX, FOX_DH), fv[seq:n_tok].reshape(bd, tdec, H_FOX, FOX_DH),
                       lf[seq:n_tok, :H_FOX].reshape(bd, tdec, H_FOX), unpack_state(st_s),
                       hre_s[tdec - 1::tdec].reshape(bd, SSM_G, SSM_P), him_s[tdec - 1::tdec].reshape(bd, SSM_G, SSM_P)))

    p_stk = [jnp.stack(t) for t in zip(*outs_p)]
    s_stk = [jnp.stack(t) for t in zip(*outs_s)]
    y_prompt = x[:seq].reshape(bp, seq, dm)
    y_sample = x[seq:n_tok].reshape(bd, tdec, dm)
    return (y_prompt, y_sample, *p_stk, *s_stk)
```

```python
import functools

import numpy as np
import jax
import jax.numpy as jnp
from jax import lax
from jax.experimental import pallas as pl
from jax.experimental.pallas import tpu as pltpu

f32 = jnp.float32
bf16 = jnp.bfloat16
i32 = jnp.int32

EPS = 1e-6
LOG2E = 1.4426950408889634
LANES = 128
VMEM_LIMIT = 56 * 1024 * 1024

H_FOX, FOX_DH = 6, 64
D_FOX = H_FOX * FOX_DH
H_GLA, GLA_DK, GLA_DV = 6, 32, 64
D_GK = H_GLA * GLA_DK
D_GLA = H_GLA * GLA_DV
GLA_RANK = 16
GLA_TAU = 16.0
SSM_G, SSM_GC, SSM_P = 16, 16, 64
D_SSM = SSM_G * SSM_GC
D_STATE = SSM_G * SSM_P
PEER_KEYS, PEER_HEADS, PEER_TOPK, PEER_DK = 128, 8, 16, 256
PEER_SLOTS = PEER_HEADS * PEER_TOPK

C_FQ, C_FK, C_FV = 0, 384, 768
C_GQ, C_GK = 1152, 1408
C_GV, C_GR, C_SU, C_TAIL = 1664, 2048, 2432, 2688
D_INP = 2816
TAIL_FF, TAIL_GA = 0, 8

TOK_TILE = 256
FOX_TILE = 1024
FOX_PAGES = 32
GLA_STEP = 512
GLA_CHUNK = 16
PEER_TOK = 512
PEER_ETILE = 2048
GATE_GROUP = 16
GLA_UNROLL = 8


def _dot(a, b):
    return jnp.dot(a, b, preferred_element_type=f32)


def _dot_nt(a, b):
    return lax.dot_general(a, b, (((1,), (1,)), ((), ())), preferred_element_type=f32)


def _dot_tn(a, b):
    return lax.dot_general(a, b, (((0,), (0,)), ((), ())), preferred_element_type=f32)


def _split3(x):
    h = x.astype(bf16)
    r = x - h.astype(f32)
    m = r.astype(bf16)
    l = (r - m.astype(f32)).astype(bf16)
    return h, m, l


def _dot3_lhs01(sel, x):
    h, m, l = _split3(x)
    return _dot(sel, h) + _dot(sel, m) + _dot(sel, l)


def _dot3_rhs01(x, sel):
    h, m, l = _split3(x)
    return _dot(h, sel) + _dot(m, sel) + _dot(l, sel)


def _log_sigmoid(x):
    return -(jnp.maximum(-x, 0.0) + jnp.log1p(jnp.exp(-jnp.abs(x))))


def _iota(shape, dim):
    return lax.broadcasted_iota(i32, shape, dim)


def _div_pow2(x, n):
    assert n & (n - 1) == 0
    return jnp.right_shift(x, n.bit_length() - 1)


def _mod_pow2(x, n):
    assert n & (n - 1) == 0
    return jnp.bitwise_and(x, n - 1)


def _params(sem, vmem=None):
    return pltpu.CompilerParams(dimension_semantics=sem, vmem_limit_bytes=vmem)


def _fox_bias_lanes():
    place_q = np.zeros((3, LANES, H_FOX * LANES), np.float32)
    place_k = np.zeros((3, LANES, H_FOX * LANES), np.float32)
    ones_q = np.zeros((1, H_FOX * LANES), np.float32)
    ones_k = np.zeros((1, H_FOX * LANES), np.float32)
    for j in range(H_FOX):
        base = j * LANES + (FOX_DH if j % 2 == 0 else 0)
        for p in range(3):
            place_q[p, j, base + p] = 1.0
            ones_k[0, base + p] = 1.0
            ones_q[0, base + 3 + p] = 1.0
            place_k[p, j, base + 3 + p] = 1.0
    return place_q, place_k, ones_q, ones_k


def _inproj_body(x_ref, g_ref, w_ref, bfp_ref, wa2_ref, ba_ref, pq_ref, pk_ref, oq_ref, ok_ref,
                 fq_ref, qa_ref, ka_ref, fvb_ref, fk_ref, fv_ref, lf_ref,
                 gq_ref, gk_ref, gv_ref, la_ref, gr_ref, su_ref, carry_ref):
    @pl.when(pl.program_id(0) == 0)
    def _init():
        carry_ref[...] = jnp.zeros_like(carry_ref)

    x = x_ref[...]
    tt = x.shape[0]
    xn = x * lax.rsqrt(jnp.mean(x * x, axis=-1, keepdims=True) + EPS) * g_ref[...]
    z = _dot(xn.astype(bf16), w_ref[...])
    fq_ref[...] = (z[:, C_FQ:C_FQ + D_FOX] * (FOX_DH ** -0.5)).astype(bf16)
    fk = z[:, C_FK:C_FK + D_FOX]
    fv = z[:, C_FV:C_FV + D_FOX]
    fk_ref[...] = fk
    fv_ref[...] = fv
    fvb_ref[...] = fv.astype(bf16)
    gq_ref[...] = z[:, C_GQ:C_GQ + D_GK] * (GLA_DK ** -0.5)
    gk_ref[...] = z[:, C_GK:C_GK + D_GK]
    gv_ref[...] = z[:, C_GV:C_GV + D_GLA]
    gr_ref[...] = z[:, C_GR:C_GR + D_GLA]
    su_ref[...] = z[:, C_SU:C_SU + D_SSM]
    tail = z[:, C_TAIL:C_TAIL + LANES]
    lane = _iota((tt, LANES), 1)
    lf = jnp.where(lane < H_FOX, _log_sigmoid(tail + bfp_ref[...]), 0.0)
    lf_ref[...] = lf
    tri = jnp.where(_iota((tt, tt), 0) >= _iota((tt, tt), 1), 1.0, 0.0).astype(bf16)
    cs = _dot3_lhs01(tri, lf) + carry_ref[...]
    carry_ref[...] = cs[tt - 1:tt, :]
    pieces = _split3(cs * LOG2E)
    q_bias = sum(_dot(pc, pq_ref[i]) for i, pc in enumerate(pieces)) + oq_ref[...]
    k_bias = ok_ref[...] - sum(_dot(pc, pk_ref[i]) for i, pc in enumerate(pieces))
    low = lane < FOX_DH
    for hp in range(H_FOX // 2):
        qt = z[:, C_FQ + hp * LANES:C_FQ + (hp + 1) * LANES] * (LOG2E * FOX_DH ** -0.5)
        kt = z[:, C_FK + hp * LANES:C_FK + (hp + 1) * LANES]
        for hh in range(2):
            blk = slice((2 * hp + hh) * LANES, (2 * hp + hh + 1) * LANES)
            real = low if hh == 0 else jnp.logical_not(low)
            qa_ref[:, blk] = jnp.where(real, qt, q_bias[:, blk]).astype(bf16)
            ka_ref[:, blk] = jnp.where(real, kt, k_bias[:, blk]).astype(bf16)
    za = _dot(tail.astype(bf16), wa2_ref[...]) + ba_ref[...]
    la_ref[...] = _log_sigmoid(za) * (1.0 / GLA_TAU)


def _inproj(x, g_mix, w_p, bfp, wa2p, ba):
    t = x.shape[0]
    tt = TOK_TILE
    row = lambda n: pl.BlockSpec((tt, n), lambda i: (i, 0))
    const = lambda a: pl.BlockSpec(a.shape, lambda i: (0,) * a.ndim)
    outs = [
        (D_FOX, bf16), (H_FOX * LANES, bf16), (H_FOX * LANES, bf16), (D_FOX, bf16), (D_FOX, f32), (D_FOX, f32),
        (LANES, f32),
        (D_GK, f32), (D_GK, f32), (D_GLA, f32), (D_GK, f32), (D_GLA, f32), (D_SSM, f32),
    ]
    place_q, place_k, ones_q, ones_k = _fox_bias_lanes()
    consts = [g_mix, w_p, bfp, wa2p, ba, jnp.asarray(place_q, bf16), jnp.asarray(place_k, bf16),
              jnp.asarray(ones_q), jnp.asarray(ones_k)]
    return pl.pallas_call(
        _inproj_body,
        grid=(t // tt,),
        in_specs=[row(x.shape[1])] + [const(a) for a in consts],
        out_specs=[row(n) for n, _ in outs],
        out_shape=[jax.ShapeDtypeStruct((t, n), d) for n, d in outs],
        scratch_shapes=[pltpu.VMEM((1, LANES), f32)],
        compiler_params=_params(("arbitrary",), VMEM_LIMIT),
        name="inproj",
    )(x, *consts)


def _fox_prompt_body(qi_ref, ki_ref, qa_ref, qb_ref, ka_ref, kb_ref, v_ref, o_ref,
                     m_ref, l_ref, acc_ref):
    p = pl.program_id(1)
    qi = qi_ref[p]
    ki = ki_ref[p]
    tq, tk = qa_ref.shape[0], ka_ref.shape[0]

    @pl.when(ki == 0)
    def _init():
        m_ref[...] = jnp.full_like(m_ref, -jnp.inf)
        l_ref[...] = jnp.zeros_like(l_ref)
        acc_ref[...] = jnp.zeros_like(acc_ref)

    def step(diagonal):
        vext = jnp.concatenate([v_ref[...], jnp.ones((tk, LANES), bf16)], axis=1)
        scores = [_dot_nt(qa_ref[...], ka_ref[...]), _dot_nt(qb_ref[...], kb_ref[...])]
        for hh in range(2):
            s = scores[hh]
            if diagonal:
                s = jnp.where(_iota((tq, tk), 0) >= _iota((tq, tk), 1), s, -jnp.inf)
            m_old = m_ref[hh]
            m_new = jnp.maximum(m_old, jnp.max(s, axis=-1, keepdims=True))
            pr = jnp.exp2(s - jnp.tile(m_new, (1, tk // LANES)))
            alpha = jnp.exp2(m_old - m_new)
            pv = _dot(pr.astype(bf16), vext)
            l_ref[hh] = alpha * l_ref[hh] + pv[:, LANES:]
            m_ref[hh] = m_new
            acc_ref[hh] = alpha * acc_ref[hh] + pv[:, :LANES]

    @pl.when(ki < qi)
    def _off():
        step(False)

    @pl.when(ki == qi)
    def _diag():
        step(True)
        lane = _iota((1, LANES), 1)
        o_ref[...] = jnp.where(lane < FOX_DH, acc_ref[0] / l_ref[0], acc_ref[1] / l_ref[1])


def _fox_prompt(qaug, kaug, fvb, seq):
    tq = FOX_TILE
    nq = seq // tq
    qi_tbl = np.concatenate([np.full(i + 1, i) for i in range(nq)]).astype(np.int32)
    ki_tbl = np.concatenate([np.arange(i + 1) for i in range(nq)]).astype(np.int32)
    grid_spec = pltpu.PrefetchScalarGridSpec(
        num_scalar_prefetch=2,
        grid=(H_FOX // 2, len(qi_tbl)),
        in_specs=[
            pl.BlockSpec((tq, LANES), lambda h, p, qi, ki: (qi[p], 2 * h)),
            pl.BlockSpec((tq, LANES), lambda h, p, qi, ki: (qi[p], 2 * h + 1)),
            pl.BlockSpec((tq, LANES), lambda h, p, qi, ki: (ki[p], 2 * h)),
            pl.BlockSpec((tq, LANES), lambda h, p, qi, ki: (ki[p], 2 * h + 1)),
            pl.BlockSpec((tq, LANES), lambda h, p, qi, ki: (ki[p], h)),
        ],
        out_specs=pl.BlockSpec((tq, LANES), lambda h, p, qi, ki: (qi[p], h)),
        scratch_shapes=[pltpu.VMEM((2, tq, LANES), f32), pltpu.VMEM((2, tq, LANES), f32),
                        pltpu.VMEM((2, tq, LANES), f32)],
    )
    return pl.pallas_call(
        _fox_prompt_body,
        grid_spec=grid_spec,
        out_shape=jax.ShapeDtypeStruct((seq, D_FOX), f32),
        compiler_params=_params(("arbitrary", "arbitrary"), VMEM_LIMIT),
        name="fox_prompt",
    )(jnp.asarray(qi_tbl), jnp.asarray(ki_tbl), qaug, qaug, kaug, kaug, fvb)


def _expand_heads(x, rows):
    return jnp.concatenate(
        [jnp.broadcast_to(x[h:h + 1, :], (rows, x.shape[1])) for h in range(H_FOX)], axis=0)


def _fox_sample_body(pt_ref, q_ref, kn_ref, vn_ref, lfn_ref, *rest):
    del pt_ref
    npg = FOX_PAGES
    kp_refs = rest[0:npg]
    vp_refs = rest[npg:2 * npg]
    lfp_refs = rest[2 * npg:3 * npg]
    o_ref = rest[3 * npg]
    m_ref, l_ref, acc_ref, car_ref, qbd_ref, cq_ref = rest[3 * npg + 1:]
    j = pl.program_id(1)
    t = q_ref.shape[0]
    hq = H_FOX * t
    page = kp_refs[0].shape[-1]

    def update(s, weigh):
        m_old = m_ref[...]
        m_new = jnp.maximum(m_old, jnp.max(s, axis=-1, keepdims=True))
        pr = jnp.exp(s - m_new)
        alpha = jnp.exp(m_old - m_new)
        l_ref[...] = alpha * l_ref[...] + jnp.sum(pr, axis=-1, keepdims=True)
        m_ref[...] = m_new
        acc_ref[...] = alpha * acc_ref[...] + weigh(pr.astype(bf16))

    @pl.when(j == 0)
    def _new_tokens():
        m_ref[...] = jnp.full_like(m_ref, -jnp.inf)
        l_ref[...] = jnp.zeros_like(l_ref)
        acc_ref[...] = jnp.zeros_like(acc_ref)
        car_ref[...] = jnp.zeros_like(car_ref)
        q = q_ref[...]
        q6 = jnp.concatenate([q] * H_FOX, axis=0)
        keep = _div_pow2(_iota((hq, D_FOX), 0), t) == _div_pow2(_iota((hq, D_FOX), 1), FOX_DH)
        qbd_ref[...] = jnp.where(keep, q6, 0.0).astype(bf16)
        cum = lfn_ref[...]
        lane8 = _iota(cum.shape, 1)
        sh = 1
        while sh < t:
            cum = cum + jnp.where(lane8 >= sh, pltpu.roll(cum, sh, 1), 0.0)
            sh *= 2
        cum_rows = _expand_heads(cum, t)
        qpos = _mod_pow2(_iota((hq, LANES), 0), t)
        kpos = _iota((hq, LANES), 1)
        cq = jnp.sum(jnp.where(kpos == qpos, cum_rows, 0.0), axis=-1, keepdims=True)
        cq_ref[...] = cq
        zpad = jnp.zeros((LANES - t, D_FOX), f32)
        kn = jnp.concatenate([kn_ref[...], zpad], axis=0).astype(bf16)
        vn = jnp.concatenate([vn_ref[...], zpad], axis=0).astype(bf16)
        s = _dot_nt(qbd_ref[...], kn) + cq - cum_rows
        s = jnp.where(kpos <= qpos, s, -jnp.inf)
        update(s, lambda pr: _dot(pr, vn))

    upper = jnp.where(_iota((page, page), 0) > _iota((page, page), 1), 1.0, 0.0).astype(bf16)
    qbd = qbd_ref[...]
    qk = [_dot(qbd, kp_refs[g][...].reshape(D_FOX, page).astype(bf16)) for g in range(npg)]
    lf_all = jnp.concatenate([lfp_refs[g][...] for g in range(npg)], axis=0)
    suf_all = _dot3_rhs01(lf_all, upper)
    tot_all = jnp.sum(lf_all, axis=-1, keepdims=True)
    car = car_ref[...]
    scores = []
    for g in range(npg):
        scores.append(qk[g] + _expand_heads(suf_all[g * 8:(g + 1) * 8, :] + car, t))
        car = car + tot_all[g * 8:(g + 1) * 8, :]
    car_ref[...] = car

    def weigh_pages(pr):
        out = None
        for g in range(npg):
            vpt = vp_refs[g][...].reshape(D_FOX, page).astype(bf16)
            part = _dot_nt(pr[:, g * page:(g + 1) * page], vpt)
            out = part if out is None else out + part
        return out

    update(jnp.concatenate(scores, axis=1) + cq_ref[...], weigh_pages)

    @pl.when(j == pl.num_programs(1) - 1)
    def _finish():
        o48 = acc_ref[...] / l_ref[...]
        head = _div_pow2(_iota((t, D_FOX), 1), FOX_DH)
        out = jnp.zeros((t, D_FOX), f32)
        for h in range(H_FOX):
            out = out + jnp.where(head == h, o48[h * t:(h + 1) * t, :], 0.0)
        o_ref[...] = out


def _fox_sample(q_s, k_s, v_s, lfn_t, cache_k, cache_v, cache_lft, page_table, layer, t):
    nb, n_pages = page_table.shape
    page = cache_k.shape[-1]
    npg = FOX_PAGES
    hq = H_FOX * t
    seq_spec = pl.BlockSpec((t, D_FOX), lambda b, j, pt: (b, 0))

    def page_spec(shape, g):
        zeros = (0,) * len(shape)
        return pl.BlockSpec((None, None) + shape,
                            lambda b, j, pt: (layer, pt[b, n_pages - 1 - (j * npg + g)]) + zeros)

    grid_spec = pltpu.PrefetchScalarGridSpec(
        num_scalar_prefetch=1,
        grid=(nb, n_pages // npg),
        in_specs=[seq_spec, seq_spec, seq_spec,
                  pl.BlockSpec((None, 8, LANES), lambda b, j, pt: (b, 0, 0))]
        + [page_spec((H_FOX, FOX_DH, page), g) for g in range(npg)]
        + [page_spec((H_FOX, FOX_DH, page), g) for g in range(npg)]
        + [page_spec((8, page), g) for g in range(npg)],
        out_specs=seq_spec,
        scratch_shapes=[pltpu.VMEM((hq, 1), f32), pltpu.VMEM((hq, 1), f32),
                        pltpu.VMEM((hq, D_FOX), f32), pltpu.VMEM((8, 1), f32),
                        pltpu.VMEM((hq, D_FOX), bf16), pltpu.VMEM((hq, 1), f32)],
    )
    return pl.pallas_call(
        _fox_sample_body,
        grid_spec=grid_spec,
        out_shape=jax.ShapeDtypeStruct((nb * t, D_FOX), f32),
        compiler_params=_params(("arbitrary", "arbitrary"), VMEM_LIMIT),
        name="fox_sample",
    )(page_table, q_s, k_s, v_s, lfn_t, *([cache_k] * npg), *([cache_v] * npg), *([cache_lft] * npg))


def _gla_body(q_ref, k_ref, la_ref, v_ref, s0_ref, e_ref, mask_ref, o_ref, sout_ref, st_ref, *, chunk):
    i = pl.program_id(1)

    @pl.when(i == 0)
    def _init():
        st_ref[...] = s0_ref[...]

    c = chunk
    n_chunks = q_ref.shape[0] // c
    row = _iota((c, D_GK), 0)

    def one_chunk(ci, carry):
        r0 = pl.multiple_of(ci * c, c)
        q = q_ref[pl.ds(r0, c), :]
        k = k_ref[pl.ds(r0, c), :]
        v = v_ref[pl.ds(r0, c), :]
        b = la_ref[pl.ds(r0, c), :]
        sh = 1
        while sh < c:
            b = b + jnp.where(row >= sh, pltpu.roll(b, sh, 0), 0.0)
            sh *= 2
        pieces = []
        for s in range(c):
            d = jnp.where(row >= s, b - b[s:s + 1, :], -jnp.inf)
            pieces.append(q * k[s:s + 1, :] * jnp.exp(d))
        pst = jnp.concatenate(pieces, axis=0)
        hi = pst.astype(bf16)
        lo = (pst - hi.astype(f32)).astype(bf16)
        att = _dot(hi, e_ref[...]) + _dot(lo, e_ref[...])
        b_last = b[c - 1:c, :]
        kd = (k * jnp.exp(b_last - b)).astype(bf16)
        kv = _dot_tn(v.astype(bf16), kd) * mask_ref[...]
        qe = (q * jnp.exp(b)).astype(bf16)
        o = att[0:c, :] * v[0:1, :]
        for s in range(1, c):
            o = o + att[s * c:(s + 1) * c, :] * v[s:s + 1, :]
        st = st_ref[...]
        o_ref[pl.ds(r0, c), :] = o + _dot_nt(qe, st.astype(bf16))
        st_ref[...] = st * jnp.exp(b_last) + kv
        return carry

    lax.fori_loop(0, n_chunks, one_chunk, 0, unroll=min(GLA_UNROLL, n_chunks))

    @pl.when(i == pl.num_programs(1) - 1)
    def _finish():
        sout_ref[...] = st_ref[...]


def _gla(gq, gk, la, gv, s0t, e_mat, mask_t, *, row0, nseq, steps, rows_per_step, chunk):
    blk0 = row0 // rows_per_step
    rows = lambda n: pl.BlockSpec((rows_per_step, n), lambda s, i: (blk0 + s * steps + i, 0))
    orow = pl.BlockSpec((rows_per_step, D_GLA), lambda s, i: (s * steps + i, 0))
    st_spec = pl.BlockSpec((None, D_GLA, D_GK), lambda s, i: (s, 0, 0))
    const = lambda a: pl.BlockSpec(a.shape, lambda s, i: (0,) * a.ndim)
    return pl.pallas_call(
        functools.partial(_gla_body, chunk=chunk),
        grid=(nseq, steps),
        in_specs=[rows(D_GK), rows(D_GK), rows(D_GK), rows(D_GLA), st_spec, const(e_mat), const(mask_t)],
        out_specs=[orow, st_spec],
        out_shape=[jax.ShapeDtypeStruct((nseq * steps * rows_per_step, D_GLA), f32),
                   jax.ShapeDtypeStruct((nseq, D_GLA, D_GK), f32)],
        scratch_shapes=[pltpu.VMEM((D_GLA, D_GK), f32)],
        compiler_params=_params(("arbitrary", "arbitrary"), VMEM_LIMIT),
        name="gla",
    )(gq, gk, la, gv, s0t, e_mat, mask_t)


def _ssm_prep_body(are_ref, aim_ref, ldt_ref, bre_ref, bim_ref,
                   pre_ref, pim_ref, bbre_ref, bbim_ref):
    a_re = are_ref[...]
    a_im = aim_ref[...]
    dt = jnp.exp(ldt_ref[...])
    mag = jnp.exp(a_re * dt)
    abar_re = mag * jnp.cos(a_im * dt)
    abar_im = mag * jnp.sin(a_im * dt)
    den = a_re * a_re + a_im * a_im
    nr, ni = abar_re - 1.0, abar_im
    fr = (nr * a_re + ni * a_im) / den
    fi = (ni * a_re - nr * a_im) / den
    bbre_ref[...] = fr * bre_ref[...] - fi * bim_ref[...]
    bbim_ref[...] = fr * bim_ref[...] + fi * bre_ref[...]
    n = pre_ref.shape[0]
    pre_ref[0:1, :] = abar_re
    pim_ref[0:1, :] = abar_im
    m = 1
    while m < n:
        tr = pre_ref[m - 1:m, :]
        ti = pim_ref[m - 1:m, :]
        xr = pre_ref[0:m, :]
        xi = pim_ref[0:m, :]
        pre_ref[m:2 * m, :] = xr * tr - xi * ti
        pim_ref[m:2 * m, :] = xr * ti + xi * tr
        m *= 2


def _ssm_prep(a_re, a_im, log_dt, bt_re, bt_im, n_pow):
    outs = [(n_pow, D_STATE), (n_pow, D_STATE), (SSM_GC, D_STATE), (SSM_GC, D_STATE)]
    return pl.pallas_call(
        _ssm_prep_body,
        out_shape=[jax.ShapeDtypeStruct(s, f32) for s in outs],
        name="ssm_prep",
    )(a_re, a_im, log_dt, bt_re, bt_im)


def _ssm_body(u_ref, wb_ref, pre_ref, pim_ref, hin_re_ref, hin_im_ref, wc_ref, d_ref, wg_ref, bg_ref,
              o_ref, hre_ref, him_ref, car_re_ref, car_im_ref, *, tile_carry):
    i = pl.program_id(0)
    u = u_ref[...]
    tt = u.shape[0]
    grp = pre_ref.shape[0]
    d = _dot(u.astype(bf16), wb_ref[...])
    xr = d[:, :D_STATE]
    xi = d[:, D_STATE:]
    pos = _mod_pow2(_iota((tt, D_STATE), 0), grp)
    sh = 1
    while sh < grp:
        ar = pre_ref[sh - 1:sh, :]
        ai = pim_ref[sh - 1:sh, :]
        sr = jnp.where(pos >= sh, pltpu.roll(xr, sh, 0), 0.0)
        si = jnp.where(pos >= sh, pltpu.roll(xi, sh, 0), 0.0)
        xr, xi = xr + ar * sr - ai * si, xi + ar * si + ai * sr
        sh *= 2
    pr = pre_ref[...]
    pi = pim_ref[...]
    if tile_carry:
        @pl.when(i == 0)
        def _init():
            car_re_ref[...] = hin_re_ref[...]
            car_im_ref[...] = hin_im_ref[...]
        cr = car_re_ref[...]
        ci = car_im_ref[...]
        hr_groups, hi_groups = [], []
        for g in range(tt // grp):
            gr = xr[g * grp:(g + 1) * grp, :] + pr * cr - pi * ci
            gi = xi[g * grp:(g + 1) * grp, :] + pr * ci + pi * cr
            cr, ci = gr[grp - 1:grp, :], gi[grp - 1:grp, :]
            hr_groups.append(gr)
            hi_groups.append(gi)
        hr = jnp.concatenate(hr_groups, axis=0)
        hi = jnp.concatenate(hi_groups, axis=0)
        car_re_ref[...] = cr
        car_im_ref[...] = ci
        hre_ref[...] = cr
        him_ref[...] = ci
    else:
        h0r = hin_re_ref[...]
        h0i = hin_im_ref[...]
        prt = jnp.tile(pr, (tt // grp, 1))
        pit = jnp.tile(pi, (tt // grp, 1))
        hr = xr + prt * h0r - pit * h0i
        hi = xi + prt * h0i + pit * h0r
        hre_ref[...] = hr
        him_ref[...] = hi
    hcat = jnp.concatenate([hr, hi], axis=1).astype(bf16)
    y = _dot(hcat, wc_ref[...]) + d_ref[...] * u
    zg = jax.nn.gelu(y)
    o_ref[...] = (zg * jax.nn.sigmoid(_dot(zg.astype(bf16), wg_ref[...]) + bg_ref[...])).astype(bf16)


def _ssm(su, wb, p_re, p_im, hin_re, hin_im, wc, dvec, wglu, bglu, *, row0, n_rows, tile_carry):
    tt = TOK_TILE
    blk0 = row0 // tt
    const = lambda a: pl.BlockSpec(a.shape, lambda i: (0,) * a.ndim)
    if tile_carry:
        h_spec = const(hin_re)
        hout_spec = pl.BlockSpec((1, D_STATE), lambda i: (0, 0))
        hout_shape = jax.ShapeDtypeStruct((1, D_STATE), f32)
    else:
        h_spec = pl.BlockSpec((tt, D_STATE), lambda i: (i, 0))
        hout_spec = h_spec
        hout_shape = jax.ShapeDtypeStruct((n_rows, D_STATE), f32)
    return pl.pallas_call(
        functools.partial(_ssm_body, tile_carry=tile_carry),
        grid=(n_rows // tt,),
        in_specs=[pl.BlockSpec((tt, D_SSM), lambda i: (blk0 + i, 0)), const(wb), const(p_re), const(p_im),
                  h_spec, h_spec, const(wc), const(dvec), const(wglu), const(bglu)],
        out_specs=[pl.BlockSpec((tt, D_SSM), lambda i: (i, 0)), hout_spec, hout_spec],
        out_shape=[jax.ShapeDtypeStruct((n_rows, D_SSM), bf16), hout_shape, hout_shape],
        scratch_shapes=[pltpu.VMEM((1, D_STATE), f32), pltpu.VMEM((1, D_STATE), f32)],
        compiler_params=_params(("arbitrary",), VMEM_LIMIT),
        name="ssm",
    )(su, wb, p_re, p_im, hin_re, hin_im, wc, dvec, wglu, bglu)


def _outproj_body(fox_ref, gla_ref, gr_ref, ssm_ref, x_ref, fn_ref, gn_ref, avg_ref, w_ref, gf_ref,
                  x1_ref, xf_ref):
    fo = fox_ref[...]
    fo = fo * lax.rsqrt(jnp.mean(fo * fo, axis=-1, keepdims=True) + EPS) * fn_ref[...]
    go = gla_ref[...]
    ms = _dot3_rhs01(go * go, avg_ref[...]) * (1.0 / GLA_DV)
    go = go * lax.rsqrt(ms + EPS) * gn_ref[...] * jax.nn.silu(gr_ref[...])
    w = w_ref
    y = (_dot(fo.astype(bf16), w[0:D_FOX, :]) + _dot(go.astype(bf16), w[D_FOX:D_FOX + D_GLA, :])
         + _dot(ssm_ref[...], w[D_FOX + D_GLA:, :]))
    x1 = x_ref[...] + y
    x1_ref[...] = x1
    xf = x1 * lax.rsqrt(jnp.mean(x1 * x1, axis=-1, keepdims=True) + EPS) * gf_ref[...]
    xf_ref[...] = xf.astype(bf16)


def _outproj(fox_o, gla_o, gr, ssm_o, x, fox_norm, gla_norm6, avg, w_out, g_ffn):
    t, dm = x.shape
    tt = TOK_TILE
    row = lambda n: pl.BlockSpec((tt, n), lambda i: (i, 0))
    const = lambda a: pl.BlockSpec(a.shape, lambda i: (0,) * a.ndim)
    return pl.pallas_call(
        _outproj_body,
        grid=(t // tt,),
        in_specs=[row(D_FOX), row(D_GLA), row(D_GLA), row(D_SSM), row(dm),
                  const(fox_norm), const(gla_norm6), const(avg), const(w_out), const(g_ffn)],
        out_specs=[row(dm), row(dm)],
        out_shape=[jax.ShapeDtypeStruct((t, dm), f32), jax.ShapeDtypeStruct((t, dm), bf16)],
        compiler_params=_params(("arbitrary",), VMEM_LIMIT),
        name="outproj",
    )(fox_o, gla_o, gr, ssm_o, x, fox_norm, gla_norm6, avg, w_out, g_ffn)


def _topk_rows(s, k, ids=None):
    it = _iota(s.shape, 0) if ids is None else ids
    big = jnp.iinfo(jnp.int32).max
    vals, idxs = [], []
    for _ in range(k):
        m = jnp.max(s, axis=0, keepdims=True)
        idx = jnp.min(jnp.where(s == m, it, big), axis=0, keepdims=True)
        vals.append(m)
        idxs.append(idx)
        s = jnp.where(it == idx, -jnp.inf, s)
    return jnp.concatenate(vals, axis=0), jnp.concatenate(idxs, axis=0)


def _pair_candidates(v1, v2, tt):
    kk = v1.shape[0]
    blocks, ids = [], []
    for a in range(kk):
        nb = kk // (a + 1)
        rows = -(-nb // 8) * 8
        blk = v1[a:a + 1, :] + v2[0:rows, :]
        b_id = _iota((rows, tt), 0)
        if nb < rows:
            blk = jnp.where(b_id < nb, blk, -jnp.inf)
        blocks.append(blk)
        ids.append(b_id + a * kk)
    return jnp.concatenate(blocks, axis=0), jnp.concatenate(ids, axis=0)


def _take_rows(table, sel):
    return jnp.sum(jnp.where(_iota(table.shape, 0) == sel, table, 0), axis=0, keepdims=True)


def _peer_select(qt, keys1, keys2):
    kk = PEER_TOPK
    half = PEER_DK // 2
    v1, i1 = _topk_rows(_dot(keys1, qt[:half, :].astype(bf16)), kk)
    v2, i2 = _topk_rows(_dot(keys2, qt[half:, :].astype(bf16)), kk)
    cand, cand_id = _pair_candidates(v1, v2, qt.shape[1])
    sc, pos = _topk_rows(cand, kk, cand_id)
    sel1 = jnp.concatenate([_take_rows(i1, _div_pow2(pos[j:j + 1, :], kk)) for j in range(kk)], axis=0)
    sel2 = jnp.concatenate([_take_rows(i2, _mod_pow2(pos[j:j + 1, :], kk)) for j in range(kk)], axis=0)
    ex = jnp.exp(sc - sc[0:1, :])
    return sel1, sel2, ex / jnp.sum(ex, axis=0, keepdims=True)


def _peer_topk_body(xf_ref, wqt_ref, keys_ref, i1_ref, i2_ref, g_ref):
    qt = _dot_nt(wqt_ref[...], xf_ref[...])
    picks = [_peer_select(qt[h * PEER_DK:(h + 1) * PEER_DK, :], keys_ref[h, 0], keys_ref[h, 1])
             for h in range(PEER_HEADS)]
    i1_ref[...] = jnp.concatenate([p[0] for p in picks], axis=0).T
    i2_ref[...] = jnp.concatenate([p[1] for p in picks], axis=0).T
    g_ref[...] = jnp.concatenate([p[2] for p in picks], axis=0).T


def _peer_topk(xf, wqt, keys):
    t, dm = xf.shape
    tt = TOK_TILE
    const = lambda a: pl.BlockSpec(a.shape, lambda i: (0,) * a.ndim)
    slot = pl.BlockSpec((tt, PEER_SLOTS), lambda i: (i, 0))
    return pl.pallas_call(
        _peer_topk_body,
        grid=(t // tt,),
        in_specs=[pl.BlockSpec((tt, dm), lambda i: (i, 0)), const(wqt), const(keys)],
        out_specs=[slot, slot, slot],
        out_shape=[jax.ShapeDtypeStruct((t, PEER_SLOTS), i32), jax.ShapeDtypeStruct((t, PEER_SLOTS), i32),
                   jax.ShapeDtypeStruct((t, PEER_SLOTS), f32)],
        compiler_params=_params(("arbitrary",), VMEM_LIMIT),
        name="peer_topk",
    )(xf, wqt, keys)


def _peer_dense_body(xf_ref, i1_ref, i2_ref, g_ref, u_ref, v_ref, x1_ref, gfin_ref, o_ref,
                     gate_ref, acc_ref, *, final_norm):
    e = pl.program_id(1)
    tt = xf_ref.shape[0]
    nk = PEER_KEYS
    rows_per_tile = u_ref.shape[0] // nk

    @pl.when(e == 0)
    def _build_gates():
        acc_ref[...] = jnp.zeros_like(acc_ref)
        key_id = _iota((nk, PEER_SLOTS), 0)

        def token_group(gi, carry):
            t0 = pl.multiple_of(gi * GATE_GROUP, GATE_GROUP)
            r1 = i1_ref[pl.ds(t0, GATE_GROUP), :]
            r2 = i2_ref[pl.ds(t0, GATE_GROUP), :]
            gr = g_ref[pl.ds(t0, GATE_GROUP), :]
            mats = []
            for j in range(GATE_GROUP):
                p1 = jnp.where(r1[j:j + 1, :] == key_id, gr[j:j + 1, :], 0.0).astype(bf16)
                p2 = jnp.where(r2[j:j + 1, :] == key_id, 1.0, 0.0).astype(bf16)
                mats.append(_dot_nt(p1, p2))
            gate_ref[:, pl.ds(t0, GATE_GROUP), :] = jnp.swapaxes(jnp.stack(mats, axis=0), 0, 1).astype(bf16)
            return carry

        lax.fori_loop(0, tt // GATE_GROUP, token_group, 0, unroll=4)

    a = _dot_nt(xf_ref[...], u_ref[...])
    act = jax.nn.gelu(a)
    ws = []
    for r in range(rows_per_tile):
        gt = gate_ref[e * rows_per_tile + r]
        ws.append((gt.astype(f32) * act[:, r * nk:(r + 1) * nk]).astype(bf16))
    acc_ref[...] += _dot(jnp.concatenate(ws, axis=1), v_ref[...])

    @pl.when(e == pl.num_programs(1) - 1)
    def _finish():
        y = x1_ref[...] + acc_ref[...]
        if final_norm:
            y = y * lax.rsqrt(jnp.mean(y * y, axis=-1, keepdims=True) + EPS) * gfin_ref[...]
        o_ref[...] = y


def _peer_dense(xf, i1, i2, g, u_tab, v_tab, layer, x1, g_final, final_norm):
    t, dm = xf.shape
    tt = PEER_TOK
    et = PEER_ETILE
    tok = lambda n: pl.BlockSpec((tt, n), lambda i, e: (i, 0))
    tab = pl.BlockSpec((None, et, dm), lambda i, e: (layer, e, 0))
    return pl.pallas_call(
        functools.partial(_peer_dense_body, final_norm=final_norm),
        grid=(t // tt, u_tab.shape[1] // et),
        in_specs=[tok(dm), tok(PEER_SLOTS), tok(PEER_SLOTS), tok(PEER_SLOTS), tab, tab, tok(dm),
                  pl.BlockSpec(g_final.shape, lambda i, e: (0, 0))],
        out_specs=tok(dm),
        out_shape=jax.ShapeDtypeStruct((t, dm), f32),
        scratch_shapes=[pltpu.VMEM((PEER_KEYS, tt, PEER_KEYS), bf16), pltpu.VMEM((tt, dm), f32)],
        compiler_params=_params(("arbitrary", "arbitrary"), VMEM_LIMIT),
        name="peer_dense",
    )(xf, i1, i2, g, u_tab, v_tab, x1, g_final)


def _pack_w_in(w_in):
    dm = w_in.shape[0]
    sizes = (D_FOX, D_FOX, D_FOX, H_FOX, D_GK, D_GK, D_GLA, GLA_RANK, D_GLA, D_SSM)
    pts = np.cumsum(sizes)[:-1].tolist()
    fq, fk, fv, ff, gq, gk, gv, ga, gr, su = jnp.split(w_in, pts, axis=1)
    z = lambda n: jnp.zeros((dm, n), w_in.dtype)
    tail = jnp.concatenate([ff, z(TAIL_GA - H_FOX), ga, z(LANES - TAIL_GA - GLA_RANK)], axis=1)
    packed = jnp.concatenate([fq, fk, fv, gq, z(C_GK - C_GQ - D_GK), gk, z(C_GV - C_GK - D_GK),
                              gv, gr, su, tail], axis=1)
    assert packed.shape[1] == D_INP
    return packed.astype(bf16)


def _block_diag(blocks):
    g, a, b = blocks.shape
    eye = jnp.eye(g, dtype=blocks.dtype)
    return (blocks[:, :, None, :] * eye[:, None, :, None]).reshape(g * a, g * b)


def _head_match(rows, row_w, cols, col_w):
    return (np.arange(rows)[:, None] // row_w == np.arange(cols)[None, :] // col_w)


def kernel(x_prompt, x_sample, cache_fox_k, cache_fox_v, cache_fox_logf, state_gla, state_ssm_re, state_ssm_im, page_table, g_mix, w_in, fox_bf, fox_norm, gla_wa2, gla_ba, gla_norm, ssm_a_re, ssm_a_im, ssm_log_dt, ssm_b_re, ssm_b_im, ssm_c_re, ssm_c_im, ssm_d, ssm_w_glu, ssm_b_glu, w_out, g_ffn, peer_wq, peer_keys, peer_u, peer_v, g_final):
    bp, seq, dm = x_prompt.shape
    bd, tdec, _ = x_sample.shape
    depth = w_in.shape[0]
    assert bp == 1 and seq % FOX_TILE == 0 and seq % GLA_STEP == 0
    n_s = bd * tdec
    assert n_s == TOK_TILE and tdec == 8
    n_tok = seq + n_s
    n_pool, page = cache_fox_k.shape[1], cache_fox_k.shape[2]
    assert page == LANES and page_table.shape[1] % FOX_PAGES == 0

    n_pad = -n_tok % PEER_TOK
    pad_rows = lambda a: jnp.zeros((n_pad, a.shape[1]), a.dtype)
    x = jnp.concatenate([x_prompt.reshape(seq, dm), x_sample.reshape(n_s, dm), jnp.zeros((n_pad, dm), f32)], axis=0)
    ck = jnp.transpose(cache_fox_k, (0, 1, 3, 4, 2))
    cv = jnp.transpose(cache_fox_v, (0, 1, 3, 4, 2))
    clft = jnp.pad(jnp.swapaxes(cache_fox_logf, 2, 3), ((0, 0), (0, 0), (0, 8 - H_FOX), (0, 0)))

    e_mat = jnp.asarray(_head_match(D_GK, GLA_DK, D_GLA, GLA_DV), bf16)
    mask_t = jnp.asarray(_head_match(D_GLA, GLA_DV, D_GK, GLA_DK), f32)
    avg = jnp.asarray(_head_match(D_GLA, GLA_DV, D_GLA, GLA_DV), bf16)
    zeros_state = jnp.zeros((1, D_STATE), f32)
    u_tab = peer_u.astype(bf16)
    v_tab = peer_v.astype(bf16)

    outs_p, outs_s = [], []
    for l in range(depth):
        w_p = _pack_w_in(w_in[l])
        bfp = jnp.pad(fox_bf[l], (0, LANES - H_FOX)).reshape(1, LANES)
        wa2p = jnp.pad(gla_wa2[l], ((TAIL_GA, LANES - TAIL_GA - GLA_RANK), (0, 0))).astype(bf16)
        ba = gla_ba[l].reshape(1, D_GK)
        a_re = ssm_a_re[l].reshape(1, D_STATE)
        a_im = ssm_a_im[l].reshape(1, D_STATE)
        ldt = jnp.repeat(ssm_log_dt[l], SSM_P).reshape(1, D_STATE)
        bt_re = jnp.transpose(ssm_b_re[l], (2, 0, 1)).reshape(SSM_GC, D_STATE)
        bt_im = jnp.transpose(ssm_b_im[l], (2, 0, 1)).reshape(SSM_GC, D_STATE)
        p_re, p_im, bb_re, bb_im = _ssm_prep(a_re, a_im, ldt, bt_re, bt_im, tdec)
        to_blocks = lambda bb: jnp.transpose(bb.reshape(SSM_GC, SSM_G, SSM_P), (1, 0, 2))
        wb = jnp.concatenate([_block_diag(to_blocks(bb_re)), _block_diag(to_blocks(bb_im))], axis=1).astype(bf16)
        c_blocks = lambda cc: jnp.transpose(cc, (0, 2, 1))
        wc = jnp.concatenate([_block_diag(c_blocks(ssm_c_re[l])), -_block_diag(c_blocks(ssm_c_im[l]))],
                             axis=0).astype(bf16)
        dvec = ssm_d[l].reshape(1, D_SSM)
        wglu = ssm_w_glu[l].astype(bf16)
        bglu = ssm_b_glu[l].reshape(1, D_SSM)
        wqt = jnp.transpose(peer_wq[l].reshape(dm, PEER_HEADS * PEER_DK)).astype(bf16)
        keys = peer_keys[l].astype(bf16)

        (fq, qaug, kaug, fvb, fk, fv, lf, gq, gk, gv, la, gr, su) = _inproj(
            x, g_mix[l].reshape(1, dm), w_p, bfp, wa2p, ba)

        fox_p = _fox_prompt(qaug, kaug, fvb, seq)
        lfn_t = jnp.pad(jnp.swapaxes(lf[seq:n_tok, :8].reshape(bd, tdec, 8), 1, 2), ((0, 0), (0, 0), (0, LANES - tdec)))
        fox_s = _fox_sample(fq[seq:n_tok].astype(f32), fk[seq:n_tok], fv[seq:n_tok], lfn_t, ck, cv, clft, page_table, l, tdec)
        fox_o = jnp.concatenate([fox_p, fox_s, pad_rows(fox_s)], axis=0)

        gla_p, st_p = _gla(gq, gk, la, gv, jnp.zeros((1, D_GLA, D_GK), f32), e_mat, mask_t,
                           row0=0, nseq=1, steps=seq // GLA_STEP, rows_per_step=GLA_STEP, chunk=GLA_CHUNK)
        s0 = state_gla[l].astype(f32)
        s0t = jnp.transpose(s0, (0, 1, 3, 2))
        s0t = (s0t[:, :, :, None, :] * jnp.eye(H_GLA, dtype=f32)[None, :, None, :, None]).reshape(bd, D_GLA, D_GK)
        gla_s, st_s = _gla(gq, gk, la, gv, s0t, e_mat, mask_t,
                           row0=seq, nseq=bd, steps=1, rows_per_step=tdec, chunk=tdec)
        gla_o = jnp.concatenate([gla_p, gla_s, pad_rows(gla_s)], axis=0)

        def unpack_state(st):
            st = st.reshape(-1, H_GLA, GLA_DV, H_GLA, GLA_DK)
            st = jnp.stack([st[:, h, :, h, :] for h in range(H_GLA)], axis=1)
            return jnp.transpose(st, (0, 1, 3, 2))

        ssm_p, hre_p, him_p = _ssm(su, wb, p_re, p_im, zeros_state, zeros_state, wc, dvec, wglu, bglu,
                                   row0=0, n_rows=seq, tile_carry=True)
        h0r = jnp.repeat(state_ssm_re[l].reshape(bd, D_STATE).astype(f32), tdec, axis=0)
        h0i = jnp.repeat(state_ssm_im[l].reshape(bd, D_STATE).astype(f32), tdec, axis=0)
        ssm_s, hre_s, him_s = _ssm(su, wb, p_re, p_im, h0r, h0i, wc, dvec, wglu, bglu,
                                   row0=seq, n_rows=n_s, tile_carry=False)
        ssm_o = jnp.concatenate([ssm_p, ssm_s, pad_rows(ssm_s)], axis=0)

        x1, xf = _outproj(fox_o, gla_o, gr, ssm_o, x, fox_norm[l].reshape(1, D_FOX),
                          jnp.tile(gla_norm[l], H_GLA).reshape(1, D_GLA), avg,
                          w_out[l].astype(bf16), g_ffn[l].reshape(1, dm))
        i1, i2, gate = _peer_topk(xf, wqt, keys)
        x = _peer_dense(xf, i1, i2, gate, u_tab, v_tab, l, x1, g_final.reshape(1, dm), l == depth - 1)

        outs_p.append((fk[:seq].reshape(bp, seq, H_FOX, FOX_DH), fv[:seq].reshape(bp, seq, H_FOX, FOX_DH),
                       lf[:seq, :H_FOX].reshape(bp, seq, H_FOX), unpack_state(st_p),
                       hre_p.reshape(bp, SSM_G, SSM_P), him_p.reshape(bp, SSM_G, SSM_P)))
        outs_s.append((fk[seq:n_tok].reshape(bd, tdec, H_FOX, FOX_DH), fv[seq:n_tok].reshape(bd, tdec, H_FOX, FOX_DH),
                       lf[seq:n_tok, :H_FOX].reshape(bd, tdec, H_FOX), unpack_state(st_s),
                       hre_s[tdec - 1::tdec].reshape(bd, SSM_G, SSM_P), him_s[tdec - 1::tdec].reshape(bd, SSM_G, SSM_P)))

    p_stk = [jnp.stack(t) for t in zip(*outs_p)]
    s_stk = [jnp.stack(t) for t in zip(*outs_s)]
    y_prompt = x[:seq].reshape(bp, seq, dm)
    y_sample = x[seq:n_tok].reshape(bd, tdec, dm)
    return (y_prompt, y_sample, *p_stk, *s_stk)
```

```python
import functools

import numpy as np
import jax
import jax.numpy as jnp
from jax import lax
from jax.experimental import pallas as pl
from jax.experimental.pallas import tpu as pltpu

f32 = jnp.float32
bf16 = jnp.bfloat16
i32 = jnp.int32

EPS = 1e-6
LOG2E = 1.4426950408889634
LANES = 128
VMEM_LIMIT = 56 * 1024 * 1024

H_FOX, FOX_DH = 6, 64
D_FOX = H_FOX * FOX_DH
H_GLA, GLA_DK, GLA_DV = 6, 32, 64
D_GK = H_GLA * GLA_DK
D_GLA = H_GLA * GLA_DV
GLA_RANK = 16
GLA_TAU = 16.0
SSM_G, SSM_GC, SSM_P = 16, 16, 64
D_SSM = SSM_G * SSM_GC
D_STATE = SSM_G * SSM_P
PEER_KEYS, PEER_HEADS, PEER_TOPK, PEER_DK = 128, 8, 16, 256
PEER_SLOTS = PEER_HEADS * PEER_TOPK

C_FQ, C_FK, C_FV = 0, 384, 768
C_GQ, C_GK = 1152, 1408
C_GV, C_GR, C_SU, C_TAIL = 1664, 2048, 2432, 2688
D_INP = 2816
TAIL_FF, TAIL_GA = 0, 8

TOK_TILE = 256
FOX_TILE = 1024
FOX_PAGES = 32
GLA_STEP = 512
GLA_CHUNK = 16
PEER_TOK = 512
PEER_ETILE = 2048
GATE_GROUP = 16
GLA_UNROLL = 8


def _dot(a, b):
    return jnp.dot(a, b, preferred_element_type=f32)


def _dot_nt(a, b):
    return lax.dot_general(a, b, (((1,), (1,)), ((), ())), preferred_element_type=f32)


def _dot_tn(a, b):
    return lax.dot_general(a, b, (((0,), (0,)), ((), ())), preferred_element_type=f32)


def _split3(x):
    h = x.astype(bf16)
    r = x - h.astype(f32)
    m = r.astype(bf16)
    l = (r - m.astype(f32)).astype(bf16)
    return h, m, l


def _dot3_lhs01(sel, x):
    h, m, l = _split3(x)
    return _dot(sel, h) + _dot(sel, m) + _dot(sel, l)


def _dot3_rhs01(x, sel):
    h, m, l = _split3(x)
    return _dot(h, sel) + _dot(m, sel) + _dot(l, sel)


def _log_sigmoid(x):
    return -(jnp.maximum(-x, 0.0) + jnp.log1p(jnp.exp(-jnp.abs(x))))


def _iota(shape, dim):
    return lax.broadcasted_iota(i32, shape, dim)


def _div_pow2(x, n):
    assert n & (n - 1) == 0
    return jnp.right_shift(x, n.bit_length() - 1)


def _mod_pow2(x, n):
    assert n & (n - 1) == 0
    return jnp.bitwise_and(x, n - 1)


def _params(sem, vmem=None):
    return pltpu.CompilerParams(dimension_semantics=sem, vmem_limit_bytes=vmem)


def _fox_bias_lanes():
    place_q = np.zeros((3, LANES, H_FOX * LANES), np.float32)
    place_k = np.zeros((3, LANES, H_FOX * LANES), np.float32)
    ones_q = np.zeros((1, H_FOX * LANES), np.float32)
    ones_k = np.zeros((1, H_FOX * LANES), np.float32)
    for j in range(H_FOX):
        base = j * LANES + (FOX_DH if j % 2 == 0 else 0)
        for p in range(3):
            place_q[p, j, base + p] = 1.0
            ones_k[0, base + p] = 1.0
            ones_q[0, base + 3 + p] = 1.0
            place_k[p, j, base + 3 + p] = 1.0
    return place_q, place_k, ones_q, ones_k


def _inproj_body(x_ref, g_ref, w_ref, bfp_ref, wa2_ref, ba_ref, pq_ref, pk_ref, oq_ref, ok_ref,
                 fq_ref, qa_ref, ka_ref, fvb_ref, fk_ref, fv_ref, lf_ref,
                 gq_ref, gk_ref, gv_ref, la_ref, gr_ref, su_ref, carry_ref):
    @pl.when(pl.program_id(0) == 0)
    def _init():
        carry_ref[...] = jnp.zeros_like(carry_ref)

    x = x_ref[...]
    tt = x.shape[0]
    xn = x * lax.rsqrt(jnp.mean(x * x, axis=-1, keepdims=True) + EPS) * g_ref[...]
    z = _dot(xn.astype(bf16), w_ref[...])
    fq_ref[...] = (z[:, C_FQ:C_FQ + D_FOX] * (FOX_DH ** -0.5)).astype(bf16)
    fk = z[:, C_FK:C_FK + D_FOX]
    fv = z[:, C_FV:C_FV + D_FOX]
    fk_ref[...] = fk
    fv_ref[...] = fv
    fvb_ref[...] = fv.astype(bf16)
    gq_ref[...] = z[:, C_GQ:C_GQ + D_GK] * (GLA_DK ** -0.5)
    gk_ref[...] = z[:, C_GK:C_GK + D_GK]
    gv_ref[...] = z[:, C_GV:C_GV + D_GLA]
    gr_ref[...] = z[:, C_GR:C_GR + D_GLA]
    su_ref[...] = z[:, C_SU:C_SU + D_SSM]
    tail = z[:, C_TAIL:C_TAIL + LANES]
    lane = _iota((tt, LANES), 1)
    lf = jnp.where(lane < H_FOX, _log_sigmoid(tail + bfp_ref[...]), 0.0)
    lf_ref[...] = lf
    tri = jnp.where(_iota((tt, tt), 0) >= _iota((tt, tt), 1), 1.0, 0.0).astype(bf16)
    cs = _dot3_lhs01(tri, lf) + carry_ref[...]
    carry_ref[...] = cs[tt - 1:tt, :]
    pieces = _split3(cs * LOG2E)
    q_bias = sum(_dot(pc, pq_ref[i]) for i, pc in enumerate(pieces)) + oq_ref[...]
    k_bias = ok_ref[...] - sum(_dot(pc, pk_ref[i]) for i, pc in enumerate(pieces))
    low = lane < FOX_DH
    for hp in range(H_FOX // 2):
        qt = z[:, C_FQ + hp * LANES:C_FQ + (hp + 1) * LANES] * (LOG2E * FOX_DH ** -0.5)
        kt = z[:, C_FK + hp * LANES:C_FK + (hp + 1) * LANES]
        for hh in range(2):
            blk = slice((2 * hp + hh) * LANES, (2 * hp + hh + 1) * LANES)
            real = low if hh == 0 else jnp.logical_not(low)
            qa_ref[:, blk] = jnp.where(real, qt, q_bias[:, blk]).astype(bf16)
            ka_ref[:, blk] = jnp.where(real, kt, k_bias[:, blk]).astype(bf16)
    za = _dot(tail.astype(bf16), wa2_ref[...]) + ba_ref[...]
    la_ref[...] = _log_sigmoid(za) * (1.0 / GLA_TAU)


def _inproj(x, g_mix, w_p, bfp, wa2p, ba):
    t = x.shape[0]
    tt = TOK_TILE
    row = lambda n: pl.BlockSpec((tt, n), lambda i: (i, 0))
    const = lambda a: pl.BlockSpec(a.shape, lambda i: (0,) * a.ndim)
    outs = [
        (D_FOX, bf16), (H_FOX * LANES, bf16), (H_FOX * LANES, bf16), (D_FOX, bf16), (D_FOX, f32), (D_FOX, f32),
        (LANES, f32),
        (D_GK, f32), (D_GK, f32), (D_GLA, f32), (D_GK, f32), (D_GLA, f32), (D_SSM, f32),
    ]
    place_q, place_k, ones_q, ones_k = _fox_bias_lanes()
    consts = [g_mix, w_p, bfp, wa2p, ba, jnp.asarray(place_q, bf16), jnp.asarray(place_k, bf16),
              jnp.asarray(ones_q), jnp.asarray(ones_k)]
    return pl.pallas_call(
        _inproj_body,
        grid=(t // tt,),
        in_specs=[row(x.shape[1])] + [const(a) for a in consts],
        out_specs=[row(n) for n, _ in outs],
        out_shape=[jax.ShapeDtypeStruct((t, n), d) for n, d in outs],
        scratch_shapes=[pltpu.VMEM((1, LANES), f32)],
        compiler_params=_params(("arbitrary",), VMEM_LIMIT),
        name="inproj",
    )(x, *consts)


def _fox_prompt_body(qi_ref, ki_ref, qa_ref, qb_ref, ka_ref, kb_ref, v_ref, o_ref,
                     m_ref, l_ref, acc_ref):
    p = pl.program_id(1)
    qi = qi_ref[p]
    ki = ki_ref[p]
    tq, tk = qa_ref.shape[0], ka_ref.shape[0]

    @pl.when(ki == 0)
    def _init():
        m_ref[...] = jnp.full_like(m_ref, -jnp.inf)
        l_ref[...] = jnp.zeros_like(l_ref)
        acc_ref[...] = jnp.zeros_like(acc_ref)

    def step(diagonal):
        vext = jnp.concatenate([v_ref[...], jnp.ones((tk, LANES), bf16)], axis=1)
        scores = [_dot_nt(qa_ref[...], ka_ref[...]), _dot_nt(qb_ref[...], kb_ref[...])]
        for hh in range(2):
            s = scores[hh]
            if diagonal:
                s = jnp.where(_iota((tq, tk), 0) >= _iota((tq, tk), 1), s, -jnp.inf)
            m_old = m_ref[hh]
            m_new = jnp.maximum(m_old, jnp.max(s, axis=-1, keepdims=True))
            pr = jnp.exp2(s - jnp.tile(m_new, (1, tk // LANES)))
            alpha = jnp.exp2(m_old - m_new)
            pv = _dot(pr.astype(bf16), vext)
            l_ref[hh] = alpha * l_ref[hh] + pv[:, LANES:]
            m_ref[hh] = m_new
            acc_ref[hh] = alpha * acc_ref[hh] + pv[:, :LANES]

    @pl.when(ki < qi)
    def _off():
        step(False)

    @pl.when(ki == qi)
    def _diag():
        step(True)
        lane = _iota((1, LANES), 1)
        o_ref[...] = jnp.where(lane < FOX_DH, acc_ref[0] / l_ref[0], acc_ref[1] / l_ref[1])


def _fox_prompt(qaug, kaug, fvb, seq):
    tq = FOX_TILE
    nq = seq // tq
    qi_tbl = np.concatenate([np.full(i + 1, i) for i in range(nq)]).astype(np.int32)
    ki_tbl = np.concatenate([np.arange(i + 1) for i in range(nq)]).astype(np.int32)
    grid_spec = pltpu.PrefetchScalarGridSpec(
        num_scalar_prefetch=2,
        grid=(H_FOX // 2, len(qi_tbl)),
        in_specs=[
            pl.BlockSpec((tq, LANES), lambda h, p, qi, ki: (qi[p], 2 * h)),
            pl.BlockSpec((tq, LANES), lambda h, p, qi, ki: (qi[p], 2 * h + 1)),
            pl.BlockSpec((tq, LANES), lambda h, p, qi, ki: (ki[p], 2 * h)),
            pl.BlockSpec((tq, LANES), lambda h, p, qi, ki: (ki[p], 2 * h + 1)),
            pl.BlockSpec((tq, LANES), lambda h, p, qi, ki: (ki[p], h)),
        ],
        out_specs=pl.BlockSpec((tq, LANES), lambda h, p, qi, ki: (qi[p], h)),
        scratch_shapes=[pltpu.VMEM((2, tq, LANES), f32), pltpu.VMEM((2, tq, LANES), f32),
                        pltpu.VMEM((2, tq, LANES), f32)],
    )
    return pl.pallas_call(
        _fox_prompt_body,
        grid_spec=grid_spec,
        out_shape=jax.ShapeDtypeStruct((seq, D_FOX), f32),
        compiler_params=_params(("arbitrary", "arbitrary"), VMEM_LIMIT),
        name="fox_prompt",
    )(jnp.asarray(qi_tbl), jnp.asarray(ki_tbl), qaug, qaug, kaug, kaug, fvb)


def _expand_heads(x, rows):
    return jnp.concatenate(
        [jnp.broadcast_to(x[h:h + 1, :], (rows, x.shape[1])) for h in range(H_FOX)], axis=0)


def _fox_sample_body(pt_ref, q_ref, kn_ref, vn_ref, lfn_ref, *rest):
    del pt_ref
    npg = FOX_PAGES
    kp_refs = rest[0:npg]
    vp_refs = rest[npg:2 * npg]
    lfp_refs = rest[2 * npg:3 * npg]
    o_ref = rest[3 * npg]
    m_ref, l_ref, acc_ref, car_ref, qbd_ref, cq_ref = rest[3 * npg + 1:]
    j = pl.program_id(1)
    t = q_ref.shape[0]
    hq = H_FOX * t
    page = kp_refs[0].shape[-1]

    def update(s, weigh):
        m_old = m_ref[...]
        m_new = jnp.maximum(m_old, jnp.max(s, axis=-1, keepdims=True))
        pr = jnp.exp(s - m_new)
        alpha = jnp.exp(m_old - m_new)
        l_ref[...] = alpha * l_ref[...] + jnp.sum(pr, axis=-1, keepdims=True)
        m_ref[...] = m_new
        acc_ref[...] = alpha * acc_ref[...] + weigh(pr.astype(bf16))

    @pl.when(j == 0)
    def _new_tokens():
        m_ref[...] = jnp.full_like(m_ref, -jnp.inf)
        l_ref[...] = jnp.zeros_like(l_ref)
        acc_ref[...] = jnp.zeros_like(acc_ref)
        car_ref[...] = jnp.zeros_like(car_ref)
        q = q_ref[...]
        q6 = jnp.concatenate([q] * H_FOX, axis=0)
        keep = _div_pow2(_iota((hq, D_FOX), 0), t) == _div_pow2(_iota((hq, D_FOX), 1), FOX_DH)
        qbd_ref[...] = jnp.where(keep, q6, 0.0).astype(bf16)
        cum = lfn_ref[...]
        lane8 = _iota(cum.shape, 1)
        sh = 1
        while sh < t:
            cum = cum + jnp.where(lane8 >= sh, pltpu.roll(cum, sh, 1), 0.0)
            sh *= 2
        cum_rows = _expand_heads(cum, t)
        qpos = _mod_pow2(_iota((hq, LANES), 0), t)
        kpos = _iota((hq, LANES), 1)
        cq = jnp.sum(jnp.where(kpos == qpos, cum_rows, 0.0), axis=-1, keepdims=True)
        cq_ref[...] = cq
        zpad = jnp.zeros((LANES - t, D_FOX), f32)
        kn = jnp.concatenate([kn_ref[...], zpad], axis=0).astype(bf16)
        vn = jnp.concatenate([vn_ref[...], zpad], axis=0).astype(bf16)
        s = _dot_nt(qbd_ref[...], kn) + cq - cum_rows
        s = jnp.where(kpos <= qpos, s, -jnp.inf)
        update(s, lambda pr: _dot(pr, vn))

    upper = jnp.where(_iota((page, page), 0) > _iota((page, page), 1), 1.0, 0.0).astype(bf16)
    qbd = qbd_ref[...]
    qk = [_dot(qbd, kp_refs[g][...].reshape(D_FOX, page).astype(bf16)) for g in range(npg)]
    lf_all = jnp.concatenate([lfp_refs[g][...] for g in range(npg)], axis=0)
    suf_all = _dot3_rhs01(lf_all, upper)
    tot_all = jnp.sum(lf_all, axis=-1, keepdims=True)
    car = car_ref[...]
    scores = []
    for g in range(npg):
        scores.append(qk[g] + _expand_heads(suf_all[g * 8:(g + 1) * 8, :] + car, t))
        car = car + tot_all[g * 8:(g + 1) * 8, :]
    car_ref[...] = car

    def weigh_pages(pr):
        out = None
        for g in range(npg):
            vpt = vp_refs[g][...].reshape(D_FOX, page).astype(bf16)
            part = _dot_nt(pr[:, g * page:(g + 1) * page], vpt)
            out = part if out is None else out + part
        return out

    update(jnp.concatenate(scores, axis=1) + cq_ref[...], weigh_pages)

    @pl.when(j == pl.num_programs(1) - 1)
    def _finish():
        o48 = acc_ref[...] / l_ref[...]
        head = _div_pow2(_iota((t, D_FOX), 1), FOX_DH)
        out = jnp.zeros((t, D_FOX), f32)
        for h in range(H_FOX):
            out = out + jnp.where(head == h, o48[h * t:(h + 1) * t, :], 0.0)
        o_ref[...] = out


def _fox_sample(q_s, k_s, v_s, lfn_t, cache_k, cache_v, cache_lft, page_table, layer, t):
    nb, n_pages = page_table.shape
    page = cache_k.shape[-1]
    npg = FOX_PAGES
    hq = H_FOX * t
    seq_spec = pl.BlockSpec((t, D_FOX), lambda b, j, pt: (b, 0))

    def page_spec(shape, g):
        zeros = (0,) * len(shape)
        return pl.BlockSpec((None, None) + shape,
                            lambda b, j, pt: (layer, pt[b, n_pages - 1 - (j * npg + g)]) + zeros)

    grid_spec = pltpu.PrefetchScalarGridSpec(
        num_scalar_prefetch=1,
        grid=(nb, n_pages // npg),
        in_specs=[seq_spec, seq_spec, seq_spec,
                  pl.BlockSpec((None, 8, LANES), lambda b, j, pt: (b, 0, 0))]
        + [page_spec((H_FOX, FOX_DH, page), g) for g in range(npg)]
        + [page_spec((H_FOX, FOX_DH, page), g) for g in range(npg)]
        + [page_spec((8, page), g) for g in range(npg)],
        out_specs=seq_spec,
        scratch_shapes=[pltpu.VMEM((hq, 1), f32), pltpu.VMEM((hq, 1), f32),
                        pltpu.VMEM((hq, D_FOX), f32), pltpu.VMEM((8, 1), f32),
                        pltpu.VMEM((hq, D_FOX), bf16), pltpu.VMEM((hq, 1), f32)],
    )
    return pl.pallas_call(
        _fox_sample_body,
        grid_spec=grid_spec,
        out_shape=jax.ShapeDtypeStruct((nb * t, D_FOX), f32),
        compiler_params=_params(("arbitrary", "arbitrary"), VMEM_LIMIT),
        name="fox_sample",
    )(page_table, q_s, k_s, v_s, lfn_t, *([cache_k] * npg), *([cache_v] * npg), *([cache_lft] * npg))


def _gla_body(q_ref, k_ref, la_ref, v_ref, s0_ref, e_ref, mask_ref, o_ref, sout_ref, st_ref, *, chunk):
    i = pl.program_id(1)

    @pl.when(i == 0)
    def _init():
        st_ref[...] = s0_ref[...]

    c = chunk
    n_chunks = q_ref.shape[0] // c
    row = _iota((c, D_GK), 0)

    def one_chunk(ci, carry):
        r0 = pl.multiple_of(ci * c, c)
        q = q_ref[pl.ds(r0, c), :]
        k = k_ref[pl.ds(r0, c), :]
        v = v_ref[pl.ds(r0, c), :]
        b = la_ref[pl.ds(r0, c), :]
        sh = 1
        while sh < c:
            b = b + jnp.where(row >= sh, pltpu.roll(b, sh, 0), 0.0)
            sh *= 2
        pieces = []
        for s in range(c):
            d = jnp.where(row >= s, b - b[s:s + 1, :], -jnp.inf)
            pieces.append(q * k[s:s + 1, :] * jnp.exp(d))
        pst = jnp.concatenate(pieces, axis=0)
        hi = pst.astype(bf16)
        lo = (pst - hi.astype(f32)).astype(bf16)
        att = _dot(hi, e_ref[...]) + _dot(lo, e_ref[...])
        b_last = b[c - 1:c, :]
        kd = (k * jnp.exp(b_last - b)).astype(bf16)
        kv = _dot_tn(v.astype(bf16), kd) * mask_ref[...]
        qe = (q * jnp.exp(b)).astype(bf16)
        o = att[0:c, :] * v[0:1, :]
        for s in range(1, c):
            o = o + att[s * c:(s + 1) * c, :] * v[s:s + 1, :]
        st = st_ref[...]
        o_ref[pl.ds(r0, c), :] = o + _dot_nt(qe, st.astype(bf16))
        st_ref[...] = st * jnp.exp(b_last) + kv
        return carry

    lax.fori_loop(0, n_chunks, one_chunk, 0, unroll=min(GLA_UNROLL, n_chunks))

    @pl.when(i == pl.num_programs(1) - 1)
    def _finish():
        sout_ref[...] = st_ref[...]


def _gla(gq, gk, la, gv, s0t, e_mat, mask_t, *, row0, nseq, steps, rows_per_step, chunk):
    blk0 = row0 // rows_per_step
    rows = lambda n: pl.BlockSpec((rows_per_step, n), lambda s, i: (blk0 + s * steps + i, 0))
    orow = pl.BlockSpec((rows_per_step, D_GLA), lambda s, i: (s * steps + i, 0))
    st_spec = pl.BlockSpec((None, D_GLA, D_GK), lambda s, i: (s, 0, 0))
    const = lambda a: pl.BlockSpec(a.shape, lambda s, i: (0,) * a.ndim)
    return pl.pallas_call(
        functools.partial(_gla_body, chunk=chunk),
        grid=(nseq, steps),
        in_specs=[rows(D_GK), rows(D_GK), rows(D_GK), rows(D_GLA), st_spec, const(e_mat), const(mask_t)],
        out_specs=[orow, st_spec],
        out_shape=[jax.ShapeDtypeStruct((nseq * steps * rows_per_step, D_GLA), f32),
                   jax.ShapeDtypeStruct((nseq, D_GLA, D_GK), f32)],
        scratch_shapes=[pltpu.VMEM((D_GLA, D_GK), f32)],
        compiler_params=_params(("arbitrary", "arbitrary"), VMEM_LIMIT),
        name="gla",
    )(gq, gk, la, gv, s0t, e_mat, mask_t)


def _ssm_prep_body(are_ref, aim_ref, ldt_ref, bre_ref, bim_ref,
                   pre_ref, pim_ref, bbre_ref, bbim_ref):
    a_re = are_ref[...]
    a_im = aim_ref[...]
    dt = jnp.exp(ldt_ref[...])
    mag = jnp.exp(a_re * dt)
    abar_re = mag * jnp.cos(a_im * dt)
    abar_im = mag * jnp.sin(a_im * dt)
    den = a_re * a_re + a_im * a_im
    nr, ni = abar_re - 1.0, abar_im
    fr = (nr * a_re + ni * a_im) / den
    fi = (ni * a_re - nr * a_im) / den
    bbre_ref[...] = fr * bre_ref[...] - fi * bim_ref[...]
    bbim_ref[...] = fr * bim_ref[...] + fi * bre_ref[...]
    n = pre_ref.shape[0]
    pre_ref[0:1, :] = abar_re
    pim_ref[0:1, :] = abar_im
    m = 1
    while m < n:
        tr = pre_ref[m - 1:m, :]
        ti = pim_ref[m - 1:m, :]
        xr = pre_ref[0:m, :]
        xi = pim_ref[0:m, :]
        pre_ref[m:2 * m, :] = xr * tr - xi * ti
        pim_ref[m:2 * m, :] = xr * ti + xi * tr
        m *= 2


def _ssm_prep(a_re, a_im, log_dt, bt_re, bt_im, n_pow):
    outs = [(n_pow, D_STATE), (n_pow, D_STATE), (SSM_GC, D_STATE), (SSM_GC, D_STATE)]
    return pl.pallas_call(
        _ssm_prep_body,
        out_shape=[jax.ShapeDtypeStruct(s, f32) for s in outs],
        name="ssm_prep",
    )(a_re, a_im, log_dt, bt_re, bt_im)


def _ssm_body(u_ref, wb_ref, pre_ref, pim_ref, hin_re_ref, hin_im_ref, wc_ref, d_ref, wg_ref, bg_ref,
              o_ref, hre_ref, him_ref, car_re_ref, car_im_ref, *, tile_carry):
    i = pl.program_id(0)
    u = u_ref[...]
    tt = u.shape[0]
    grp = pre_ref.shape[0]
    d = _dot(u.astype(bf16), wb_ref[...])
    xr = d[:, :D_STATE]
    xi = d[:, D_STATE:]
    pos = _mod_pow2(_iota((tt, D_STATE), 0), grp)
    sh = 1
    while sh < grp:
        ar = pre_ref[sh - 1:sh, :]
        ai = pim_ref[sh - 1:sh, :]
        sr = jnp.where(pos >= sh, pltpu.roll(xr, sh, 0), 0.0)
        si = jnp.where(pos >= sh, pltpu.roll(xi, sh, 0), 0.0)
        xr, xi = xr + ar * sr - ai * si, xi + ar * si + ai * sr
        sh *= 2
    pr = pre_ref[...]
    pi = pim_ref[...]
    if tile_carry:
        @pl.when(i == 0)
        def _init():
            car_re_ref[...] = hin_re_ref[...]
            car_im_ref[...] = hin_im_ref[...]
        cr = car_re_ref[...]
        ci = car_im_ref[...]
        hr_groups, hi_groups = [], []
        for g in range(tt // grp):
            gr = xr[g * grp:(g + 1) * grp, :] + pr * cr - pi * ci
            gi = xi[g * grp:(g + 1) * grp, :] + pr * ci + pi * cr
            cr, ci = gr[grp - 1:grp, :], gi[grp - 1:grp, :]
            hr_groups.append(gr)
            hi_groups.append(gi)
        hr = jnp.concatenate(hr_groups, axis=0)
        hi = jnp.concatenate(hi_groups, axis=0)
        car_re_ref[...] = cr
        car_im_ref[...] = ci
        hre_ref[...] = cr
        him_ref[...] = ci
    else:
        h0r = hin_re_ref[...]
        h0i = hin_im_ref[...]
        prt = jnp.tile(pr, (tt // grp, 1))
        pit = jnp.tile(pi, (tt // grp, 1))
        hr = xr + prt * h0r - pit * h0i
        hi = xi + prt * h0i + pit * h0r
        hre_ref[...] = hr
        him_ref[...] = hi
    hcat = jnp.concatenate([hr, hi], axis=1).astype(bf16)
    y = _dot(hcat, wc_ref[...]) + d_ref[...] * u
    zg = jax.nn.gelu(y)
    o_ref[...] = (zg * jax.nn.sigmoid(_dot(zg.astype(bf16), wg_ref[...]) + bg_ref[...])).astype(bf16)


def _ssm(su, wb, p_re, p_im, hin_re, hin_im, wc, dvec, wglu, bglu, *, row0, n_rows, tile_carry):
    tt = TOK_TILE
    blk0 = row0 // tt
    const = lambda a: pl.BlockSpec(a.shape, lambda i: (0,) * a.ndim)
    if tile_carry:
        h_spec = const(hin_re)
        hout_spec = pl.BlockSpec((1, D_STATE), lambda i: (0, 0))
        hout_shape = jax.ShapeDtypeStruct((1, D_STATE), f32)
    else:
        h_spec = pl.BlockSpec((tt, D_STATE), lambda i: (i, 0))
        hout_spec = h_spec
        hout_shape = jax.ShapeDtypeStruct((n_rows, D_STATE), f32)
    return pl.pallas_call(
        functools.partial(_ssm_body, tile_carry=tile_carry),
        grid=(n_rows // tt,),
        in_specs=[pl.BlockSpec((tt, D_SSM), lambda i: (blk0 + i, 0)), const(wb), const(p_re), const(p_im),
                  h_spec, h_spec, const(wc), const(dvec), const(wglu), const(bglu)],
        out_specs=[pl.BlockSpec((tt, D_SSM), lambda i: (i, 0)), hout_spec, hout_spec],
        out_shape=[jax.ShapeDtypeStruct((n_rows, D_SSM), bf16), hout_shape, hout_shape],
        scratch_shapes=[pltpu.VMEM((1, D_STATE), f32), pltpu.VMEM((1, D_STATE), f32)],
        compiler_params=_params(("arbitrary",), VMEM_LIMIT),
        name="ssm",
    )(su, wb, p_re, p_im, hin_re, hin_im, wc, dvec, wglu, bglu)


def _outproj_body(fox_ref, gla_ref, gr_ref, ssm_ref, x_ref, fn_ref, gn_ref, avg_ref, w_ref, gf_ref,
                  x1_ref, xf_ref):
    fo = fox_ref[...]
    fo = fo * lax.rsqrt(jnp.mean(fo * fo, axis=-1, keepdims=True) + EPS) * fn_ref[...]
    go = gla_ref[...]
    ms = _dot3_rhs01(go * go, avg_ref[...]) * (1.0 / GLA_DV)
    go = go * lax.rsqrt(ms + EPS) * gn_ref[...] * jax.nn.silu(gr_ref[...])
    w = w_ref
    y = (_dot(fo.astype(bf16), w[0:D_FOX, :]) + _dot(go.astype(bf16), w[D_FOX:D_FOX + D_GLA, :])
         + _dot(ssm_ref[...], w[D_FOX + D_GLA:, :]))
    x1 = x_ref[...] + y
    x1_ref[...] = x1
    xf = x1 * lax.rsqrt(jnp.mean(x1 * x1, axis=-1, keepdims=True) + EPS) * gf_ref[...]
    xf_ref[...] = xf.astype(bf16)


def _outproj(fox_o, gla_o, gr, ssm_o, x, fox_norm, gla_norm6, avg, w_out, g_ffn):
    t, dm = x.shape
    tt = TOK_TILE
    row = lambda n: pl.BlockSpec((tt, n), lambda i: (i, 0))
    const = lambda a: pl.BlockSpec(a.shape, lambda i: (0,) * a.ndim)
    return pl.pallas_call(
        _outproj_body,
        grid=(t // tt,),
        in_specs=[row(D_FOX), row(D_GLA), row(D_GLA), row(D_SSM), row(dm),
                  const(fox_norm), const(gla_norm6), const(avg), const(w_out), const(g_ffn)],
        out_specs=[row(dm), row(dm)],
        out_shape=[jax.ShapeDtypeStruct((t, dm), f32), jax.ShapeDtypeStruct((t, dm), bf16)],
        compiler_params=_params(("arbitrary",), VMEM_LIMIT),
        name="outproj",
    )(fox_o, gla_o, gr, ssm_o, x, fox_norm, gla_norm6, avg, w_out, g_ffn)


def _topk_rows(s, k, ids=None):
    it = _iota(s.shape, 0) if ids is None else ids
    big = jnp.iinfo(jnp.int32).max
    vals, idxs = [], []
    for _ in range(k):
        m = jnp.max(s, axis=0, keepdims=True)
        idx = jnp.min(jnp.where(s == m, it, big), axis=0, keepdims=True)
        vals.append(m)
        idxs.append(idx)
        s = jnp.where(it == idx, -jnp.inf, s)
    return jnp.concatenate(vals, axis=0), jnp.concatenate(idxs, axis=0)


def _candidate_plan(kk):
    runs = [(a, kk // (a + 1)) for a in range(kk)]
    blocks = []
    for a, count in runs:
        for b0 in range(0, count - count % 8, 8):
            blocks.append([(a, b0, 8, 0)])
    open_blocks = []
    for a, count in sorted(runs, key=lambda r: -(r[1] % 8)):
        rem = count % 8
        if rem == 0:
            continue
        for blk in open_blocks:
            used = sum(seg[2] for seg in blk)
            if used + rem <= 8:
                blk.append((a, count - rem, rem, used))
                break
        else:
            open_blocks.append([(a, count - rem, rem, 0)])
    return blocks + open_blocks


def _candidate_ids(kk):
    ids = []
    for blk in _candidate_plan(kk):
        col = np.full(8, 1 << 30, np.int32)
        for a, b0, count, off in blk:
            col[off:off + count] = a * kk + b0 + np.arange(count)
        ids.append(col)
    return np.concatenate(ids)[:, None]


def _pair_candidates(v1, v2):
    kk, tt = v1.shape
    row = _iota((8, tt), 0)
    shifted = {}
    blocks = []
    for blk in _candidate_plan(kk):
        out = jnp.full((8, tt), -jnp.inf, f32)
        for a, b0, count, off in blk:
            if (b0, off) not in shifted:
                piece = v2[b0:b0 + 8, :]
                shifted[(b0, off)] = piece if off == 0 else pltpu.roll(piece, off, 0)
            seg = v1[a:a + 1, :] + shifted[(b0, off)]
            out = seg if count == 8 else jnp.where((row >= off) & (row < off + count), seg, out)
        blocks.append(out)
    return jnp.concatenate(blocks, axis=0)


def _take_rows(table, sel):
    return jnp.sum(jnp.where(_iota(table.shape, 0) == sel, table, 0), axis=0, keepdims=True)


def _peer_select(qt, keys1, keys2, cand_id):
    kk = PEER_TOPK
    half = PEER_DK // 2
    v1, i1 = _topk_rows(_dot(keys1, qt[:half, :].astype(bf16)), kk)
    v2, i2 = _topk_rows(_dot(keys2, qt[half:, :].astype(bf16)), kk)
    sc, pos = _topk_rows(_pair_candidates(v1, v2), kk, cand_id)
    sel1 = jnp.concatenate([_take_rows(i1, _div_pow2(pos[j:j + 1, :], kk)) for j in range(kk)], axis=0)
    sel2 = jnp.concatenate([_take_rows(i2, _mod_pow2(pos[j:j + 1, :], kk)) for j in range(kk)], axis=0)
    ex = jnp.exp(sc - sc[0:1, :])
    return sel1, sel2, ex / jnp.sum(ex, axis=0, keepdims=True)


def _peer_topk_body(xf_ref, wqt_ref, keys_ref, cid_ref, i1_ref, i2_ref, g_ref):
    qt = _dot_nt(wqt_ref[...], xf_ref[...])
    cand_id = cid_ref[...]
    picks = [_peer_select(qt[h * PEER_DK:(h + 1) * PEER_DK, :], keys_ref[h, 0], keys_ref[h, 1], cand_id)
             for h in range(PEER_HEADS)]
    i1_ref[...] = jnp.concatenate([p[0] for p in picks], axis=0).T
    i2_ref[...] = jnp.concatenate([p[1] for p in picks], axis=0).T
    g_ref[...] = jnp.concatenate([p[2] for p in picks], axis=0).T


def _peer_topk(xf, wqt, keys):
    t, dm = xf.shape
    tt = TOK_TILE
    const = lambda a: pl.BlockSpec(a.shape, lambda i: (0,) * a.ndim)
    slot = pl.BlockSpec((tt, PEER_SLOTS), lambda i: (i, 0))
    ids = _candidate_ids(PEER_TOPK)
    cand_id = jnp.asarray(np.broadcast_to(ids, (ids.shape[0], tt)))
    return pl.pallas_call(
        _peer_topk_body,
        grid=(t // tt,),
        in_specs=[pl.BlockSpec((tt, dm), lambda i: (i, 0)), const(wqt), const(keys), const(cand_id)],
        out_specs=[slot, slot, slot],
        out_shape=[jax.ShapeDtypeStruct((t, PEER_SLOTS), i32), jax.ShapeDtypeStruct((t, PEER_SLOTS), i32),
                   jax.ShapeDtypeStruct((t, PEER_SLOTS), f32)],
        compiler_params=_params(("arbitrary",), VMEM_LIMIT),
        name="peer_topk",
    )(xf, wqt, keys, cand_id)


def _peer_dense_body(xf_ref, i1_ref, i2_ref, g_ref, u_ref, v_ref, x1_ref, gfin_ref, o_ref,
                     gate_ref, acc_ref, *, final_norm):
    e = pl.program_id(1)
    tt = xf_ref.shape[0]
    nk = PEER_KEYS
    rows_per_tile = u_ref.shape[0] // nk

    @pl.when(e == 0)
    def _build_gates():
        acc_ref[...] = jnp.zeros_like(acc_ref)
        key_id = _iota((nk, PEER_SLOTS), 0)

        def token_group(gi, carry):
            t0 = pl.multiple_of(gi * GATE_GROUP, GATE_GROUP)
            r1 = i1_ref[pl.ds(t0, GATE_GROUP), :]
            r2 = i2_ref[pl.ds(t0, GATE_GROUP), :]
            gr = g_ref[pl.ds(t0, GATE_GROUP), :]
            mats = []
            for j in range(GATE_GROUP):
                p1 = jnp.where(r1[j:j + 1, :] == key_id, gr[j:j + 1, :], 0.0).astype(bf16)
                p2 = jnp.where(r2[j:j + 1, :] == key_id, 1.0, 0.0).astype(bf16)
                mats.append(_dot_nt(p1, p2))
            gate_ref[:, pl.ds(t0, GATE_GROUP), :] = jnp.swapaxes(jnp.stack(mats, axis=0), 0, 1).astype(bf16)
            return carry

        lax.fori_loop(0, tt // GATE_GROUP, token_group, 0, unroll=4)

    a = _dot_nt(xf_ref[...], u_ref[...])
    act = jax.nn.gelu(a)
    ws = []
    for r in range(rows_per_tile):
        gt = gate_ref[e * rows_per_tile + r]
        ws.append((gt.astype(f32) * act[:, r * nk:(r + 1) * nk]).astype(bf16))
    acc_ref[...] += _dot(jnp.concatenate(ws, axis=1), v_ref[...])

    @pl.when(e == pl.num_programs(1) - 1)
    def _finish():
        y = x1_ref[...] + acc_ref[...]
        if final_norm:
            y = y * lax.rsqrt(jnp.mean(y * y, axis=-1, keepdims=True) + EPS) * gfin_ref[...]
        o_ref[...] = y


def _peer_dense(xf, i1, i2, g, u_tab, v_tab, layer, x1, g_final, final_norm):
    t, dm = xf.shape
    tt = PEER_TOK
    et = PEER_ETILE
    tok = lambda n: pl.BlockSpec((tt, n), lambda i, e: (i, 0))
    tab = pl.BlockSpec((None, et, dm), lambda i, e: (layer, e, 0))
    return pl.pallas_call(
        functools.partial(_peer_dense_body, final_norm=final_norm),
        grid=(t // tt, u_tab.shape[1] // et),
        in_specs=[tok(dm), tok(PEER_SLOTS), tok(PEER_SLOTS), tok(PEER_SLOTS), tab, tab, tok(dm),
                  pl.BlockSpec(g_final.shape, lambda i, e: (0, 0))],
        out_specs=tok(dm),
        out_shape=jax.ShapeDtypeStruct((t, dm), f32),
        scratch_shapes=[pltpu.VMEM((PEER_KEYS, tt, PEER_KEYS), bf16), pltpu.VMEM((tt, dm), f32)],
        compiler_params=_params(("arbitrary", "arbitrary"), VMEM_LIMIT),
        name="peer_dense",
    )(xf, i1, i2, g, u_tab, v_tab, x1, g_final)


def _pack_w_in(w_in):
    dm = w_in.shape[0]
    sizes = (D_FOX, D_FOX, D_FOX, H_FOX, D_GK, D_GK, D_GLA, GLA_RANK, D_GLA, D_SSM)
    pts = np.cumsum(sizes)[:-1].tolist()
    fq, fk, fv, ff, gq, gk, gv, ga, gr, su = jnp.split(w_in, pts, axis=1)
    z = lambda n: jnp.zeros((dm, n), w_in.dtype)
    tail = jnp.concatenate([ff, z(TAIL_GA - H_FOX), ga, z(LANES - TAIL_GA - GLA_RANK)], axis=1)
    packed = jnp.concatenate([fq, fk, fv, gq, z(C_GK - C_GQ - D_GK), gk, z(C_GV - C_GK - D_GK),
                              gv, gr, su, tail], axis=1)
    assert packed.shape[1] == D_INP
    return packed.astype(bf16)


def _block_diag(blocks):
    g, a, b = blocks.shape
    eye = jnp.eye(g, dtype=blocks.dtype)
    return (blocks[:, :, None, :] * eye[:, None, :, None]).reshape(g * a, g * b)


def _head_match(rows, row_w, cols, col_w):
    return (np.arange(rows)[:, None] // row_w == np.arange(cols)[None, :] // col_w)


def kernel(x_prompt, x_sample, cache_fox_k, cache_fox_v, cache_fox_logf, state_gla, state_ssm_re, state_ssm_im, page_table, g_mix, w_in, fox_bf, fox_norm, gla_wa2, gla_ba, gla_norm, ssm_a_re, ssm_a_im, ssm_log_dt, ssm_b_re, ssm_b_im, ssm_c_re, ssm_c_im, ssm_d, ssm_w_glu, ssm_b_glu, w_out, g_ffn, peer_wq, peer_keys, peer_u, peer_v, g_final):
    bp, seq, dm = x_prompt.shape
    bd, tdec, _ = x_sample.shape
    depth = w_in.shape[0]
    assert bp == 1 and seq % FOX_TILE == 0 and seq % GLA_STEP == 0
    n_s = bd * tdec
    assert n_s == TOK_TILE and tdec == 8
    n_tok = seq + n_s
    n_pool, page = cache_fox_k.shape[1], cache_fox_k.shape[2]
    assert page == LANES and page_table.shape[1] % FOX_PAGES == 0

    n_pad = -n_tok % PEER_TOK
    pad_rows = lambda a: jnp.zeros((n_pad, a.shape[1]), a.dtype)
    x = jnp.concatenate([x_prompt.reshape(seq, dm), x_sample.reshape(n_s, dm), jnp.zeros((n_pad, dm), f32)], axis=0)
    ck = jnp.transpose(cache_fox_k, (0, 1, 3, 4, 2))
    cv = jnp.transpose(cache_fox_v, (0, 1, 3, 4, 2))
    clft = jnp.pad(jnp.swapaxes(cache_fox_logf, 2, 3), ((0, 0), (0, 0), (0, 8 - H_FOX), (0, 0)))

    e_mat = jnp.asarray(_head_match(D_GK, GLA_DK, D_GLA, GLA_DV), bf16)
    mask_t = jnp.asarray(_head_match(D_GLA, GLA_DV, D_GK, GLA_DK), f32)
    avg = jnp.asarray(_head_match(D_GLA, GLA_DV, D_GLA, GLA_DV), bf16)
    zeros_state = jnp.zeros((1, D_STATE), f32)
    u_tab = peer_u.astype(bf16)
    v_tab = peer_v.astype(bf16)

    outs_p, outs_s = [], []
    for l in range(depth):
        w_p = _pack_w_in(w_in[l])
        bfp = jnp.pad(fox_bf[l], (0, LANES - H_FOX)).reshape(1, LANES)
        wa2p = jnp.pad(gla_wa2[l], ((TAIL_GA, LANES - TAIL_GA - GLA_RANK), (0, 0))).astype(bf16)
        ba = gla_ba[l].reshape(1, D_GK)
        a_re = ssm_a_re[l].reshape(1, D_STATE)
        a_im = ssm_a_im[l].reshape(1, D_STATE)
        ldt = jnp.repeat(ssm_log_dt[l], SSM_P).reshape(1, D_STATE)
        bt_re = jnp.transpose(ssm_b_re[l], (2, 0, 1)).reshape(SSM_GC, D_STATE)
        bt_im = jnp.transpose(ssm_b_im[l], (2, 0, 1)).reshape(SSM_GC, D_STATE)
        p_re, p_im, bb_re, bb_im = _ssm_prep(a_re, a_im, ldt, bt_re, bt_im, tdec)
        to_blocks = lambda bb: jnp.transpose(bb.reshape(SSM_GC, SSM_G, SSM_P), (1, 0, 2))
        wb = jnp.concatenate([_block_diag(to_blocks(bb_re)), _block_diag(to_blocks(bb_im))], axis=1).astype(bf16)
        c_blocks = lambda cc: jnp.transpose(cc, (0, 2, 1))
        wc = jnp.concatenate([_block_diag(c_blocks(ssm_c_re[l])), -_block_diag(c_blocks(ssm_c_im[l]))],
                             axis=0).astype(bf16)
        dvec = ssm_d[l].reshape(1, D_SSM)
        wglu = ssm_w_glu[l].astype(bf16)
        bglu = ssm_b_glu[l].reshape(1, D_SSM)
        wqt = jnp.transpose(peer_wq[l].reshape(dm, PEER_HEADS * PEER_DK)).astype(bf16)
        keys = peer_keys[l].astype(bf16)

        (fq, qaug, kaug, fvb, fk, fv, lf, gq, gk, gv, la, gr, su) = _inproj(
            x, g_mix[l].reshape(1, dm), w_p, bfp, wa2p, ba)

        fox_p = _fox_prompt(qaug, kaug, fvb, seq)
        lfn_t = jnp.pad(jnp.swapaxes(lf[seq:n_tok, :8].reshape(bd, tdec, 8), 1, 2), ((0, 0), (0, 0), (0, LANES - tdec)))
        fox_s = _fox_sample(fq[seq:n_tok].astype(f32), fk[seq:n_tok], fv[seq:n_tok], lfn_t, ck, cv, clft, page_table, l, tdec)
        fox_o = jnp.concatenate([fox_p, fox_s, pad_rows(fox_s)], axis=0)

        gla_p, st_p = _gla(gq, gk, la, gv, jnp.zeros((1, D_GLA, D_GK), f32), e_mat, mask_t,
                           row0=0, nseq=1, steps=seq // GLA_STEP, rows_per_step=GLA_STEP, chunk=GLA_CHUNK)
        s0 = state_gla[l].astype(f32)
        s0t = jnp.transpose(s0, (0, 1, 3, 2))
        s0t = (s0t[:, :, :, None, :] * jnp.eye(H_GLA, dtype=f32)[None, :, None, :, None]).reshape(bd, D_GLA, D_GK)
        gla_s, st_s = _gla(gq, gk, la, gv, s0t, e_mat, mask_t,
                           row0=seq, nseq=bd, steps=1, rows_per_step=tdec, chunk=tdec)
        gla_o = jnp.concatenate([gla_p, gla_s, pad_rows(gla_s)], axis=0)

        def unpack_state(st):
            st = st.reshape(-1, H_GLA, GLA_DV, H_GLA, GLA_DK)
            st = jnp.stack([st[:, h, :, h, :] for h in range(H_GLA)], axis=1)
            return jnp.transpose(st, (0, 1, 3, 2))

        ssm_p, hre_p, him_p = _ssm(su, wb, p_re, p_im, zeros_state, zeros_state, wc, dvec, wglu, bglu,
                                   row0=0, n_rows=seq, tile_carry=True)
        h0r = jnp.repeat(state_ssm_re[l].reshape(bd, D_STATE).astype(f32), tdec, axis=0)
        h0i = jnp.repeat(state_ssm_im[l].reshape(bd, D_STATE).astype(f32), tdec, axis=0)
        ssm_s, hre_s, him_s = _ssm(su, wb, p_re, p_im, h0r, h0i, wc, dvec, wglu, bglu,
                                   row0=seq, n_rows=n_s, tile_carry=False)
        ssm_o = jnp.concatenate([ssm_p, ssm_s, pad_rows(ssm_s)], axis=0)

        x1, xf = _outproj(fox_o, gla_o, gr, ssm_o, x, fox_norm[l].reshape(1, D_FOX),
                          jnp.tile(gla_norm[l], H_GLA).reshape(1, D_GLA), avg,
                          w_out[l].astype(bf16), g_ffn[l].reshape(1, dm))
        i1, i2, gate = _peer_topk(xf, wqt, keys)
        x = _peer_dense(xf, i1, i2, gate, u_tab, v_tab, l, x1, g_final.reshape(1, dm), l == depth - 1)

        outs_p.append((fk[:seq].reshape(bp, seq, H_FOX, FOX_DH), fv[:seq].reshape(bp, seq, H_FOX, FOX_DH),
                       lf[:seq, :H_FOX].reshape(bp, seq, H_FOX), unpack_state(st_p),
                       hre_p.reshape(bp, SSM_G, SSM_P), him_p.reshape(bp, SSM_G, SSM_P)))
        outs_s.append((fk[seq:n_tok].reshape(bd, tdec, H_FOX, FOX_DH), fv[seq:n_tok].reshape(bd, tdec, H_FOX, FOX_DH),
                       lf[seq:n_tok, :H_FOX].reshape(bd, tdec, H_FOX), unpack_state(st_s),
                       hre_s[tdec - 1::tdec].reshape(bd, SSM_G, SSM_P), him_s[tdec - 1::tdec].reshape(bd, SSM_G, SSM_P)))

    p_stk = [jnp.stack(t) for t in zip(*outs_p)]
    s_stk = [jnp.stack(t) for t in zip(*outs_s)]
    y_prompt = x[:seq].reshape(bp, seq, dm)
    y_sample = x[seq:n_tok].reshape(bd, tdec, dm)
    return (y_prompt, y_sample, *p_stk, *s_stk)
```

```python
import functools

import numpy as np
import jax
import jax.numpy as jnp
from jax import lax
from jax.experimental import pallas as pl
from jax.experimental.pallas import tpu as pltpu

f32 = jnp.float32
bf16 = jnp.bfloat16
i32 = jnp.int32

EPS = 1e-6
LOG2E = 1.4426950408889634
LANES = 128
VMEM_LIMIT = 56 * 1024 * 1024

H_FOX, FOX_DH = 6, 64
D_FOX = H_FOX * FOX_DH
H_GLA, GLA_DK, GLA_DV = 6, 32, 64
D_GK = H_GLA * GLA_DK
D_GLA = H_GLA * GLA_DV
GLA_RANK = 16
GLA_TAU = 16.0
SSM_G, SSM_GC, SSM_P = 16, 16, 64
D_SSM = SSM_G * SSM_GC
D_STATE = SSM_G * SSM_P
PEER_KEYS, PEER_HEADS, PEER_TOPK, PEER_DK = 128, 8, 16, 256
PEER_SLOTS = PEER_HEADS * PEER_TOPK

C_FQ, C_FK, C_FV = 0, 384, 768
C_GQ, C_GK = 1152, 1408
C_GV, C_GR, C_SU, C_TAIL = 1664, 2048, 2432, 2688
D_INP = 2816
TAIL_FF, TAIL_GA = 0, 8
PIECE_LANES = 8

TOK_TILE = 256
FOX_TILE = 1024
FOX_PAGES = 32
GLA_STEP = 512
GLA_CHUNK = 16
PEER_TOK = 512
PEER_ETILE = 2048
GATE_GROUP = 16
GLA_UNROLL = 8


def _dot(a, b):
    return jnp.dot(a, b, preferred_element_type=f32)


def _dot_nt(a, b):
    return lax.dot_general(a, b, (((1,), (1,)), ((), ())), preferred_element_type=f32)


def _dot_tn(a, b):
    return lax.dot_general(a, b, (((0,), (0,)), ((), ())), preferred_element_type=f32)


def _split3(x):
    h = x.astype(bf16)
    r = x - h.astype(f32)
    m = r.astype(bf16)
    l = (r - m.astype(f32)).astype(bf16)
    return h, m, l


def _dot3_lhs01(sel, x):
    h, m, l = _split3(x)
    return _dot(sel, h) + _dot(sel, m) + _dot(sel, l)


def _dot3_rhs01(x, sel):
    h, m, l = _split3(x)
    return _dot(h, sel) + _dot(m, sel) + _dot(l, sel)


def _log_sigmoid(x):
    return -(jnp.maximum(-x, 0.0) + jnp.log1p(jnp.exp(-jnp.abs(x))))


def _iota(shape, dim):
    return lax.broadcasted_iota(i32, shape, dim)


def _div_pow2(x, n):
    assert n & (n - 1) == 0
    return jnp.right_shift(x, n.bit_length() - 1)


def _mod_pow2(x, n):
    assert n & (n - 1) == 0
    return jnp.bitwise_and(x, n - 1)


def _params(sem, vmem=None):
    return pltpu.CompilerParams(dimension_semantics=sem, vmem_limit_bytes=vmem)


def _fox_bias_lanes():
    nq = H_FOX * LANES
    place = np.zeros((LANES, 2 * nq), np.float32)
    ones = np.zeros((1, 2 * nq), np.float32)
    for j in range(H_FOX):
        base = j * LANES + (FOX_DH if j % 2 == 0 else 0)
        for p in range(3):
            place[PIECE_LANES * p + j, base + p] = 1.0
            ones[0, nq + base + p] = 1.0
            ones[0, base + 3 + p] = 1.0
            place[PIECE_LANES * p + j, nq + base + 3 + p] = -1.0
    return place, ones


def _inproj_body(x_ref, g_ref, w_ref, bfp_ref, wa2_ref, ba_ref, place_ref, ones_ref,
                 fq_ref, qa_ref, ka_ref, fvb_ref, fk_ref, fv_ref, lf_ref,
                 gq_ref, gk_ref, gv_ref, la_ref, gr_ref, su_ref, carry_ref):
    @pl.when(pl.program_id(0) == 0)
    def _init():
        carry_ref[...] = jnp.zeros_like(carry_ref)

    x = x_ref[...]
    tt = x.shape[0]
    xn = x * lax.rsqrt(jnp.mean(x * x, axis=-1, keepdims=True) + EPS) * g_ref[...]
    z = _dot(xn.astype(bf16), w_ref[...])
    fq_ref[...] = (z[:, C_FQ:C_FQ + D_FOX] * (FOX_DH ** -0.5)).astype(bf16)
    fk = z[:, C_FK:C_FK + D_FOX]
    fv = z[:, C_FV:C_FV + D_FOX]
    fk_ref[...] = fk
    fv_ref[...] = fv
    fvb_ref[...] = fv.astype(bf16)
    gq_ref[...] = z[:, C_GQ:C_GQ + D_GK] * (GLA_DK ** -0.5)
    gk_ref[...] = z[:, C_GK:C_GK + D_GK]
    gv_ref[...] = z[:, C_GV:C_GV + D_GLA]
    gr_ref[...] = z[:, C_GR:C_GR + D_GLA]
    su_ref[...] = z[:, C_SU:C_SU + D_SSM]
    tail = z[:, C_TAIL:C_TAIL + LANES]
    lane = _iota((tt, LANES), 1)
    lf = jnp.where(lane < H_FOX, _log_sigmoid(tail + bfp_ref[...]), 0.0)
    lf_ref[...] = lf
    tri = jnp.where(_iota((tt, tt), 0) >= _iota((tt, tt), 1), 1.0, 0.0).astype(bf16)
    cs = _dot3_lhs01(tri, lf) + carry_ref[...]
    carry_ref[...] = cs[tt - 1:tt, :]
    pieces = [pc.astype(f32) for pc in _split3(cs * LOG2E)]
    packed = pieces[0] + pltpu.roll(pieces[1], PIECE_LANES, 1) + pltpu.roll(pieces[2], 2 * PIECE_LANES, 1)
    bias = _dot(packed.astype(bf16), place_ref[...]) + ones_ref[...]
    q_bias = bias[:, :H_FOX * LANES]
    k_bias = bias[:, H_FOX * LANES:]
    low = lane < FOX_DH
    for hp in range(H_FOX // 2):
        qt = z[:, C_FQ + hp * LANES:C_FQ + (hp + 1) * LANES] * (LOG2E * FOX_DH ** -0.5)
        kt = z[:, C_FK + hp * LANES:C_FK + (hp + 1) * LANES]
        for hh in range(2):
            blk = slice((2 * hp + hh) * LANES, (2 * hp + hh + 1) * LANES)
            real = low if hh == 0 else jnp.logical_not(low)
            qa_ref[:, blk] = jnp.where(real, qt, q_bias[:, blk]).astype(bf16)
            ka_ref[:, blk] = jnp.where(real, kt, k_bias[:, blk]).astype(bf16)
    za = _dot(tail.astype(bf16), wa2_ref[...]) + ba_ref[...]
    la_ref[...] = _log_sigmoid(za) * (1.0 / GLA_TAU)


def _inproj(x, g_mix, w_p, bfp, wa2p, ba):
    t = x.shape[0]
    tt = TOK_TILE
    row = lambda n: pl.BlockSpec((tt, n), lambda i: (i, 0))
    const = lambda a: pl.BlockSpec(a.shape, lambda i: (0,) * a.ndim)
    outs = [
        (D_FOX, bf16), (H_FOX * LANES, bf16), (H_FOX * LANES, bf16), (D_FOX, bf16), (D_FOX, f32), (D_FOX, f32),
        (LANES, f32),
        (D_GK, f32), (D_GK, f32), (D_GLA, f32), (D_GK, f32), (D_GLA, f32), (D_SSM, f32),
    ]
    place, ones = _fox_bias_lanes()
    consts = [g_mix, w_p, bfp, wa2p, ba, jnp.asarray(place, bf16), jnp.asarray(ones)]
    return pl.pallas_call(
        _inproj_body,
        grid=(t // tt,),
        in_specs=[row(x.shape[1])] + [const(a) for a in consts],
        out_specs=[row(n) for n, _ in outs],
        out_shape=[jax.ShapeDtypeStruct((t, n), d) for n, d in outs],
        scratch_shapes=[pltpu.VMEM((1, LANES), f32)],
        compiler_params=_params(("arbitrary",), VMEM_LIMIT),
        name="inproj",
    )(x, *consts)


def _fox_prompt_body(qi_ref, ki_ref, qa_ref, qb_ref, ka_ref, kb_ref, v_ref, o_ref,
                     m_ref, l_ref, acc_ref):
    p = pl.program_id(1)
    qi = qi_ref[p]
    ki = ki_ref[p]
    tq, tk = qa_ref.shape[0], ka_ref.shape[0]

    @pl.when(ki == 0)
    def _init():
        m_ref[...] = jnp.full_like(m_ref, -jnp.inf)
        l_ref[...] = jnp.zeros_like(l_ref)
        acc_ref[...] = jnp.zeros_like(acc_ref)

    def step(diagonal):
        vext = jnp.concatenate([v_ref[...], jnp.ones((tk, LANES), bf16)], axis=1)
        scores = [_dot_nt(qa_ref[...], ka_ref[...]), _dot_nt(qb_ref[...], kb_ref[...])]
        for hh in range(2):
            s = scores[hh]
            if diagonal:
                s = jnp.where(_iota((tq, tk), 0) >= _iota((tq, tk), 1), s, -jnp.inf)
            m_old = m_ref[hh]
            m_new = jnp.maximum(m_old, jnp.max(s, axis=-1, keepdims=True))
            pr = jnp.exp2(s - jnp.tile(m_new, (1, tk // LANES)))
            alpha = jnp.exp2(m_old - m_new)
            pv = _dot(pr.astype(bf16), vext)
            l_ref[hh] = alpha * l_ref[hh] + pv[:, LANES:]
            m_ref[hh] = m_new
            acc_ref[hh] = alpha * acc_ref[hh] + pv[:, :LANES]

    @pl.when(ki < qi)
    def _off():
        step(False)

    @pl.when(ki == qi)
    def _diag():
        step(True)
        lane = _iota((1, LANES), 1)
        o_ref[...] = jnp.where(lane < FOX_DH, acc_ref[0] / l_ref[0], acc_ref[1] / l_ref[1])


def _fox_prompt(qaug, kaug, fvb, seq):
    tq = FOX_TILE
    nq = seq // tq
    qi_tbl = np.concatenate([np.full(i + 1, i) for i in range(nq)]).astype(np.int32)
    ki_tbl = np.concatenate([np.arange(i + 1) for i in range(nq)]).astype(np.int32)
    grid_spec = pltpu.PrefetchScalarGridSpec(
        num_scalar_prefetch=2,
        grid=(H_FOX // 2, len(qi_tbl)),
        in_specs=[
            pl.BlockSpec((tq, LANES), lambda h, p, qi, ki: (qi[p], 2 * h)),
            pl.BlockSpec((tq, LANES), lambda h, p, qi, ki: (qi[p], 2 * h + 1)),
            pl.BlockSpec((tq, LANES), lambda h, p, qi, ki: (ki[p], 2 * h)),
            pl.BlockSpec((tq, LANES), lambda h, p, qi, ki: (ki[p], 2 * h + 1)),
            pl.BlockSpec((tq, LANES), lambda h, p, qi, ki: (ki[p], h)),
        ],
        out_specs=pl.BlockSpec((tq, LANES), lambda h, p, qi, ki: (qi[p], h)),
        scratch_shapes=[pltpu.VMEM((2, tq, LANES), f32), pltpu.VMEM((2, tq, LANES), f32),
                        pltpu.VMEM((2, tq, LANES), f32)],
    )
    return pl.pallas_call(
        _fox_prompt_body,
        grid_spec=grid_spec,
        out_shape=jax.ShapeDtypeStruct((seq, D_FOX), f32),
        compiler_params=_params(("arbitrary", "arbitrary"), VMEM_LIMIT),
        name="fox_prompt",
    )(jnp.asarray(qi_tbl), jnp.asarray(ki_tbl), qaug, qaug, kaug, kaug, fvb)


def _expand_heads(x, rows):
    return jnp.concatenate(
        [jnp.broadcast_to(x[h:h + 1, :], (rows, x.shape[1])) for h in range(H_FOX)], axis=0)


def _fox_sample_body(pt_ref, q_ref, kn_ref, vn_ref, lfn_ref, *rest):
    del pt_ref
    npg = FOX_PAGES
    kp_refs = rest[0:npg]
    vp_refs = rest[npg:2 * npg]
    lfp_refs = rest[2 * npg:3 * npg]
    o_ref = rest[3 * npg]
    m_ref, l_ref, acc_ref, car_ref, qbd_ref, cq_ref = rest[3 * npg + 1:]
    j = pl.program_id(1)
    t = q_ref.shape[0]
    hq = H_FOX * t
    page = kp_refs[0].shape[-1]

    def update(s, weigh):
        m_old = m_ref[...]
        m_new = jnp.maximum(m_old, jnp.max(s, axis=-1, keepdims=True))
        pr = jnp.exp(s - m_new)
        alpha = jnp.exp(m_old - m_new)
        l_ref[...] = alpha * l_ref[...] + jnp.sum(pr, axis=-1, keepdims=True)
        m_ref[...] = m_new
        acc_ref[...] = alpha * acc_ref[...] + weigh(pr.astype(bf16))

    @pl.when(j == 0)
    def _new_tokens():
        m_ref[...] = jnp.full_like(m_ref, -jnp.inf)
        l_ref[...] = jnp.zeros_like(l_ref)
        acc_ref[...] = jnp.zeros_like(acc_ref)
        car_ref[...] = jnp.zeros_like(car_ref)
        q = q_ref[...]
        q6 = jnp.concatenate([q] * H_FOX, axis=0)
        keep = _div_pow2(_iota((hq, D_FOX), 0), t) == _div_pow2(_iota((hq, D_FOX), 1), FOX_DH)
        qbd_ref[...] = jnp.where(keep, q6, 0.0).astype(bf16)
        cum = lfn_ref[...]
        lane8 = _iota(cum.shape, 1)
        sh = 1
        while sh < t:
            cum = cum + jnp.where(lane8 >= sh, pltpu.roll(cum, sh, 1), 0.0)
            sh *= 2
        cum_rows = _expand_heads(cum, t)
        qpos = _mod_pow2(_iota((hq, LANES), 0), t)
        kpos = _iota((hq, LANES), 1)
        cq = jnp.sum(jnp.where(kpos == qpos, cum_rows, 0.0), axis=-1, keepdims=True)
        cq_ref[...] = cq
        zpad = jnp.zeros((LANES - t, D_FOX), f32)
        kn = jnp.concatenate([kn_ref[...], zpad], axis=0).astype(bf16)
        vn = jnp.concatenate([vn_ref[...], zpad], axis=0).astype(bf16)
        s = _dot_nt(qbd_ref[...], kn) + cq - cum_rows
        s = jnp.where(kpos <= qpos, s, -jnp.inf)
        update(s, lambda pr: _dot(pr, vn))

    upper = jnp.where(_iota((page, page), 0) > _iota((page, page), 1), 1.0, 0.0).astype(bf16)
    qbd = qbd_ref[...]
    qk = [_dot(qbd, kp_refs[g][...].reshape(D_FOX, page).astype(bf16)) for g in range(npg)]
    lf_all = jnp.concatenate([lfp_refs[g][...] for g in range(npg)], axis=0)
    suf_all = _dot3_rhs01(lf_all, upper)
    tot_all = jnp.sum(lf_all, axis=-1, keepdims=True)
    car = car_ref[...]
    scores = []
    for g in range(npg):
        scores.append(qk[g] + _expand_heads(suf_all[g * 8:(g + 1) * 8, :] + car, t))
        car = car + tot_all[g * 8:(g + 1) * 8, :]
    car_ref[...] = car

    def weigh_pages(pr):
        out = None
        for g in range(npg):
            vpt = vp_refs[g][...].reshape(D_FOX, page).astype(bf16)
            part = _dot_nt(pr[:, g * page:(g + 1) * page], vpt)
            out = part if out is None else out + part
        return out

    update(jnp.concatenate(scores, axis=1) + cq_ref[...], weigh_pages)

    @pl.when(j == pl.num_programs(1) - 1)
    def _finish():
        o48 = acc_ref[...] / l_ref[...]
        head = _div_pow2(_iota((t, D_FOX), 1), FOX_DH)
        out = jnp.zeros((t, D_FOX), f32)
        for h in range(H_FOX):
            out = out + jnp.where(head == h, o48[h * t:(h + 1) * t, :], 0.0)
        o_ref[...] = out


def _fox_sample(q_s, k_s, v_s, lfn_t, cache_k, cache_v, cache_lft, page_table, layer, t):
    nb, n_pages = page_table.shape
    page = cache_k.shape[-1]
    npg = FOX_PAGES
    hq = H_FOX * t
    seq_spec = pl.BlockSpec((t, D_FOX), lambda b, j, pt: (b, 0))

    def page_spec(shape, g):
        zeros = (0,) * len(shape)
        return pl.BlockSpec((None, None) + shape,
                            lambda b, j, pt: (layer, pt[b, n_pages - 1 - (j * npg + g)]) + zeros)

    grid_spec = pltpu.PrefetchScalarGridSpec(
        num_scalar_prefetch=1,
        grid=(nb, n_pages // npg),
        in_specs=[seq_spec, seq_spec, seq_spec,
                  pl.BlockSpec((None, 8, LANES), lambda b, j, pt: (b, 0, 0))]
        + [page_spec((H_FOX, FOX_DH, page), g) for g in range(npg)]
        + [page_spec((H_FOX, FOX_DH, page), g) for g in range(npg)]
        + [page_spec((8, page), g) for g in range(npg)],
        out_specs=seq_spec,
        scratch_shapes=[pltpu.VMEM((hq, 1), f32), pltpu.VMEM((hq, 1), f32),
                        pltpu.VMEM((hq, D_FOX), f32), pltpu.VMEM((8, 1), f32),
                        pltpu.VMEM((hq, D_FOX), bf16), pltpu.VMEM((hq, 1), f32)],
    )
    return pl.pallas_call(
        _fox_sample_body,
        grid_spec=grid_spec,
        out_shape=jax.ShapeDtypeStruct((nb * t, D_FOX), f32),
        compiler_params=_params(("arbitrary", "arbitrary"), VMEM_LIMIT),
        name="fox_sample",
    )(page_table, q_s, k_s, v_s, lfn_t, *([cache_k] * npg), *([cache_v] * npg), *([cache_lft] * npg))


def _gla_body(q_ref, k_ref, la_ref, v_ref, s0_ref, e_ref, mask_ref, o_ref, sout_ref, st_ref, *, chunk):
    i = pl.program_id(1)

    @pl.when(i == 0)
    def _init():
        st_ref[...] = s0_ref[...]

    c = chunk
    n_chunks = q_ref.shape[0] // c
    row = _iota((c, D_GK), 0)

    def one_chunk(ci, carry):
        r0 = pl.multiple_of(ci * c, c)
        q = q_ref[pl.ds(r0, c), :]
        k = k_ref[pl.ds(r0, c), :]
        v = v_ref[pl.ds(r0, c), :]
        b = la_ref[pl.ds(r0, c), :]
        sh = 1
        while sh < c:
            b = b + jnp.where(row >= sh, pltpu.roll(b, sh, 0), 0.0)
            sh *= 2
        pieces = []
        for s in range(c):
            d = jnp.where(row >= s, b - b[s:s + 1, :], -jnp.inf)
            pieces.append(q * k[s:s + 1, :] * jnp.exp(d))
        pst = jnp.concatenate(pieces, axis=0)
        hi = pst.astype(bf16)
        lo = (pst - hi.astype(f32)).astype(bf16)
        att = _dot(hi, e_ref[...]) + _dot(lo, e_ref[...])
        b_last = b[c - 1:c, :]
        kd = (k * jnp.exp(b_last - b)).astype(bf16)
        kv = _dot_tn(v.astype(bf16), kd) * mask_ref[...]
        qe = (q * jnp.exp(b)).astype(bf16)
        o = att[0:c, :] * v[0:1, :]
        for s in range(1, c):
            o = o + att[s * c:(s + 1) * c, :] * v[s:s + 1, :]
        st = st_ref[...]
        o_ref[pl.ds(r0, c), :] = o + _dot_nt(qe, st.astype(bf16))
        st_ref[...] = st * jnp.exp(b_last) + kv
        return carry

    lax.fori_loop(0, n_chunks, one_chunk, 0, unroll=min(GLA_UNROLL, n_chunks))

    @pl.when(i == pl.num_programs(1) - 1)
    def _finish():
        sout_ref[...] = st_ref[...]


def _gla(gq, gk, la, gv, s0t, e_mat, mask_t, *, row0, nseq, steps, rows_per_step, chunk):
    blk0 = row0 // rows_per_step
    rows = lambda n: pl.BlockSpec((rows_per_step, n), lambda s, i: (blk0 + s * steps + i, 0))
    orow = pl.BlockSpec((rows_per_step, D_GLA), lambda s, i: (s * steps + i, 0))
    st_spec = pl.BlockSpec((None, D_GLA, D_GK), lambda s, i: (s, 0, 0))
    const = lambda a: pl.BlockSpec(a.shape, lambda s, i: (0,) * a.ndim)
    return pl.pallas_call(
        functools.partial(_gla_body, chunk=chunk),
        grid=(nseq, steps),
        in_specs=[rows(D_GK), rows(D_GK), rows(D_GK), rows(D_GLA), st_spec, const(e_mat), const(mask_t)],
        out_specs=[orow, st_spec],
        out_shape=[jax.ShapeDtypeStruct((nseq * steps * rows_per_step, D_GLA), f32),
                   jax.ShapeDtypeStruct((nseq, D_GLA, D_GK), f32)],
        scratch_shapes=[pltpu.VMEM((D_GLA, D_GK), f32)],
        compiler_params=_params(("arbitrary", "arbitrary"), VMEM_LIMIT),
        name="gla",
    )(gq, gk, la, gv, s0t, e_mat, mask_t)


def _ssm_prep_body(are_ref, aim_ref, ldt_ref, bre_ref, bim_ref,
                   pre_ref, pim_ref, bbre_ref, bbim_ref):
    a_re = are_ref[...]
    a_im = aim_ref[...]
    dt = jnp.exp(ldt_ref[...])
    mag = jnp.exp(a_re * dt)
    abar_re = mag * jnp.cos(a_im * dt)
    abar_im = mag * jnp.sin(a_im * dt)
    den = a_re * a_re + a_im * a_im
    nr, ni = abar_re - 1.0, abar_im
    fr = (nr * a_re + ni * a_im) / den
    fi = (ni * a_re - nr * a_im) / den
    bbre_ref[...] = fr * bre_ref[...] - fi * bim_ref[...]
    bbim_ref[...] = fr * bim_ref[...] + fi * bre_ref[...]
    n = pre_ref.shape[0]
    pre_ref[0:1, :] = abar_re
    pim_ref[0:1, :] = abar_im
    m = 1
    while m < n:
        tr = pre_ref[m - 1:m, :]
        ti = pim_ref[m - 1:m, :]
        xr = pre_ref[0:m, :]
        xi = pim_ref[0:m, :]
        pre_ref[m:2 * m, :] = xr * tr - xi * ti
        pim_ref[m:2 * m, :] = xr * ti + xi * tr
        m *= 2


def _ssm_prep(a_re, a_im, log_dt, bt_re, bt_im, n_pow):
    outs = [(n_pow, D_STATE), (n_pow, D_STATE), (SSM_GC, D_STATE), (SSM_GC, D_STATE)]
    return pl.pallas_call(
        _ssm_prep_body,
        out_shape=[jax.ShapeDtypeStruct(s, f32) for s in outs],
        name="ssm_prep",
    )(a_re, a_im, log_dt, bt_re, bt_im)


def _ssm_body(u_ref, wb_ref, pre_ref, pim_ref, hin_re_ref, hin_im_ref, wc_ref, d_ref, wg_ref, bg_ref,
              o_ref, hre_ref, him_ref, car_re_ref, car_im_ref, *, tile_carry):
    i = pl.program_id(0)
    u = u_ref[...]
    tt = u.shape[0]
    grp = pre_ref.shape[0]
    d = _dot(u.astype(bf16), wb_ref[...])
    xr = d[:, :D_STATE]
    xi = d[:, D_STATE:]
    pos = _mod_pow2(_iota((tt, D_STATE), 0), grp)
    sh = 1
    while sh < grp:
        ar = pre_ref[sh - 1:sh, :]
        ai = pim_ref[sh - 1:sh, :]
        sr = jnp.where(pos >= sh, pltpu.roll(xr, sh, 0), 0.0)
        si = jnp.where(pos >= sh, pltpu.roll(xi, sh, 0), 0.0)
        xr, xi = xr + ar * sr - ai * si, xi + ar * si + ai * sr
        sh *= 2
    pr = pre_ref[...]
    pi = pim_ref[...]
    if tile_carry:
        @pl.when(i == 0)
        def _init():
            car_re_ref[...] = hin_re_ref[...]
            car_im_ref[...] = hin_im_ref[...]
        cr = car_re_ref[...]
        ci = car_im_ref[...]
        hr_groups, hi_groups = [], []
        for g in range(tt // grp):
            gr = xr[g * grp:(g + 1) * grp, :] + pr * cr - pi * ci
            gi = xi[g * grp:(g + 1) * grp, :] + pr * ci + pi * cr
            cr, ci = gr[grp - 1:grp, :], gi[grp - 1:grp, :]
            hr_groups.append(gr)
            hi_groups.append(gi)
        hr = jnp.concatenate(hr_groups, axis=0)
        hi = jnp.concatenate(hi_groups, axis=0)
        car_re_ref[...] = cr
        car_im_ref[...] = ci
        hre_ref[...] = cr
        him_ref[...] = ci
    else:
        h0r = hin_re_ref[...]
        h0i = hin_im_ref[...]
        prt = jnp.tile(pr, (tt // grp, 1))
        pit = jnp.tile(pi, (tt // grp, 1))
        hr = xr + prt * h0r - pit * h0i
        hi = xi + prt * h0i + pit * h0r
        hre_ref[...] = hr
        him_ref[...] = hi
    hcat = jnp.concatenate([hr, hi], axis=1).astype(bf16)
    y = _dot(hcat, wc_ref[...]) + d_ref[...] * u
    zg = jax.nn.gelu(y)
    o_ref[...] = (zg * jax.nn.sigmoid(_dot(zg.astype(bf16), wg_ref[...]) + bg_ref[...])).astype(bf16)


def _ssm(su, wb, p_re, p_im, hin_re, hin_im, wc, dvec, wglu, bglu, *, row0, n_rows, tile_carry):
    tt = TOK_TILE
    blk0 = row0 // tt
    const = lambda a: pl.BlockSpec(a.shape, lambda i: (0,) * a.ndim)
    if tile_carry:
        h_spec = const(hin_re)
        hout_spec = pl.BlockSpec((1, D_STATE), lambda i: (0, 0))
        hout_shape = jax.ShapeDtypeStruct((1, D_STATE), f32)
    else:
        h_spec = pl.BlockSpec((tt, D_STATE), lambda i: (i, 0))
        hout_spec = h_spec
        hout_shape = jax.ShapeDtypeStruct((n_rows, D_STATE), f32)
    return pl.pallas_call(
        functools.partial(_ssm_body, tile_carry=tile_carry),
        grid=(n_rows // tt,),
        in_specs=[pl.BlockSpec((tt, D_SSM), lambda i: (blk0 + i, 0)), const(wb), const(p_re), const(p_im),
                  h_spec, h_spec, const(wc), const(dvec), const(wglu), const(bglu)],
        out_specs=[pl.BlockSpec((tt, D_SSM), lambda i: (i, 0)), hout_spec, hout_spec],
        out_shape=[jax.ShapeDtypeStruct((n_rows, D_SSM), bf16), hout_shape, hout_shape],
        scratch_shapes=[pltpu.VMEM((1, D_STATE), f32), pltpu.VMEM((1, D_STATE), f32)],
        compiler_params=_params(("arbitrary",), VMEM_LIMIT),
        name="ssm",
    )(su, wb, p_re, p_im, hin_re, hin_im, wc, dvec, wglu, bglu)


def _outproj_body(fox_ref, gla_ref, gr_ref, ssm_ref, x_ref, fn_ref, gn_ref, avg_ref, w_ref, gf_ref,
                  x1_ref, xf_ref):
    fo = fox_ref[...]
    fo = fo * lax.rsqrt(jnp.mean(fo * fo, axis=-1, keepdims=True) + EPS) * fn_ref[...]
    go = gla_ref[...]
    ms = _dot3_rhs01(go * go, avg_ref[...]) * (1.0 / GLA_DV)
    go = go * lax.rsqrt(ms + EPS) * gn_ref[...] * jax.nn.silu(gr_ref[...])
    w = w_ref
    y = (_dot(fo.astype(bf16), w[0:D_FOX, :]) + _dot(go.astype(bf16), w[D_FOX:D_FOX + D_GLA, :])
         + _dot(ssm_ref[...], w[D_FOX + D_GLA:, :]))
    x1 = x_ref[...] + y
    x1_ref[...] = x1
    xf = x1 * lax.rsqrt(jnp.mean(x1 * x1, axis=-1, keepdims=True) + EPS) * gf_ref[...]
    xf_ref[...] = xf.astype(bf16)


def _outproj(fox_o, gla_o, gr, ssm_o, x, fox_norm, gla_norm6, avg, w_out, g_ffn):
    t, dm = x.shape
    tt = PEER_TOK
    row = lambda n: pl.BlockSpec((tt, n), lambda i: (i, 0))
    const = lambda a: pl.BlockSpec(a.shape, lambda i: (0,) * a.ndim)
    return pl.pallas_call(
        _outproj_body,
        grid=(t // tt,),
        in_specs=[row(D_FOX), row(D_GLA), row(D_GLA), row(D_SSM), row(dm),
                  const(fox_norm), const(gla_norm6), const(avg), const(w_out), const(g_ffn)],
        out_specs=[row(dm), row(dm)],
        out_shape=[jax.ShapeDtypeStruct((t, dm), f32), jax.ShapeDtypeStruct((t, dm), bf16)],
        compiler_params=_params(("arbitrary",), VMEM_LIMIT),
        name="outproj",
    )(fox_o, gla_o, gr, ssm_o, x, fox_norm, gla_norm6, avg, w_out, g_ffn)


def _topk_rows(s, k, ids=None):
    it = _iota(s.shape, 0) if ids is None else ids
    big = jnp.iinfo(jnp.int32).max
    vals, idxs = [], []
    for _ in range(k):
        m = jnp.max(s, axis=0, keepdims=True)
        idx = jnp.min(jnp.where(s == m, it, big), axis=0, keepdims=True)
        vals.append(m)
        idxs.append(idx)
        s = jnp.where(it == idx, -jnp.inf, s)
    return jnp.concatenate(vals, axis=0), jnp.concatenate(idxs, axis=0)


def _candidate_plan(kk):
    runs = [(a, kk // (a + 1)) for a in range(kk)]
    blocks = []
    for a, count in runs:
        for b0 in range(0, count - count % 8, 8):
            blocks.append([(a, b0, 8, 0)])
    open_blocks = []
    for a, count in sorted(runs, key=lambda r: -(r[1] % 8)):
        rem = count % 8
        if rem == 0:
            continue
        for blk in open_blocks:
            used = sum(seg[2] for seg in blk)
            if used + rem <= 8:
                blk.append((a, count - rem, rem, used))
                break
        else:
            open_blocks.append([(a, count - rem, rem, 0)])
    return blocks + open_blocks


def _candidate_ids(kk):
    ids = []
    for blk in _candidate_plan(kk):
        col = np.full(8, 1 << 30, np.int32)
        for a, b0, count, off in blk:
            col[off:off + count] = a * kk + b0 + np.arange(count)
        ids.append(col)
    return np.concatenate(ids)[:, None]


def _pair_candidates(v1, v2):
    kk, tt = v1.shape
    row = _iota((8, tt), 0)
    shifted = {}
    blocks = []
    for blk in _candidate_plan(kk):
        out = jnp.full((8, tt), -jnp.inf, f32)
        for a, b0, count, off in blk:
            if (b0, off) not in shifted:
                piece = v2[b0:b0 + 8, :]
                shifted[(b0, off)] = piece if off == 0 else pltpu.roll(piece, off, 0)
            seg = v1[a:a + 1, :] + shifted[(b0, off)]
            out = seg if count == 8 else jnp.where((row >= off) & (row < off + count), seg, out)
        blocks.append(out)
    return jnp.concatenate(blocks, axis=0)


def _take_rows(table, sel):
    return jnp.sum(jnp.where(_iota(table.shape, 0) == sel, table, 0), axis=0, keepdims=True)


def _peer_select(qt, keys1, keys2, cand_id):
    kk = PEER_TOPK
    half = PEER_DK // 2
    v1, i1 = _topk_rows(_dot(keys1, qt[:half, :].astype(bf16)), kk)
    v2, i2 = _topk_rows(_dot(keys2, qt[half:, :].astype(bf16)), kk)
    sc, pos = _topk_rows(_pair_candidates(v1, v2), kk, cand_id)
    sel1 = jnp.concatenate([_take_rows(i1, _div_pow2(pos[j:j + 1, :], kk)) for j in range(kk)], axis=0)
    sel2 = jnp.concatenate([_take_rows(i2, _mod_pow2(pos[j:j + 1, :], kk)) for j in range(kk)], axis=0)
    ex = jnp.exp(sc - sc[0:1, :])
    return sel1, sel2, ex / jnp.sum(ex, axis=0, keepdims=True)


def _peer_topk_body(xf_ref, wqt_ref, keys_ref, cid_ref, i1_ref, i2_ref, g_ref):
    qt = _dot_nt(wqt_ref[...], xf_ref[...])
    cand_id = cid_ref[...]
    picks = [_peer_select(qt[h * PEER_DK:(h + 1) * PEER_DK, :], keys_ref[h, 0], keys_ref[h, 1], cand_id)
             for h in range(PEER_HEADS)]
    i1_ref[...] = jnp.concatenate([p[0] for p in picks], axis=0).T
    i2_ref[...] = jnp.concatenate([p[1] for p in picks], axis=0).T
    g_ref[...] = jnp.concatenate([p[2] for p in picks], axis=0).T


def _peer_topk(xf, wqt, keys):
    t, dm = xf.shape
    tt = TOK_TILE
    const = lambda a: pl.BlockSpec(a.shape, lambda i: (0,) * a.ndim)
    slot = pl.BlockSpec((tt, PEER_SLOTS), lambda i: (i, 0))
    ids = _candidate_ids(PEER_TOPK)
    cand_id = jnp.asarray(np.broadcast_to(ids, (ids.shape[0], tt)))
    return pl.pallas_call(
        _peer_topk_body,
        grid=(t // tt,),
        in_specs=[pl.BlockSpec((tt, dm), lambda i: (i, 0)), const(wqt), const(keys), const(cand_id)],
        out_specs=[slot, slot, slot],
        out_shape=[jax.ShapeDtypeStruct((t, PEER_SLOTS), i32), jax.ShapeDtypeStruct((t, PEER_SLOTS), i32),
                   jax.ShapeDtypeStruct((t, PEER_SLOTS), f32)],
        compiler_params=_params(("arbitrary",), VMEM_LIMIT),
        name="peer_topk",
    )(xf, wqt, keys, cand_id)


def _peer_dense_body(xf_ref, i1_ref, i2_ref, g_ref, u_ref, v_ref, x1_ref, gfin_ref, o_ref,
                     gate_ref, acc_ref, *, final_norm):
    e = pl.program_id(1)
    tt = xf_ref.shape[0]
    nk = PEER_KEYS
    rows_per_tile = u_ref.shape[0] // nk

    @pl.when(e == 0)
    def _build_gates():
        acc_ref[...] = jnp.zeros_like(acc_ref)
        key_id = _iota((nk, PEER_SLOTS), 0)

        def token_group(gi, carry):
            t0 = pl.multiple_of(gi * GATE_GROUP, GATE_GROUP)
            r1 = i1_ref[pl.ds(t0, GATE_GROUP), :]
            r2 = i2_ref[pl.ds(t0, GATE_GROUP), :]
            gr = g_ref[pl.ds(t0, GATE_GROUP), :]
            mats = []
            for j in range(GATE_GROUP):
                p1 = jnp.where(r1[j:j + 1, :] == key_id, gr[j:j + 1, :], 0.0).astype(bf16)
                p2 = jnp.where(r2[j:j + 1, :] == key_id, 1.0, 0.0).astype(bf16)
                mats.append(_dot_nt(p1, p2))
            gate_ref[:, pl.ds(t0, GATE_GROUP), :] = jnp.swapaxes(jnp.stack(mats, axis=0), 0, 1).astype(bf16)
            return carry

        lax.fori_loop(0, tt // GATE_GROUP, token_group, 0, unroll=4)

    a = _dot_nt(xf_ref[...], u_ref[...])
    act = jax.nn.gelu(a)
    ws = []
    for r in range(rows_per_tile):
        gt = gate_ref[e * rows_per_tile + r]
        ws.append((gt.astype(f32) * act[:, r * nk:(r + 1) * nk]).astype(bf16))
    acc_ref[...] += _dot(jnp.concatenate(ws, axis=1), v_ref[...])

    @pl.when(e == pl.num_programs(1) - 1)
    def _finish():
        y = x1_ref[...] + acc_ref[...]
        if final_norm:
            y = y * lax.rsqrt(jnp.mean(y * y, axis=-1, keepdims=True) + EPS) * gfin_ref[...]
        o_ref[...] = y


def _peer_dense(xf, i1, i2, g, u_tab, v_tab, layer, x1, g_final, final_norm):
    t, dm = xf.shape
    tt = PEER_TOK
    et = PEER_ETILE
    tok = lambda n: pl.BlockSpec((tt, n), lambda i, e: (i, 0))
    tab = pl.BlockSpec((None, et, dm), lambda i, e: (layer, e, 0))
    return pl.pallas_call(
        functools.partial(_peer_dense_body, final_norm=final_norm),
        grid=(t // tt, u_tab.shape[1] // et),
        in_specs=[tok(dm), tok(PEER_SLOTS), tok(PEER_SLOTS), tok(PEER_SLOTS), tab, tab, tok(dm),
                  pl.BlockSpec(g_final.shape, lambda i, e: (0, 0))],
        out_specs=tok(dm),
        out_shape=jax.ShapeDtypeStruct((t, dm), f32),
        scratch_shapes=[pltpu.VMEM((PEER_KEYS, tt, PEER_KEYS), bf16), pltpu.VMEM((tt, dm), f32)],
        compiler_params=_params(("arbitrary", "arbitrary"), VMEM_LIMIT),
        name="peer_dense",
    )(xf, i1, i2, g, u_tab, v_tab, x1, g_final)


def _pack_w_in(w_in):
    dm = w_in.shape[0]
    sizes = (D_FOX, D_FOX, D_FOX, H_FOX, D_GK, D_GK, D_GLA, GLA_RANK, D_GLA, D_SSM)
    pts = np.cumsum(sizes)[:-1].tolist()
    fq, fk, fv, ff, gq, gk, gv, ga, gr, su = jnp.split(w_in, pts, axis=1)
    z = lambda n: jnp.zeros((dm, n), w_in.dtype)
    tail = jnp.concatenate([ff, z(TAIL_GA - H_FOX), ga, z(LANES - TAIL_GA - GLA_RANK)], axis=1)
    packed = jnp.concatenate([fq, fk, fv, gq, z(C_GK - C_GQ - D_GK), gk, z(C_GV - C_GK - D_GK),
                              gv, gr, su, tail], axis=1)
    assert packed.shape[1] == D_INP
    return packed.astype(bf16)


def _block_diag(blocks):
    g, a, b = blocks.shape
    eye = jnp.eye(g, dtype=blocks.dtype)
    return (blocks[:, :, None, :] * eye[:, None, :, None]).reshape(g * a, g * b)


def _head_match(rows, row_w, cols, col_w):
    return (np.arange(rows)[:, None] // row_w == np.arange(cols)[None, :] // col_w)


def kernel(x_prompt, x_sample, cache_fox_k, cache_fox_v, cache_fox_logf, state_gla, state_ssm_re, state_ssm_im, page_table, g_mix, w_in, fox_bf, fox_norm, gla_wa2, gla_ba, gla_norm, ssm_a_re, ssm_a_im, ssm_log_dt, ssm_b_re, ssm_b_im, ssm_c_re, ssm_c_im, ssm_d, ssm_w_glu, ssm_b_glu, w_out, g_ffn, peer_wq, peer_keys, peer_u, peer_v, g_final):
    bp, seq, dm = x_prompt.shape
    bd, tdec, _ = x_sample.shape
    depth = w_in.shape[0]
    assert bp == 1 and seq % FOX_TILE == 0 and seq % GLA_STEP == 0
    n_s = bd * tdec
    assert n_s == TOK_TILE and tdec == 8
    n_tok = seq + n_s
    n_pool, page = cache_fox_k.shape[1], cache_fox_k.shape[2]
    assert page == LANES and page_table.shape[1] % FOX_PAGES == 0

    n_pad = -n_tok % PEER_TOK
    pad_rows = lambda a: jnp.zeros((n_pad, a.shape[1]), a.dtype)
    x = jnp.concatenate([x_prompt.reshape(seq, dm), x_sample.reshape(n_s, dm), jnp.zeros((n_pad, dm), f32)], axis=0)
    ck = jnp.transpose(cache_fox_k, (0, 1, 3, 4, 2))
    cv = jnp.transpose(cache_fox_v, (0, 1, 3, 4, 2))
    clft = jnp.pad(jnp.swapaxes(cache_fox_logf, 2, 3), ((0, 0), (0, 0), (0, 8 - H_FOX), (0, 0)))

    e_mat = jnp.asarray(_head_match(D_GK, GLA_DK, D_GLA, GLA_DV), bf16)
    mask_t = jnp.asarray(_head_match(D_GLA, GLA_DV, D_GK, GLA_DK), f32)
    avg = jnp.asarray(_head_match(D_GLA, GLA_DV, D_GLA, GLA_DV), bf16)
    zeros_state = jnp.zeros((1, D_STATE), f32)
    u_tab = peer_u.astype(bf16)
    v_tab = peer_v.astype(bf16)

    outs_p, outs_s = [], []
    for l in range(depth):
        w_p = _pack_w_in(w_in[l])
        bfp = jnp.pad(fox_bf[l], (0, LANES - H_FOX)).reshape(1, LANES)
        wa2p = jnp.pad(gla_wa2[l], ((TAIL_GA, LANES - TAIL_GA - GLA_RANK), (0, 0))).astype(bf16)
        ba = gla_ba[l].reshape(1, D_GK)
        a_re = ssm_a_re[l].reshape(1, D_STATE)
        a_im = ssm_a_im[l].reshape(1, D_STATE)
        ldt = jnp.repeat(ssm_log_dt[l], SSM_P).reshape(1, D_STATE)
        bt_re = jnp.transpose(ssm_b_re[l], (2, 0, 1)).reshape(SSM_GC, D_STATE)
        bt_im = jnp.transpose(ssm_b_im[l], (2, 0, 1)).reshape(SSM_GC, D_STATE)
        p_re, p_im, bb_re, bb_im = _ssm_prep(a_re, a_im, ldt, bt_re, bt_im, tdec)
        to_blocks = lambda bb: jnp.transpose(bb.reshape(SSM_GC, SSM_G, SSM_P), (1, 0, 2))
        wb = jnp.concatenate([_block_diag(to_blocks(bb_re)), _block_diag(to_blocks(bb_im))], axis=1).astype(bf16)
        c_blocks = lambda cc: jnp.transpose(cc, (0, 2, 1))
        wc = jnp.concatenate([_block_diag(c_blocks(ssm_c_re[l])), -_block_diag(c_blocks(ssm_c_im[l]))],
                             axis=0).astype(bf16)
        dvec = ssm_d[l].reshape(1, D_SSM)
        wglu = ssm_w_glu[l].astype(bf16)
        bglu = ssm_b_glu[l].reshape(1, D_SSM)
        wqt = jnp.transpose(peer_wq[l].reshape(dm, PEER_HEADS * PEER_DK)).astype(bf16)
        keys = peer_keys[l].astype(bf16)

        (fq, qaug, kaug, fvb, fk, fv, lf, gq, gk, gv, la, gr, su) = _inproj(
            x, g_mix[l].reshape(1, dm), w_p, bfp, wa2p, ba)

        fox_p = _fox_prompt(qaug, kaug, fvb, seq)
        lfn_t = jnp.pad(jnp.swapaxes(lf[seq:n_tok, :8].reshape(bd, tdec, 8), 1, 2), ((0, 0), (0, 0), (0, LANES - tdec)))
        fox_s = _fox_sample(fq[seq:n_tok].astype(f32), fk[seq:n_tok], fv[seq:n_tok], lfn_t, ck, cv, clft, page_table, l, tdec)
        fox_o = jnp.concatenate([fox_p, fox_s, pad_rows(fox_s)], axis=0)

        gla_p, st_p = _gla(gq, gk, la, gv, jnp.zeros((1, D_GLA, D_GK), f32), e_mat, mask_t,
                           row0=0, nseq=1, steps=seq // GLA_STEP, rows_per_step=GLA_STEP, chunk=GLA_CHUNK)
        s0 = state_gla[l].astype(f32)
        s0t = jnp.transpose(s0, (0, 1, 3, 2))
        s0t = (s0t[:, :, :, None, :] * jnp.eye(H_GLA, dtype=f32)[None, :, None, :, None]).reshape(bd, D_GLA, D_GK)
        gla_s, st_s = _gla(gq, gk, la, gv, s0t, e_mat, mask_t,
                           row0=seq, nseq=bd, steps=1, rows_per_step=tdec, chunk=tdec)
        gla_o = jnp.concatenate([gla_p, gla_s, pad_rows(gla_s)], axis=0)

        def unpack_state(st):
            st = st.reshape(-1, H_GLA, GLA_DV, H_GLA, GLA_DK)
            st = jnp.stack([st[:, h, :, h, :] for h in range(H_GLA)], axis=1)
            return jnp.transpose(st, (0, 1, 3, 2))

        ssm_p, hre_p, him_p = _ssm(su, wb, p_re, p_im, zeros_state, zeros_state, wc, dvec, wglu, bglu,
                                   row0=0, n_rows=seq, tile_carry=True)
        h0r = jnp.repeat(state_ssm_re[l].reshape(bd, D_STATE).astype(f32), tdec, axis=0)
        h0i = jnp.repeat(state_ssm_im[l].reshape(bd, D_STATE).astype(f32), tdec, axis=0)
        ssm_s, hre_s, him_s = _ssm(su, wb, p_re, p_im, h0r, h0i, wc, dvec, wglu, bglu,
                                   row0=seq, n_rows=n_s, tile_carry=False)
        ssm_o = jnp.concatenate([ssm_p, ssm_s, pad_rows(ssm_s)], axis=0)

        x1, xf = _outproj(fox_o, gla_o, gr, ssm_o, x, fox_norm[l].reshape(1, D_FOX),
                          jnp.tile(gla_norm[l], H_GLA).reshape(1, D_GLA), avg,
                          w_out[l].astype(bf16), g_ffn[l].reshape(1, dm))
        i1, i2, gate = _peer_topk(xf, wqt, keys)
        x = _peer_dense(xf, i1, i2, gate, u_tab, v_tab, l, x1, g_final.reshape(1, dm), l == depth - 1)

        outs_p.append((fk[:seq].reshape(bp, seq, H_FOX, FOX_DH), fv[:seq].reshape(bp, seq, H_FOX, FOX_DH),
                       lf[:seq, :H_FOX].reshape(bp, seq, H_FOX), unpack_state(st_p),
                       hre_p.reshape(bp, SSM_G, SSM_P), him_p.reshape(bp, SSM_G, SSM_P)))
        outs_s.append((fk[seq:n_tok].reshape(bd, tdec, H_FOX, FOX_DH), fv[seq:n_tok].reshape(bd, tdec, H_FOX, FOX_DH),
                       lf[seq:n_tok, :H_FOX].reshape(bd, tdec, H_FOX), unpack_state(st_s),
                       hre_s[tdec - 1::tdec].reshape(bd, SSM_G, SSM_P), him_s[tdec - 1::tdec].reshape(bd, SSM_G, SSM_P)))

    p_stk = [jnp.stack(t) for t in zip(*outs_p)]
    s_stk = [jnp.stack(t) for t in zip(*outs_s)]
    y_prompt = x[:seq].reshape(bp, seq, dm)
    y_sample = x[seq:n_tok].reshape(bd, tdec, dm)
    return (y_prompt, y_sample, *p_stk, *s_stk)
```

```python
import functools

import numpy as np
import jax
import jax.numpy as jnp
from jax import lax
from jax.experimental import pallas as pl
from jax.experimental.pallas import tpu as pltpu

f32 = jnp.float32
bf16 = jnp.bfloat16
i32 = jnp.int32

EPS = 1e-6
LOG2E = 1.4426950408889634
LANES = 128
VMEM_LIMIT = 56 * 1024 * 1024

H_FOX, FOX_DH = 6, 64
D_FOX = H_FOX * FOX_DH
H_GLA, GLA_DK, GLA_DV = 6, 32, 64
D_GK = H_GLA * GLA_DK
D_GLA = H_GLA * GLA_DV
GLA_RANK = 16
GLA_TAU = 16.0
SSM_G, SSM_GC, SSM_P = 16, 16, 64
D_SSM = SSM_G * SSM_GC
D_STATE = SSM_G * SSM_P
PEER_KEYS, PEER_HEADS, PEER_TOPK, PEER_DK = 128, 8, 16, 256
PEER_SLOTS = PEER_HEADS * PEER_TOPK

C_FQ, C_FK, C_FV = 0, 384, 768
C_GQ, C_GK = 1152, 1408
C_GV, C_GR, C_SU, C_TAIL = 1664, 2048, 2432, 2688
D_INP = 2816
TAIL_FF, TAIL_GA = 0, 8
PIECE_LANES = 8

TOK_TILE = 256
FOX_TILE = 1024
FOX_PAGES = 32
GLA_STEP = 512
GLA_CHUNK = 16
PEER_TOK = 512
PEER_ETILE = 2048
GATE_GROUP = 16
GLA_UNROLL = 8


def _dot(a, b):
    return jnp.dot(a, b, preferred_element_type=f32)


def _dot_nt(a, b):
    return lax.dot_general(a, b, (((1,), (1,)), ((), ())), preferred_element_type=f32)


def _dot_tn(a, b):
    return lax.dot_general(a, b, (((0,), (0,)), ((), ())), preferred_element_type=f32)


def _split3(x):
    h = x.astype(bf16)
    r = x - h.astype(f32)
    m = r.astype(bf16)
    l = (r - m.astype(f32)).astype(bf16)
    return h, m, l


def _dot3_lhs01(sel, x):
    h, m, l = _split3(x)
    return _dot(sel, h) + _dot(sel, m) + _dot(sel, l)


def _dot3_rhs01(x, sel):
    h, m, l = _split3(x)
    return _dot(h, sel) + _dot(m, sel) + _dot(l, sel)


def _log_sigmoid(x):
    return -(jnp.maximum(-x, 0.0) + jnp.log1p(jnp.exp(-jnp.abs(x))))


def _iota(shape, dim):
    return lax.broadcasted_iota(i32, shape, dim)


def _div_pow2(x, n):
    assert n & (n - 1) == 0
    return jnp.right_shift(x, n.bit_length() - 1)


def _mod_pow2(x, n):
    assert n & (n - 1) == 0
    return jnp.bitwise_and(x, n - 1)


def _params(sem, vmem=None):
    return pltpu.CompilerParams(dimension_semantics=sem, vmem_limit_bytes=vmem)


def _fox_bias_lanes():
    nq = H_FOX * LANES
    place = np.zeros((LANES, 2 * nq), np.float32)
    ones = np.zeros((1, 2 * nq), np.float32)
    for j in range(H_FOX):
        base = j * LANES + (FOX_DH if j % 2 == 0 else 0)
        for p in range(3):
            place[PIECE_LANES * p + j, base + p] = 1.0
            ones[0, nq + base + p] = 1.0
            ones[0, base + 3 + p] = 1.0
            place[PIECE_LANES * p + j, nq + base + 3 + p] = -1.0
    return place, ones


def _inproj_body(x_ref, g_ref, w_ref, bfp_ref, wa2_ref, ba_ref, place_ref, ones_ref,
                 fq_ref, qa_ref, ka_ref, fvb_ref, fk_ref, fv_ref, lf_ref,
                 gq_ref, gk_ref, gv_ref, la_ref, gr_ref, su_ref, carry_ref):
    @pl.when(pl.program_id(0) == 0)
    def _init():
        carry_ref[...] = jnp.zeros_like(carry_ref)

    x = x_ref[...]
    tt = x.shape[0]
    xn = x * lax.rsqrt(jnp.mean(x * x, axis=-1, keepdims=True) + EPS) * g_ref[...]
    z = _dot(xn.astype(bf16), w_ref[...])
    fq_ref[...] = (z[:, C_FQ:C_FQ + D_FOX] * (FOX_DH ** -0.5)).astype(bf16)
    fk = z[:, C_FK:C_FK + D_FOX]
    fv = z[:, C_FV:C_FV + D_FOX]
    fk_ref[...] = fk
    fv_ref[...] = fv
    fvb_ref[...] = fv.astype(bf16)
    gq_ref[...] = z[:, C_GQ:C_GQ + D_GK] * (GLA_DK ** -0.5)
    gk_ref[...] = z[:, C_GK:C_GK + D_GK]
    gv_ref[...] = z[:, C_GV:C_GV + D_GLA]
    gr_ref[...] = z[:, C_GR:C_GR + D_GLA]
    su_ref[...] = z[:, C_SU:C_SU + D_SSM]
    tail = z[:, C_TAIL:C_TAIL + LANES]
    lane = _iota((tt, LANES), 1)
    lf = jnp.where(lane < H_FOX, _log_sigmoid(tail + bfp_ref[...]), 0.0)
    lf_ref[...] = lf
    tri = jnp.where(_iota((tt, tt), 0) >= _iota((tt, tt), 1), 1.0, 0.0).astype(bf16)
    cs = _dot3_lhs01(tri, lf) + carry_ref[...]
    carry_ref[...] = cs[tt - 1:tt, :]
    pieces = [pc.astype(f32) for pc in _split3(cs * LOG2E)]
    packed = pieces[0] + pltpu.roll(pieces[1], PIECE_LANES, 1) + pltpu.roll(pieces[2], 2 * PIECE_LANES, 1)
    bias = _dot(packed.astype(bf16), place_ref[...]) + ones_ref[...]
    q_bias = bias[:, :H_FOX * LANES]
    k_bias = bias[:, H_FOX * LANES:]
    low = lane < FOX_DH
    for hp in range(H_FOX // 2):
        qt = z[:, C_FQ + hp * LANES:C_FQ + (hp + 1) * LANES] * (LOG2E * FOX_DH ** -0.5)
        kt = z[:, C_FK + hp * LANES:C_FK + (hp + 1) * LANES]
        for hh in range(2):
            blk = slice((2 * hp + hh) * LANES, (2 * hp + hh + 1) * LANES)
            real = low if hh == 0 else jnp.logical_not(low)
            qa_ref[:, blk] = jnp.where(real, qt, q_bias[:, blk]).astype(bf16)
            ka_ref[:, blk] = jnp.where(real, kt, k_bias[:, blk]).astype(bf16)
    za = _dot(tail.astype(bf16), wa2_ref[...]) + ba_ref[...]
    la_ref[...] = _log_sigmoid(za) * (1.0 / GLA_TAU)


def _inproj(x, g_mix, w_p, bfp, wa2p, ba):
    t = x.shape[0]
    tt = TOK_TILE
    row = lambda n: pl.BlockSpec((tt, n), lambda i: (i, 0))
    const = lambda a: pl.BlockSpec(a.shape, lambda i: (0,) * a.ndim)
    outs = [
        (D_FOX, bf16), (H_FOX * LANES, bf16), (H_FOX * LANES, bf16), (D_FOX, bf16), (D_FOX, f32), (D_FOX, f32),
        (LANES, f32),
        (D_GK, f32), (D_GK, f32), (D_GLA, f32), (D_GK, f32), (D_GLA, f32), (D_SSM, f32),
    ]
    place, ones = _fox_bias_lanes()
    consts = [g_mix, w_p, bfp, wa2p, ba, jnp.asarray(place, bf16), jnp.asarray(ones)]
    return pl.pallas_call(
        _inproj_body,
        grid=(t // tt,),
        in_specs=[row(x.shape[1])] + [const(a) for a in consts],
        out_specs=[row(n) for n, _ in outs],
        out_shape=[jax.ShapeDtypeStruct((t, n), d) for n, d in outs],
        scratch_shapes=[pltpu.VMEM((1, LANES), f32)],
        compiler_params=_params(("arbitrary",), VMEM_LIMIT),
        name="inproj",
    )(x, *consts)


def _fox_prompt_body(qi_ref, ki_ref, qa_ref, qb_ref, ka_ref, kb_ref, v_ref, o_ref,
                     m_ref, l_ref, acc_ref):
    p = pl.program_id(1)
    qi = qi_ref[p]
    ki = ki_ref[p]
    tq, tk = qa_ref.shape[0], ka_ref.shape[0]

    @pl.when(ki == 0)
    def _init():
        m_ref[...] = jnp.full_like(m_ref, -jnp.inf)
        l_ref[...] = jnp.zeros_like(l_ref)
        acc_ref[...] = jnp.zeros_like(acc_ref)

    def attend(r0, nr, nc, causal):
        rows = slice(r0, r0 + nr)
        vext = jnp.concatenate([v_ref[0:nc, :], jnp.ones((nc, LANES), bf16)], axis=1)
        scores = [_dot_nt(qa_ref[rows, :], ka_ref[0:nc, :]), _dot_nt(qb_ref[rows, :], kb_ref[0:nc, :])]
        for hh in range(2):
            s = scores[hh]
            if causal:
                s = jnp.where(_iota((nr, nc), 0) + r0 >= _iota((nr, nc), 1), s, -jnp.inf)
            m_old = m_ref[hh, rows, :]
            m_new = jnp.maximum(m_old, jnp.max(s, axis=-1, keepdims=True))
            pr = jnp.exp2(s - jnp.tile(m_new, (1, nc // LANES)))
            alpha = jnp.exp2(m_old - m_new)
            pv = _dot(pr.astype(bf16), vext)
            l_ref[hh, rows, :] = alpha * l_ref[hh, rows, :] + pv[:, LANES:]
            m_ref[hh, rows, :] = m_new
            acc_ref[hh, rows, :] = alpha * acc_ref[hh, rows, :] + pv[:, :LANES]

    @pl.when(ki < qi)
    def _off():
        attend(0, tq, tk, False)

    @pl.when(ki == qi)
    def _diag():
        attend(0, tq // 2, tk // 2, True)
        attend(tq // 2, tq // 2, tk, True)
        lane = _iota((1, LANES), 1)
        o_ref[...] = jnp.where(lane < FOX_DH, acc_ref[0] / l_ref[0], acc_ref[1] / l_ref[1])


def _fox_prompt(qaug, kaug, fvb, seq):
    tq = FOX_TILE
    nq = seq // tq
    qi_tbl = np.concatenate([np.full(i + 1, i) for i in range(nq)]).astype(np.int32)
    ki_tbl = np.concatenate([np.arange(i + 1) for i in range(nq)]).astype(np.int32)
    grid_spec = pltpu.PrefetchScalarGridSpec(
        num_scalar_prefetch=2,
        grid=(H_FOX // 2, len(qi_tbl)),
        in_specs=[
            pl.BlockSpec((tq, LANES), lambda h, p, qi, ki: (qi[p], 2 * h)),
            pl.BlockSpec((tq, LANES), lambda h, p, qi, ki: (qi[p], 2 * h + 1)),
            pl.BlockSpec((tq, LANES), lambda h, p, qi, ki: (ki[p], 2 * h)),
            pl.BlockSpec((tq, LANES), lambda h, p, qi, ki: (ki[p], 2 * h + 1)),
            pl.BlockSpec((tq, LANES), lambda h, p, qi, ki: (ki[p], h)),
        ],
        out_specs=pl.BlockSpec((tq, LANES), lambda h, p, qi, ki: (qi[p], h)),
        scratch_shapes=[pltpu.VMEM((2, tq, LANES), f32), pltpu.VMEM((2, tq, LANES), f32),
                        pltpu.VMEM((2, tq, LANES), f32)],
    )
    return pl.pallas_call(
        _fox_prompt_body,
        grid_spec=grid_spec,
        out_shape=jax.ShapeDtypeStruct((seq, D_FOX), f32),
        compiler_params=_params(("arbitrary", "arbitrary"), VMEM_LIMIT),
        name="fox_prompt",
    )(jnp.asarray(qi_tbl), jnp.asarray(ki_tbl), qaug, qaug, kaug, kaug, fvb)


def _expand_heads(x, rows):
    return jnp.concatenate(
        [jnp.broadcast_to(x[h:h + 1, :], (rows, x.shape[1])) for h in range(H_FOX)], axis=0)


def _fox_sample_body(pt_ref, q_ref, kn_ref, vn_ref, lfn_ref, *rest):
    del pt_ref
    npg = FOX_PAGES
    kp_refs = rest[0:npg]
    vp_refs = rest[npg:2 * npg]
    lfp_refs = rest[2 * npg:3 * npg]
    o_ref = rest[3 * npg]
    m_ref, l_ref, acc_ref, car_ref, qbd_ref, cq_ref = rest[3 * npg + 1:]
    j = pl.program_id(1)
    t = q_ref.shape[0]
    hq = H_FOX * t
    page = kp_refs[0].shape[-1]

    def update(s, weigh):
        m_old = m_ref[...]
        m_new = jnp.maximum(m_old, jnp.max(s, axis=-1, keepdims=True))
        pr = jnp.exp(s - m_new)
        alpha = jnp.exp(m_old - m_new)
        l_ref[...] = alpha * l_ref[...] + jnp.sum(pr, axis=-1, keepdims=True)
        m_ref[...] = m_new
        acc_ref[...] = alpha * acc_ref[...] + weigh(pr.astype(bf16))

    @pl.when(j == 0)
    def _new_tokens():
        m_ref[...] = jnp.full_like(m_ref, -jnp.inf)
        l_ref[...] = jnp.zeros_like(l_ref)
        acc_ref[...] = jnp.zeros_like(acc_ref)
        car_ref[...] = jnp.zeros_like(car_ref)
        q = q_ref[...]
        q6 = jnp.concatenate([q] * H_FOX, axis=0)
        keep = _div_pow2(_iota((hq, D_FOX), 0), t) == _div_pow2(_iota((hq, D_FOX), 1), FOX_DH)
        qbd_ref[...] = jnp.where(keep, q6, 0.0).astype(bf16)
        cum = lfn_ref[...]
        lane8 = _iota(cum.shape, 1)
        sh = 1
        while sh < t:
            cum = cum + jnp.where(lane8 >= sh, pltpu.roll(cum, sh, 1), 0.0)
            sh *= 2
        cum_rows = _expand_heads(cum, t)
        qpos = _mod_pow2(_iota((hq, LANES), 0), t)
        kpos = _iota((hq, LANES), 1)
        cq = jnp.sum(jnp.where(kpos == qpos, cum_rows, 0.0), axis=-1, keepdims=True)
        cq_ref[...] = cq
        zpad = jnp.zeros((LANES - t, D_FOX), f32)
        kn = jnp.concatenate([kn_ref[...], zpad], axis=0).astype(bf16)
        vn = jnp.concatenate([vn_ref[...], zpad], axis=0).astype(bf16)
        s = _dot_nt(qbd_ref[...], kn) + cq - cum_rows
        s = jnp.where(kpos <= qpos, s, -jnp.inf)
        update(s, lambda pr: _dot(pr, vn))

    upper = jnp.where(_iota((page, page), 0) > _iota((page, page), 1), 1.0, 0.0).astype(bf16)
    qbd = qbd_ref[...]
    qk = [_dot(qbd, kp_refs[g][...].reshape(D_FOX, page).astype(bf16)) for g in range(npg)]
    lf_all = jnp.concatenate([lfp_refs[g][...] for g in range(npg)], axis=0)
    suf_all = _dot3_rhs01(lf_all, upper)
    tot_all = jnp.sum(lf_all, axis=-1, keepdims=True)
    car = car_ref[...]
    scores = []
    for g in range(npg):
        scores.append(qk[g] + _expand_heads(suf_all[g * 8:(g + 1) * 8, :] + car, t))
        car = car + tot_all[g * 8:(g + 1) * 8, :]
    car_ref[...] = car

    def weigh_pages(pr):
        out = None
        for g in range(npg):
            vpt = vp_refs[g][...].reshape(D_FOX, page).astype(bf16)
            part = _dot_nt(pr[:, g * page:(g + 1) * page], vpt)
            out = part if out is None else out + part
        return out

    update(jnp.concatenate(scores, axis=1) + cq_ref[...], weigh_pages)

    @pl.when(j == pl.num_programs(1) - 1)
    def _finish():
        o48 = acc_ref[...] / l_ref[...]
        head = _div_pow2(_iota((t, D_FOX), 1), FOX_DH)
        out = jnp.zeros((t, D_FOX), f32)
        for h in range(H_FOX):
            out = out + jnp.where(head == h, o48[h * t:(h + 1) * t, :], 0.0)
        o_ref[...] = out


def _fox_sample(q_s, k_s, v_s, lfn_t, cache_k, cache_v, cache_lft, page_table, layer, t):
    nb, n_pages = page_table.shape
    page = cache_k.shape[-1]
    npg = FOX_PAGES
    hq = H_FOX * t
    seq_spec = pl.BlockSpec((t, D_FOX), lambda b, j, pt: (b, 0))

    def page_spec(shape, g):
        zeros = (0,) * len(shape)
        return pl.BlockSpec((None, None) + shape,
                            lambda b, j, pt: (layer, pt[b, n_pages - 1 - (j * npg + g)]) + zeros)

    grid_spec = pltpu.PrefetchScalarGridSpec(
        num_scalar_prefetch=1,
        grid=(nb, n_pages // npg),
        in_specs=[seq_spec, seq_spec, seq_spec,
                  pl.BlockSpec((None, 8, LANES), lambda b, j, pt: (b, 0, 0))]
        + [page_spec((H_FOX, FOX_DH, page), g) for g in range(npg)]
        + [page_spec((H_FOX, FOX_DH, page), g) for g in range(npg)]
        + [page_spec((8, page), g) for g in range(npg)],
        out_specs=seq_spec,
        scratch_shapes=[pltpu.VMEM((hq, 1), f32), pltpu.VMEM((hq, 1), f32),
                        pltpu.VMEM((hq, D_FOX), f32), pltpu.VMEM((8, 1), f32),
                        pltpu.VMEM((hq, D_FOX), bf16), pltpu.VMEM((hq, 1), f32)],
    )
    return pl.pallas_call(
        _fox_sample_body,
        grid_spec=grid_spec,
        out_shape=jax.ShapeDtypeStruct((nb * t, D_FOX), f32),
        compiler_params=_params(("arbitrary", "arbitrary"), VMEM_LIMIT),
        name="fox_sample",
    )(page_table, q_s, k_s, v_s, lfn_t, *([cache_k] * npg), *([cache_v] * npg), *([cache_lft] * npg))


def _gla_body(q_ref, k_ref, la_ref, v_ref, s0_ref, e_ref, mask_ref, o_ref, sout_ref, st_ref, *, chunk):
    i = pl.program_id(1)

    @pl.when(i == 0)
    def _init():
        st_ref[...] = s0_ref[...]

    c = chunk
    n_chunks = q_ref.shape[0] // c
    row = _iota((c, D_GK), 0)

    def one_chunk(ci, carry):
        r0 = pl.multiple_of(ci * c, c)
        q = q_ref[pl.ds(r0, c), :]
        k = k_ref[pl.ds(r0, c), :]
        v = v_ref[pl.ds(r0, c), :]
        b = la_ref[pl.ds(r0, c), :]
        sh = 1
        while sh < c:
            b = b + jnp.where(row >= sh, pltpu.roll(b, sh, 0), 0.0)
            sh *= 2
        pieces = []
        for s in range(c):
            d = jnp.where(row >= s, b - b[s:s + 1, :], -jnp.inf)
            pieces.append(q * k[s:s + 1, :] * jnp.exp(d))
        pst = jnp.concatenate(pieces, axis=0)
        hi = pst.astype(bf16)
        lo = (pst - hi.astype(f32)).astype(bf16)
        att = _dot(hi, e_ref[...]) + _dot(lo, e_ref[...])
        b_last = b[c - 1:c, :]
        kd = (k * jnp.exp(b_last - b)).astype(bf16)
        kv = _dot_tn(v.astype(bf16), kd) * mask_ref[...]
        qe = (q * jnp.exp(b)).astype(bf16)
        o = att[0:c, :] * v[0:1, :]
        for s in range(1, c):
            o = o + att[s * c:(s + 1) * c, :] * v[s:s + 1, :]
        st = st_ref[...]
        o_ref[pl.ds(r0, c), :] = o + _dot_nt(qe, st.astype(bf16))
        st_ref[...] = st * jnp.exp(b_last) + kv
        return carry

    lax.fori_loop(0, n_chunks, one_chunk, 0, unroll=min(GLA_UNROLL, n_chunks))

    @pl.when(i == pl.num_programs(1) - 1)
    def _finish():
        sout_ref[...] = st_ref[...]


def _gla(gq, gk, la, gv, s0t, e_mat, mask_t, *, row0, nseq, steps, rows_per_step, chunk):
    blk0 = row0 // rows_per_step
    rows = lambda n: pl.BlockSpec((rows_per_step, n), lambda s, i: (blk0 + s * steps + i, 0))
    orow = pl.BlockSpec((rows_per_step, D_GLA), lambda s, i: (s * steps + i, 0))
    st_spec = pl.BlockSpec((None, D_GLA, D_GK), lambda s, i: (s, 0, 0))
    const = lambda a: pl.BlockSpec(a.shape, lambda s, i: (0,) * a.ndim)
    return pl.pallas_call(
        functools.partial(_gla_body, chunk=chunk),
        grid=(nseq, steps),
        in_specs=[rows(D_GK), rows(D_GK), rows(D_GK), rows(D_GLA), st_spec, const(e_mat), const(mask_t)],
        out_specs=[orow, st_spec],
        out_shape=[jax.ShapeDtypeStruct((nseq * steps * rows_per_step, D_GLA), f32),
                   jax.ShapeDtypeStruct((nseq, D_GLA, D_GK), f32)],
        scratch_shapes=[pltpu.VMEM((D_GLA, D_GK), f32)],
        compiler_params=_params(("arbitrary", "arbitrary"), VMEM_LIMIT),
        name="gla",
    )(gq, gk, la, gv, s0t, e_mat, mask_t)


def _ssm_prep_body(are_ref, aim_ref, ldt_ref, bre_ref, bim_ref,
                   pre_ref, pim_ref, bbre_ref, bbim_ref):
    a_re = are_ref[...]
    a_im = aim_ref[...]
    dt = jnp.exp(ldt_ref[...])
    mag = jnp.exp(a_re * dt)
    abar_re = mag * jnp.cos(a_im * dt)
    abar_im = mag * jnp.sin(a_im * dt)
    den = a_re * a_re + a_im * a_im
    nr, ni = abar_re - 1.0, abar_im
    fr = (nr * a_re + ni * a_im) / den
    fi = (ni * a_re - nr * a_im) / den
    bbre_ref[...] = fr * bre_ref[...] - fi * bim_ref[...]
    bbim_ref[...] = fr * bim_ref[...] + fi * bre_ref[...]
    n = pre_ref.shape[0]
    pre_ref[0:1, :] = abar_re
    pim_ref[0:1, :] = abar_im
    m = 1
    while m < n:
        tr = pre_ref[m - 1:m, :]
        ti = pim_ref[m - 1:m, :]
        xr = pre_ref[0:m, :]
        xi = pim_ref[0:m, :]
        pre_ref[m:2 * m, :] = xr * tr - xi * ti
        pim_ref[m:2 * m, :] = xr * ti + xi * tr
        m *= 2


def _ssm_prep(a_re, a_im, log_dt, bt_re, bt_im, n_pow):
    outs = [(n_pow, D_STATE), (n_pow, D_STATE), (SSM_GC, D_STATE), (SSM_GC, D_STATE)]
    return pl.pallas_call(
        _ssm_prep_body,
        out_shape=[jax.ShapeDtypeStruct(s, f32) for s in outs],
        name="ssm_prep",
    )(a_re, a_im, log_dt, bt_re, bt_im)


def _ssm_body(u_ref, wb_ref, pre_ref, pim_ref, hin_re_ref, hin_im_ref, wc_ref, d_ref, wg_ref, bg_ref,
              o_ref, hre_ref, him_ref, car_re_ref, car_im_ref, *, tile_carry):
    i = pl.program_id(0)
    u = u_ref[...]
    tt = u.shape[0]
    grp = pre_ref.shape[0]
    d = _dot(u.astype(bf16), wb_ref[...])
    xr = d[:, :D_STATE]
    xi = d[:, D_STATE:]
    pos = _mod_pow2(_iota((tt, D_STATE), 0), grp)
    sh = 1
    while sh < grp:
        ar = pre_ref[sh - 1:sh, :]
        ai = pim_ref[sh - 1:sh, :]
        sr = jnp.where(pos >= sh, pltpu.roll(xr, sh, 0), 0.0)
        si = jnp.where(pos >= sh, pltpu.roll(xi, sh, 0), 0.0)
        xr, xi = xr + ar * sr - ai * si, xi + ar * si + ai * sr
        sh *= 2
    pr = pre_ref[...]
    pi = pim_ref[...]
    if tile_carry:
        @pl.when(i == 0)
        def _init():
            car_re_ref[...] = hin_re_ref[...]
            car_im_ref[...] = hin_im_ref[...]
        cr = car_re_ref[...]
        ci = car_im_ref[...]
        hr_groups, hi_groups = [], []
        for g in range(tt // grp):
            gr = xr[g * grp:(g + 1) * grp, :] + pr * cr - pi * ci
            gi = xi[g * grp:(g + 1) * grp, :] + pr * ci + pi * cr
            cr, ci = gr[grp - 1:grp, :], gi[grp - 1:grp, :]
            hr_groups.append(gr)
            hi_groups.append(gi)
        hr = jnp.concatenate(hr_groups, axis=0)
        hi = jnp.concatenate(hi_groups, axis=0)
        car_re_ref[...] = cr
        car_im_ref[...] = ci
        hre_ref[...] = cr
        him_ref[...] = ci
    else:
        h0r = hin_re_ref[...]
        h0i = hin_im_ref[...]
        prt = jnp.tile(pr, (tt // grp, 1))
        pit = jnp.tile(pi, (tt // grp, 1))
        hr = xr + prt * h0r - pit * h0i
        hi = xi + prt * h0i + pit * h0r
        hre_ref[...] = hr
        him_ref[...] = hi
    hcat = jnp.concatenate([hr, hi], axis=1).astype(bf16)
    y = _dot(hcat, wc_ref[...]) + d_ref[...] * u
    zg = jax.nn.gelu(y)
    o_ref[...] = (zg * jax.nn.sigmoid(_dot(zg.astype(bf16), wg_ref[...]) + bg_ref[...])).astype(bf16)


def _ssm(su, wb, p_re, p_im, hin_re, hin_im, wc, dvec, wglu, bglu, *, row0, n_rows, tile_carry):
    tt = TOK_TILE
    blk0 = row0 // tt
    const = lambda a: pl.BlockSpec(a.shape, lambda i: (0,) * a.ndim)
    if tile_carry:
        h_spec = const(hin_re)
        hout_spec = pl.BlockSpec((1, D_STATE), lambda i: (0, 0))
        hout_shape = jax.ShapeDtypeStruct((1, D_STATE), f32)
    else:
        h_spec = pl.BlockSpec((tt, D_STATE), lambda i: (i, 0))
        hout_spec = h_spec
        hout_shape = jax.ShapeDtypeStruct((n_rows, D_STATE), f32)
    return pl.pallas_call(
        functools.partial(_ssm_body, tile_carry=tile_carry),
        grid=(n_rows // tt,),
        in_specs=[pl.BlockSpec((tt, D_SSM), lambda i: (blk0 + i, 0)), const(wb), const(p_re), const(p_im),
                  h_spec, h_spec, const(wc), const(dvec), const(wglu), const(bglu)],
        out_specs=[pl.BlockSpec((tt, D_SSM), lambda i: (i, 0)), hout_spec, hout_spec],
        out_shape=[jax.ShapeDtypeStruct((n_rows, D_SSM), bf16), hout_shape, hout_shape],
        scratch_shapes=[pltpu.VMEM((1, D_STATE), f32), pltpu.VMEM((1, D_STATE), f32)],
        compiler_params=_params(("arbitrary",), VMEM_LIMIT),
        name="ssm",
    )(su, wb, p_re, p_im, hin_re, hin_im, wc, dvec, wglu, bglu)


def _outproj_body(fox_ref, gla_ref, gr_ref, ssm_ref, x_ref, fn_ref, gn_ref, avg_ref, w_ref, gf_ref,
                  x1_ref, xf_ref):
    fo = fox_ref[...]
    fo = fo * lax.rsqrt(jnp.mean(fo * fo, axis=-1, keepdims=True) + EPS) * fn_ref[...]
    go = gla_ref[...]
    ms = _dot3_rhs01(go * go, avg_ref[...]) * (1.0 / GLA_DV)
    go = go * lax.rsqrt(ms + EPS) * gn_ref[...] * jax.nn.silu(gr_ref[...])
    w = w_ref
    y = (_dot(fo.astype(bf16), w[0:D_FOX, :]) + _dot(go.astype(bf16), w[D_FOX:D_FOX + D_GLA, :])
         + _dot(ssm_ref[...], w[D_FOX + D_GLA:, :]))
    x1 = x_ref[...] + y
    x1_ref[...] = x1
    xf = x1 * lax.rsqrt(jnp.mean(x1 * x1, axis=-1, keepdims=True) + EPS) * gf_ref[...]
    xf_ref[...] = xf.astype(bf16)


def _outproj(fox_o, gla_o, gr, ssm_o, x, fox_norm, gla_norm6, avg, w_out, g_ffn):
    t, dm = x.shape
    tt = PEER_TOK
    row = lambda n: pl.BlockSpec((tt, n), lambda i: (i, 0))
    const = lambda a: pl.BlockSpec(a.shape, lambda i: (0,) * a.ndim)
    return pl.pallas_call(
        _outproj_body,
        grid=(t // tt,),
        in_specs=[row(D_FOX), row(D_GLA), row(D_GLA), row(D_SSM), row(dm),
                  const(fox_norm), const(gla_norm6), const(avg), const(w_out), const(g_ffn)],
        out_specs=[row(dm), row(dm)],
        out_shape=[jax.ShapeDtypeStruct((t, dm), f32), jax.ShapeDtypeStruct((t, dm), bf16)],
        compiler_params=_params(("arbitrary",), VMEM_LIMIT),
        name="outproj",
    )(fox_o, gla_o, gr, ssm_o, x, fox_norm, gla_norm6, avg, w_out, g_ffn)


def _topk_rows(s, k, ids=None):
    it = _iota(s.shape, 0) if ids is None else ids
    big = jnp.iinfo(jnp.int32).max
    vals, idxs = [], []
    for _ in range(k):
        m = jnp.max(s, axis=0, keepdims=True)
        idx = jnp.min(jnp.where(s == m, it, big), axis=0, keepdims=True)
        vals.append(m)
        idxs.append(idx)
        s = jnp.where(it == idx, -jnp.inf, s)
    return jnp.concatenate(vals, axis=0), jnp.concatenate(idxs, axis=0)


def _candidate_plan(kk):
    runs = [(a, kk // (a + 1)) for a in range(kk)]
    blocks = []
    for a, count in runs:
        for b0 in range(0, count - count % 8, 8):
            blocks.append([(a, b0, 8, 0)])
    open_blocks = []
    for a, count in sorted(runs, key=lambda r: -(r[1] % 8)):
        rem = count % 8
        if rem == 0:
            continue
        for blk in open_blocks:
            used = sum(seg[2] for seg in blk)
            if used + rem <= 8:
                blk.append((a, count - rem, rem, used))
                break
        else:
            open_blocks.append([(a, count - rem, rem, 0)])
    return blocks + open_blocks


def _candidate_ids(kk):
    ids = []
    for blk in _candidate_plan(kk):
        col = np.full(8, 1 << 30, np.int32)
        for a, b0, count, off in blk:
            col[off:off + count] = a * kk + b0 + np.arange(count)
        ids.append(col)
    return np.concatenate(ids)[:, None]


def _pair_candidates(v1, v2):
    kk, tt = v1.shape
    row = _iota((8, tt), 0)
    shifted = {}
    blocks = []
    for blk in _candidate_plan(kk):
        out = jnp.full((8, tt), -jnp.inf, f32)
        for a, b0, count, off in blk:
            if (b0, off) not in shifted:
                piece = v2[b0:b0 + 8, :]
                shifted[(b0, off)] = piece if off == 0 else pltpu.roll(piece, off, 0)
            seg = v1[a:a + 1, :] + shifted[(b0, off)]
            out = seg if count == 8 else jnp.where((row >= off) & (row < off + count), seg, out)
        blocks.append(out)
    return jnp.concatenate(blocks, axis=0)


def _take_rows(table, sel):
    return jnp.sum(jnp.where(_iota(table.shape, 0) == sel, table, 0), axis=0, keepdims=True)


def _peer_select(qt, keys1, keys2, cand_id):
    kk = PEER_TOPK
    half = PEER_DK // 2
    v1, i1 = _topk_rows(_dot(keys1, qt[:half, :].astype(bf16)), kk)
    v2, i2 = _topk_rows(_dot(keys2, qt[half:, :].astype(bf16)), kk)
    sc, pos = _topk_rows(_pair_candidates(v1, v2), kk, cand_id)
    sel1 = jnp.concatenate([_take_rows(i1, _div_pow2(pos[j:j + 1, :], kk)) for j in range(kk)], axis=0)
    sel2 = jnp.concatenate([_take_rows(i2, _mod_pow2(pos[j:j + 1, :], kk)) for j in range(kk)], axis=0)
    ex = jnp.exp(sc - sc[0:1, :])
    return sel1, sel2, ex / jnp.sum(ex, axis=0, keepdims=True)


def _peer_topk_body(xf_ref, wqt_ref, keys_ref, cid_ref, i1_ref, i2_ref, g_ref):
    qt = _dot_nt(wqt_ref[...], xf_ref[...])
    cand_id = cid_ref[...]
    picks = [_peer_select(qt[h * PEER_DK:(h + 1) * PEER_DK, :], keys_ref[h, 0], keys_ref[h, 1], cand_id)
             for h in range(PEER_HEADS)]
    i1_ref[...] = jnp.concatenate([p[0] for p in picks], axis=0).T
    i2_ref[...] = jnp.concatenate([p[1] for p in picks], axis=0).T
    g_ref[...] = jnp.concatenate([p[2] for p in picks], axis=0).T


def _peer_topk(xf, wqt, keys):
    t, dm = xf.shape
    tt = TOK_TILE
    const = lambda a: pl.BlockSpec(a.shape, lambda i: (0,) * a.ndim)
    slot = pl.BlockSpec((tt, PEER_SLOTS), lambda i: (i, 0))
    ids = _candidate_ids(PEER_TOPK)
    cand_id = jnp.asarray(np.broadcast_to(ids, (ids.shape[0], tt)))
    return pl.pallas_call(
        _peer_topk_body,
        grid=(t // tt,),
        in_specs=[pl.BlockSpec((tt, dm), lambda i: (i, 0)), const(wqt), const(keys), const(cand_id)],
        out_specs=[slot, slot, slot],
        out_shape=[jax.ShapeDtypeStruct((t, PEER_SLOTS), i32), jax.ShapeDtypeStruct((t, PEER_SLOTS), i32),
                   jax.ShapeDtypeStruct((t, PEER_SLOTS), f32)],
        compiler_params=_params(("arbitrary",), VMEM_LIMIT),
        name="peer_topk",
    )(xf, wqt, keys, cand_id)


def _peer_dense_body(xf_ref, i1_ref, i2_ref, g_ref, u_ref, v_ref, x1_ref, gfin_ref, o_ref,
                     gate_ref, acc_ref, *, final_norm):
    e = pl.program_id(1)
    tt = xf_ref.shape[0]
    nk = PEER_KEYS
    rows_per_tile = u_ref.shape[0] // nk

    @pl.when(e == 0)
    def _build_gates():
        acc_ref[...] = jnp.zeros_like(acc_ref)
        key_id = _iota((nk, PEER_SLOTS), 0)

        def token_group(gi, carry):
            t0 = pl.multiple_of(gi * GATE_GROUP, GATE_GROUP)
            r1 = i1_ref[pl.ds(t0, GATE_GROUP), :]
            r2 = i2_ref[pl.ds(t0, GATE_GROUP), :]
            gr = g_ref[pl.ds(t0, GATE_GROUP), :]
            mats = []
            for j in range(GATE_GROUP):
                p1 = jnp.where(r1[j:j + 1, :] == key_id, gr[j:j + 1, :], 0.0).astype(bf16)
                p2 = jnp.where(r2[j:j + 1, :] == key_id, 1.0, 0.0).astype(bf16)
                mats.append(_dot_nt(p1, p2))
            gate_ref[:, pl.ds(t0, GATE_GROUP), :] = jnp.swapaxes(jnp.stack(mats, axis=0), 0, 1).astype(bf16)
            return carry

        lax.fori_loop(0, tt // GATE_GROUP, token_group, 0, unroll=4)

    a = _dot_nt(xf_ref[...], u_ref[...])
    act = jax.nn.gelu(a)
    ws = []
    for r in range(rows_per_tile):
        gt = gate_ref[e * rows_per_tile + r]
        ws.append((gt.astype(f32) * act[:, r * nk:(r + 1) * nk]).astype(bf16))
    acc_ref[...] += _dot(jnp.concatenate(ws, axis=1), v_ref[...])

    @pl.when(e == pl.num_programs(1) - 1)
    def _finish():
        y = x1_ref[...] + acc_ref[...]
        if final_norm:
            y = y * lax.rsqrt(jnp.mean(y * y, axis=-1, keepdims=True) + EPS) * gfin_ref[...]
        o_ref[...] = y


def _peer_dense(xf, i1, i2, g, u_tab, v_tab, layer, x1, g_final, final_norm):
    t, dm = xf.shape
    tt = PEER_TOK
    et = PEER_ETILE
    tok = lambda n: pl.BlockSpec((tt, n), lambda i, e: (i, 0))
    tab = pl.BlockSpec((None, et, dm), lambda i, e: (layer, e, 0))
    return pl.pallas_call(
        functools.partial(_peer_dense_body, final_norm=final_norm),
        grid=(t // tt, u_tab.shape[1] // et),
        in_specs=[tok(dm), tok(PEER_SLOTS), tok(PEER_SLOTS), tok(PEER_SLOTS), tab, tab, tok(dm),
                  pl.BlockSpec(g_final.shape, lambda i, e: (0, 0))],
        out_specs=tok(dm),
        out_shape=jax.ShapeDtypeStruct((t, dm), f32),
        scratch_shapes=[pltpu.VMEM((PEER_KEYS, tt, PEER_KEYS), bf16), pltpu.VMEM((tt, dm), f32)],
        compiler_params=_params(("arbitrary", "arbitrary"), VMEM_LIMIT),
        name="peer_dense",
    )(xf, i1, i2, g, u_tab, v_tab, x1, g_final)


def _pack_w_in(w_in):
    dm = w_in.shape[0]
    sizes = (D_FOX, D_FOX, D_FOX, H_FOX, D_GK, D_GK, D_GLA, GLA_RANK, D_GLA, D_SSM)
    pts = np.cumsum(sizes)[:-1].tolist()
    fq, fk, fv, ff, gq, gk, gv, ga, gr, su = jnp.split(w_in, pts, axis=1)
    z = lambda n: jnp.zeros((dm, n), w_in.dtype)
    tail = jnp.concatenate([ff, z(TAIL_GA - H_FOX), ga, z(LANES - TAIL_GA - GLA_RANK)], axis=1)
    packed = jnp.concatenate([fq, fk, fv, gq, z(C_GK - C_GQ - D_GK), gk, z(C_GV - C_GK - D_GK),
                              gv, gr, su, tail], axis=1)
    assert packed.shape[1] == D_INP
    return packed.astype(bf16)


def _block_diag(blocks):
    g, a, b = blocks.shape
    eye = jnp.eye(g, dtype=blocks.dtype)
    return (blocks[:, :, None, :] * eye[:, None, :, None]).reshape(g * a, g * b)


def _head_match(rows, row_w, cols, col_w):
    return (np.arange(rows)[:, None] // row_w == np.arange(cols)[None, :] // col_w)


def kernel(x_prompt, x_sample, cache_fox_k, cache_fox_v, cache_fox_logf, state_gla, state_ssm_re, state_ssm_im, page_table, g_mix, w_in, fox_bf, fox_norm, gla_wa2, gla_ba, gla_norm, ssm_a_re, ssm_a_im, ssm_log_dt, ssm_b_re, ssm_b_im, ssm_c_re, ssm_c_im, ssm_d, ssm_w_glu, ssm_b_glu, w_out, g_ffn, peer_wq, peer_keys, peer_u, peer_v, g_final):
    bp, seq, dm = x_prompt.shape
    bd, tdec, _ = x_sample.shape
    depth = w_in.shape[0]
    assert bp == 1 and seq % FOX_TILE == 0 and seq % GLA_STEP == 0
    n_s = bd * tdec
    assert n_s == TOK_TILE and tdec == 8
    n_tok = seq + n_s
    n_pool, page = cache_fox_k.shape[1], cache_fox_k.shape[2]
    assert page == LANES and page_table.shape[1] % FOX_PAGES == 0

    n_pad = -n_tok % PEER_TOK
    pad_rows = lambda a: jnp.zeros((n_pad, a.shape[1]), a.dtype)
    x = jnp.concatenate([x_prompt.reshape(seq, dm), x_sample.reshape(n_s, dm), jnp.zeros((n_pad, dm), f32)], axis=0)
    ck = jnp.transpose(cache_fox_k, (0, 1, 3, 4, 2))
    cv = jnp.transpose(cache_fox_v, (0, 1, 3, 4, 2))
    clft = jnp.pad(jnp.swapaxes(cache_fox_logf, 2, 3), ((0, 0), (0, 0), (0, 8 - H_FOX), (0, 0)))

    e_mat = jnp.asarray(_head_match(D_GK, GLA_DK, D_GLA, GLA_DV), bf16)
    mask_t = jnp.asarray(_head_match(D_GLA, GLA_DV, D_GK, GLA_DK), f32)
    avg = jnp.asarray(_head_match(D_GLA, GLA_DV, D_GLA, GLA_DV), bf16)
    zeros_state = jnp.zeros((1, D_STATE), f32)
    u_tab = peer_u.astype(bf16)
    v_tab = peer_v.astype(bf16)

    outs_p, outs_s = [], []
    for l in range(depth):
        w_p = _pack_w_in(w_in[l])
        bfp = jnp.pad(fox_bf[l], (0, LANES - H_FOX)).reshape(1, LANES)
        wa2p = jnp.pad(gla_wa2[l], ((TAIL_GA, LANES - TAIL_GA - GLA_RANK), (0, 0))).astype(bf16)
        ba = gla_ba[l].reshape(1, D_GK)
        a_re = ssm_a_re[l].reshape(1, D_STATE)
        a_im = ssm_a_im[l].reshape(1, D_STATE)
        ldt = jnp.repeat(ssm_log_dt[l], SSM_P).reshape(1, D_STATE)
        bt_re = jnp.transpose(ssm_b_re[l], (2, 0, 1)).reshape(SSM_GC, D_STATE)
        bt_im = jnp.transpose(ssm_b_im[l], (2, 0, 1)).reshape(SSM_GC, D_STATE)
        p_re, p_im, bb_re, bb_im = _ssm_prep(a_re, a_im, ldt, bt_re, bt_im, tdec)
        to_blocks = lambda bb: jnp.transpose(bb.reshape(SSM_GC, SSM_G, SSM_P), (1, 0, 2))
        wb = jnp.concatenate([_block_diag(to_blocks(bb_re)), _block_diag(to_blocks(bb_im))], axis=1).astype(bf16)
        c_blocks = lambda cc: jnp.transpose(cc, (0, 2, 1))
        wc = jnp.concatenate([_block_diag(c_blocks(ssm_c_re[l])), -_block_diag(c_blocks(ssm_c_im[l]))],
                             axis=0).astype(bf16)
        dvec = ssm_d[l].reshape(1, D_SSM)
        wglu = ssm_w_glu[l].astype(bf16)
        bglu = ssm_b_glu[l].reshape(1, D_SSM)
        wqt = jnp.transpose(peer_wq[l].reshape(dm, PEER_HEADS * PEER_DK)).astype(bf16)
        keys = peer_keys[l].astype(bf16)

        (fq, qaug, kaug, fvb, fk, fv, lf, gq, gk, gv, la, gr, su) = _inproj(
            x, g_mix[l].reshape(1, dm), w_p, bfp, wa2p, ba)

        fox_p = _fox_prompt(qaug, kaug, fvb, seq)
        lfn_t = jnp.pad(jnp.swapaxes(lf[seq:n_tok, :8].reshape(bd, tdec, 8), 1, 2), ((0, 0), (0, 0), (0, LANES - tdec)))
        fox_s = _fox_sample(fq[seq:n_tok].astype(f32), fk[seq:n_tok], fv[seq:n_tok], lfn_t, ck, cv, clft, page_table, l, tdec)
        fox_o = jnp.concatenate([fox_p, fox_s, pad_rows(fox_s)], axis=0)

        gla_p, st_p = _gla(gq, gk, la, gv, jnp.zeros((1, D_GLA, D_GK), f32), e_mat, mask_t,
                           row0=0, nseq=1, steps=seq // GLA_STEP, rows_per_step=GLA_STEP, chunk=GLA_CHUNK)
        s0 = state_gla[l].astype(f32)
        s0t = jnp.transpose(s0, (0, 1, 3, 2))
        s0t = (s0t[:, :, :, None, :] * jnp.eye(H_GLA, dtype=f32)[None, :, None, :, None]).reshape(bd, D_GLA, D_GK)
        gla_s, st_s = _gla(gq, gk, la, gv, s0t, e_mat, mask_t,
                           row0=seq, nseq=bd, steps=1, rows_per_step=tdec, chunk=tdec)
        gla_o = jnp.concatenate([gla_p, gla_s, pad_rows(gla_s)], axis=0)

        def unpack_state(st):
            st = st.reshape(-1, H_GLA, GLA_DV, H_GLA, GLA_DK)
            st = jnp.stack([st[:, h, :, h, :] for h in range(H_GLA)], axis=1)
            return jnp.transpose(st, (0, 1, 3, 2))

        ssm_p, hre_p, him_p = _ssm(su, wb, p_re, p_im, zeros_state, zeros_state, wc, dvec, wglu, bglu,
                                   row0=0, n_rows=seq, tile_carry=True)
        h0r = jnp.repeat(state_ssm_re[l].reshape(bd, D_STATE).astype(f32), tdec, axis=0)
        h0i = jnp.repeat(state_ssm_im[l].reshape(bd, D_STATE).astype(f32), tdec, axis=0)
        ssm_s, hre_s, him_s = _ssm(su, wb, p_re, p_im, h0r, h0i, wc, dvec, wglu, bglu,
                                   row0=seq, n_rows=n_s, tile_carry=False)
        ssm_o = jnp.concatenate([ssm_p, ssm_s, pad_rows(ssm_s)], axis=0)

        x1, xf = _outproj(fox_o, gla_o, gr, ssm_o, x, fox_norm[l].reshape(1, D_FOX),
                          jnp.tile(gla_norm[l], H_GLA).reshape(1, D_GLA), avg,
                          w_out[l].astype(bf16), g_ffn[l].reshape(1, dm))
        i1, i2, gate = _peer_topk(xf, wqt, keys)
        x = _peer_dense(xf, i1, i2, gate, u_tab, v_tab, l, x1, g_final.reshape(1, dm), l == depth - 1)

        outs_p.append((fk[:seq].reshape(bp, seq, H_FOX, FOX_DH), fv[:seq].reshape(bp, seq, H_FOX, FOX_DH),
                       lf[:seq, :H_FOX].reshape(bp, seq, H_FOX), unpack_state(st_p),
                       hre_p.reshape(bp, SSM_G, SSM_P), him_p.reshape(bp, SSM_G, SSM_P)))
        outs_s.append((fk[seq:n_tok].reshape(bd, tdec, H_FOX, FOX_DH), fv[seq:n_tok].reshape(bd, tdec, H_FOX, FOX_DH),
                       lf[seq:n_tok, :H_FOX].reshape(bd, tdec, H_FOX), unpack_state(st_s),
                       hre_s[tdec - 1::tdec].reshape(bd, SSM_G, SSM_P), him_s[tdec - 1::tdec].reshape(bd, SSM_G, SSM_P)))

    p_stk = [jnp.stack(t) for t in zip(*outs_p)]
    s_stk = [jnp.stack(t) for t in zip(*outs_s)]
    y_prompt = x[:seq].reshape(bp, seq, dm)
    y_sample = x[seq:n_tok].reshape(bd, tdec, dm)
    return (y_prompt, y_sample, *p_stk, *s_stk)
```
